```python
import jax, jax.numpy as jnp
from jax import lax
import numpy as np

D_MODEL = 2048
BATCH = 16
SEQ = 256
DEPTH = 2
DEC_BATCH = 4
DEC_SEQ = 2048
PAST_LEN = 256

GRID_W = 64
NA_HEADS = 16
NA_HEAD_DIM = 64
NA_WIDTH = NA_HEADS * NA_HEAD_DIM
NA_KH = 8
NA_KW = 16
RW_HEADS = 8
RW_HEAD_DIM = 64
RW_WIDTH = RW_HEADS * RW_HEAD_DIM
RW_DECAY_LORA = 64
RW_ICLR_LORA = 64
RW_GATE_LORA = 128
RW_SHIFT_WIDTH = 3 * RW_WIDTH + 2 * RW_DECAY_LORA + 2 * RW_ICLR_LORA + RW_GATE_LORA
RW_GN_EPS = 64e-5
HG_HEADS = 4
HG_KEY_DIM = 128
HG_VAL_DIM = 128
HG_WIDTH = HG_HEADS * HG_VAL_DIM
HG_CHUNK = 64
HG_MIN_F = 1e-6
MIX_WIDTH = NA_WIDTH + RW_WIDTH + HG_WIDTH
IN_SIZES = (NA_WIDTH, NA_WIDTH, NA_WIDTH, RW_SHIFT_WIDTH, HG_WIDTH, HG_WIDTH, HG_WIDTH, HG_WIDTH, HG_WIDTH)
IN_WIDTH = 3 * NA_WIDTH + RW_SHIFT_WIDTH + 5 * HG_WIDTH
RW_SIZES = (RW_WIDTH, RW_WIDTH, RW_WIDTH, RW_DECAY_LORA, RW_DECAY_LORA, RW_ICLR_LORA, RW_ICLR_LORA, RW_GATE_LORA)
N_EXPERTS = 32
TOP_K = 4
D_EXPERT = 2048
SWIGLU_LIMIT = 7.0
SWIGLU_ALPHA = 1.702
MOE_BLOCK = 128
CTX_Q_BLOCK = 128
RMS_EPS = 1e-6
NEG_INF = -1e30

kernel_name = 'hybrid_na_rwkv7_hgrn2_moe_dit_step'


def _split(x, sizes):
    return jnp.split(x, np.cumsum(sizes)[:-1].tolist(), axis=-1)


def rmsnorm(x, g):
    xf = x.astype(jnp.float32)
    y = xf * lax.rsqrt(jnp.mean(xf * xf, axis=-1, keepdims=True) + RMS_EPS)
    return (y * g.astype(jnp.float32)).astype(x.dtype)


def centred_shift(x):
    xp = jnp.pad(x, ((0, 0), (1, 1), (0, 0)))
    return 0.5 * (xp[:, :-2] + xp[:, 2:])


def ctx_attention(q, k, v):
    B, L, H, dh = q.shape
    nb = L // CTX_Q_BLOCK
    qb = q.reshape(B, nb, CTX_Q_BLOCK, H, dh).transpose(1, 0, 2, 3, 4)

    def one_block(qi):
        s = jnp.einsum('bqhd,bhld->bhql', qi, k).astype(jnp.float32) * (dh ** -0.5)
        p = jax.nn.softmax(s, axis=-1).astype(v.dtype)
        return jnp.einsum('bhql,bhld->bqhd', p, v)

    o = lax.map(one_block, qb)
    return o.transpose(1, 0, 2, 3, 4).reshape(B, L, H * dh)


def na_indices(rows):
    kh = min(NA_KH, rows)
    r = np.arange(rows)
    row_start = np.clip(r - kh // 2, 0, rows - kh)
    row_idx = row_start[:, None] + np.arange(kh)[None, :]
    cq = np.arange(GRID_W)
    col_start = np.clip(cq - NA_KW // 2, 0, GRID_W - NA_KW)
    ck = np.arange(GRID_W)
    col_mask = (ck[None, :] >= col_start[:, None]) & (ck[None, :] < col_start[:, None] + NA_KW)
    dr = row_idx - r[:, None] + (NA_KH - 1)
    dc = np.clip(ck[None, :] - cq[:, None], -(NA_KW - 1), NA_KW - 1) + (NA_KW - 1)
    return kh, row_idx, col_mask, dr, dc


def na_latent(q, k, v, ctx_k, ctx_v, rpb):
    B, T, H, dh = q.shape
    rows = T // GRID_W
    kh, row_idx, col_mask, dr, dc = na_indices(rows)
    scale = dh ** -0.5
    qg = q.reshape(B, rows, GRID_W, H, dh)
    kg = k.reshape(B, rows, GRID_W, H, dh)[:, row_idx]
    vg = v.reshape(B, rows, GRID_W, H, dh)[:, row_idx]
    s_win = jnp.einsum('brqhd,brikhd->bhrqik', qg, kg).astype(jnp.float32) * scale
    bias = rpb.astype(jnp.float32)[:, dr[:, None, :, None], dc[None, :, None, :]]
    s_win = jnp.where(col_mask[None, None, None, :, None, :], s_win + bias[None], NEG_INF)
    s_ctx = jnp.einsum('brqhd,bhld->bhrql', qg, ctx_k).astype(jnp.float32) * scale
    n_win = kh * GRID_W
    s = jnp.concatenate([s_win.reshape(B, H, rows, GRID_W, n_win), s_ctx], axis=-1)
    p = jax.nn.softmax(s, axis=-1).astype(v.dtype)
    p_win = p[..., :n_win].reshape(B, H, rows, GRID_W, kh, GRID_W)
    p_ctx = p[..., n_win:]
    o = jnp.einsum('bhrqik,brikhd->brqhd', p_win, vg) + jnp.einsum('bhrql,bhld->brqhd', p_ctx, ctx_v)
    return o.reshape(B, T, H * dh)


def rwkv_scan(r, w, k, v, a_vec, b_vec, s0, reverse):
    def step(S, inp):
        r_t, w_t, k_t, v_t, a_t, b_t = inp
        sa = jnp.einsum('bhvk,bhk->bhv', S, a_t)
        S = S * w_t[:, :, None, :] + sa[..., None] * b_t[:, :, None, :] + v_t[..., None] * k_t[:, :, None, :]
        return S, jnp.einsum('bhvk,bhk->bhv', S, r_t)

    xs = tuple(jnp.moveaxis(t, 1, 0) for t in (r, w, k, v, a_vec, b_vec))
    S, ys = lax.scan(step, s0.astype(jnp.float32), xs, reverse=reverse)
    return jnp.moveaxis(ys, 0, 1), S


def rwkv_mixer(xr, lp, s0_f, s0_b):
    B, T, _ = xr.shape
    xr = (xr + (centred_shift(xr) - xr) * lp['rw_mu']).astype(jnp.float32)
    r, k, v, wd_f, wd_b, ad_f, ad_b, gd = _split(xr, RW_SIZES)
    hd = lambda t: t.reshape(B, T, RW_HEADS, RW_HEAD_DIM)
    g = jax.nn.sigmoid(gd) @ lp['rw_g2']
    kk = hd(k * lp['rw_k_k'])
    kk = kk / jnp.maximum(jnp.sqrt(jnp.sum(kk * kk, axis=-1, keepdims=True)), 1e-12)
    ys, states, kds = [], [], []
    for d, (wd, ad, s0) in enumerate(((wd_f, ad_f, s0_f), (wd_b, ad_b, s0_b))):
        w = -jax.nn.softplus(-(lp['rw_w0'][d] + jnp.tanh(wd) @ lp['rw_w2'][d])) - 0.5
        a = jax.nn.sigmoid(lp['rw_a0'][d] + ad @ lp['rw_a2'][d])
        kd = hd(k * (1.0 + (a - 1.0) * lp['rw_k_a']))
        y, s = rwkv_scan(hd(r), jnp.exp(-jnp.exp(hd(w))), kd, hd(v), -kk, kk * hd(a), s0, reverse=(d == 1))
        ys.append(y)
        states.append(s)
        kds.append(kd)
    y = ys[0] + ys[1]
    mu = jnp.mean(y, axis=-1, keepdims=True)
    var = jnp.mean(jnp.square(y - mu), axis=-1, keepdims=True)
    yn = ((y - mu) * lax.rsqrt(var + RW_GN_EPS)).reshape(B, T, RW_WIDTH) * lp['rw_ln_g'] + lp['rw_ln_b']
    bonus = jnp.sum(hd(r) * 0.5 * (kds[0] + kds[1]) * lp['rw_r_k'], axis=-1, keepdims=True) * hd(v)
    out = (yn + bonus.reshape(B, T, RW_WIDTH)) * g
    return out, states[0], states[1]


def hgrn_chunk_scan(q, k, v, log_f, s0):
    B, T, H, K = q.shape
    n = T // HG_CHUNK
    to_chunks = lambda t: t.reshape(B, n, HG_CHUNK, H, t.shape[-1]).transpose(1, 0, 3, 2, 4)
    causal = np.tril(np.ones((HG_CHUNK, HG_CHUNK), dtype=bool))

    def step(S, inp):
        q_c, k_c, v_c, lf_c = inp
        b = jnp.cumsum(lf_c, axis=2)
        o_inter = jnp.einsum('bhtk,bhkv->bhtv', q_c * jnp.exp(b), S)
        diff = b[:, :, :, None, :] - b[:, :, None, :, :]
        dec = jnp.exp(jnp.where(causal[:, :, None], diff, NEG_INF))
        att = jnp.einsum('bhtk,bhsk,bhtsk->bhts', q_c, k_c, dec)
        o = o_inter + jnp.einsum('bhts,bhsv->bhtv', att, v_c)
        b_last = b[:, :, -1:, :]
        S = S * jnp.exp(b_last[:, :, 0, :, None]) + jnp.einsum('bhsk,bhsv->bhkv', k_c * jnp.exp(b_last - b), v_c)
        return S, o

    S, o = lax.scan(step, s0.astype(jnp.float32), tuple(to_chunks(t) for t in (q, k, v, log_f)))
    return o.transpose(1, 0, 3, 2, 4).reshape(B, T, H, v.shape[-1]), S


def hgrn_mixer(hq, hff, hfb, hi, hg, lp, s0_f, s0_b):
    B, T, _ = hq.shape
    heads = lambda t: t.astype(jnp.float32).reshape(B, T, HG_HEADS, -1)
    q = heads(jax.nn.silu(hq))
    v = heads(hi)
    o_sum, states = 0.0, []
    for d, (z, s0) in enumerate(((hff, s0_f), (hfb, s0_b))):
        lb = lp['hg_lb'][d]
        z = z.astype(jnp.float32)
        f = lb + (1.0 - lb) * jax.nn.sigmoid(z)
        log_f = jnp.log(jnp.maximum(f, HG_MIN_F))
        k = (1.0 - lb) * jax.nn.sigmoid(-z)
        args = (q, heads(k), v, heads(log_f))
        if d == 1:
            args = tuple(jnp.flip(t, axis=1) for t in args)
        o, s = hgrn_chunk_scan(*args, s0)
        o_sum = o_sum + (jnp.flip(o, axis=1) if d == 1 else o)
        states.append(s)
    gain = lp['hg_norm_g'].astype(jnp.float32).reshape(HG_HEADS, HG_VAL_DIM)
    on = o_sum * lax.rsqrt(jnp.mean(o_sum * o_sum, axis=-1, keepdims=True) + RMS_EPS) * gain
    g = hg.astype(jnp.float32)
    out = on.reshape(B, T, HG_WIDTH) * (g * jax.nn.sigmoid(g))
    return out, states[0], states[1]


def moe(x, lp):
    B, T, D = x.shape
    n_tok = B * T
    xt = x.reshape(n_tok, D)
    logits = (xt @ lp['router_w'] + lp['router_b']).astype(jnp.float32)
    top_v, top_e = lax.top_k(logits, TOP_K)
    gates = jax.nn.softmax(top_v, axis=-1)
    n_assign = n_tok * TOP_K
    flat_e = top_e.reshape(-1)
    order = jnp.argsort(flat_e)
    sorted_e = flat_e[order]
    counts = jnp.bincount(flat_e, length=N_EXPERTS)
    padded = (counts + MOE_BLOCK - 1) // MOE_BLOCK * MOE_BLOCK
    pad_end = jnp.cumsum(padded)
    pad_start = pad_end - padded
    start = jnp.cumsum(counts) - counts
    slot_sorted = (pad_start[sorted_e] + jnp.arange(n_assign) - start[sorted_e]).astype(jnp.int32)
    n_blocks = -(-n_assign // MOE_BLOCK) + N_EXPERTS
    n_slots = n_blocks * MOE_BLOCK
    slot_token = jnp.full((n_slots,), n_tok, jnp.int32).at[slot_sorted].set((order // TOP_K).astype(jnp.int32))
    slot_of_assign = jnp.zeros((n_assign,), jnp.int32).at[order].set(slot_sorted)
    block_expert = jnp.minimum(jnp.searchsorted(pad_end, jnp.arange(n_blocks) * MOE_BLOCK, side='right'), N_EXPERTS - 1)
    x_pad = jnp.concatenate([xt, jnp.zeros((1, D), xt.dtype)], axis=0)
    xb = x_pad[slot_token].reshape(n_blocks, MOE_BLOCK, D)

    def expert_block(args):
        xe, e = args
        gu = xe @ lp['exp_w_gu'][e] + lp['exp_b_gu'][e]
        gate, up = jnp.split(gu, 2, axis=-1)
        gate = jnp.minimum(gate, SWIGLU_LIMIT)
        up = jnp.clip(up, -SWIGLU_LIMIT, SWIGLU_LIMIT)
        act = (up + 1.0) * gate * jax.nn.sigmoid(SWIGLU_ALPHA * gate)
        return act @ lp['exp_w_down'][e] + lp['exp_b_down'][e]

    yb = lax.map(expert_block, (xb, block_expert)).reshape(n_slots, D)
    y = yb[slot_of_assign].reshape(n_tok, TOP_K, D)
    return jnp.einsum('nkd,nk->nd', y, gates.astype(y.dtype)).reshape(B, T, D)


def mixer(h, lp, cache):
    B, T, _ = h.shape
    na_q, na_k, na_v, rw_x, hg_q, hg_ff, hg_fb, hg_i, hg_g = _split(h @ lp['w_in'], IN_SIZES)
    q, k, v = (t.reshape(B, T, NA_HEADS, NA_HEAD_DIM) for t in (na_q, na_k, na_v))
    if cache is None:
        k_ctx, v_ctx = k.transpose(0, 2, 1, 3), v.transpose(0, 2, 1, 3)
        na_o = ctx_attention(q, k_ctx, v_ctx)
        zr = jnp.zeros((B, RW_HEADS, RW_HEAD_DIM, RW_HEAD_DIM), jnp.float32)
        zh = jnp.zeros((B, HG_HEADS, HG_KEY_DIM, HG_VAL_DIM), jnp.float32)
        rw_s0, hg_s0 = (zr, zr), (zh, zh)
    else:
        ck, cv, rwf, rwb, hgf, hgb = cache
        na_o = na_latent(q, k, v, ck, cv, lp['na_rpb'])
        rw_s0, hg_s0 = (rwf, rwb), (hgf, hgb)
    rw_o, rw_sf, rw_sb = rwkv_mixer(rw_x, lp, *rw_s0)
    hg_o, hg_sf, hg_sb = hgrn_mixer(hg_q, hg_ff, hg_fb, hg_i, hg_g, lp, *hg_s0)
    out = jnp.concatenate([na_o, rw_o.astype(h.dtype), hg_o.astype(h.dtype)], axis=-1) @ lp['w_out']
    if cache is None:
        return out, (k_ctx, v_ctx, rw_sf, rw_sb, hg_sf, hg_sb)
    return out, None


def block(x, cond, lp, cache):
    mod = (jax.nn.silu(cond) @ lp['w_ada'] + lp['b_ada'])[..., None, :]
    sh1, sc1, g1, sh2, sc2, g2 = jnp.split(mod, 6, axis=-1)
    h = rmsnorm(x, lp['norm1_g']) * (1.0 + sc1) + sh1
    mix, ctx_tensors = mixer(h, lp, cache)
    x = x + g1 * mix
    h = rmsnorm(x, lp['norm2_g']) * (1.0 + sc2) + sh2
    x = x + g2 * moe(h, lp)
    return x, ctx_tensors


def setup_inputs(seed: int = 0) -> dict:
    key = jax.random.key(seed)
    ks = jax.random.split(key, 40)
    nrm = lambda i, shape, s: s * jax.random.normal(ks[i], shape, jnp.float32)
    uni = lambda i, shape, lo, hi: jax.random.uniform(ks[i], shape, jnp.float32, lo, hi)
    D = D_MODEL
    return {
        'x_prompt': nrm(0, (BATCH, SEQ, D), 1.0),
        'x_sample': nrm(1, (DEC_BATCH, DEC_SEQ, D), 1.0),
        'c': nrm(2, (DEC_BATCH, D), 1.0),
        'cache_k': nrm(3, (DEC_BATCH, DEPTH, NA_HEADS, PAST_LEN, NA_HEAD_DIM), 1.0),
        'cache_v': nrm(4, (DEC_BATCH, DEPTH, NA_HEADS, PAST_LEN, NA_HEAD_DIM), 1.0),
        'state_rwkv_fwd': nrm(5, (DEC_BATCH, DEPTH, RW_HEADS, RW_HEAD_DIM, RW_HEAD_DIM), 0.3),
        'state_rwkv_bwd': nrm(6, (DEC_BATCH, DEPTH, RW_HEADS, RW_HEAD_DIM, RW_HEAD_DIM), 0.3),
        'state_hgrn_fwd': nrm(7, (DEC_BATCH, DEPTH, HG_HEADS, HG_KEY_DIM, HG_VAL_DIM), 0.3),
        'state_hgrn_bwd': nrm(8, (DEC_BATCH, DEPTH, HG_HEADS, HG_KEY_DIM, HG_VAL_DIM), 0.3),
        'c_ctx': nrm(9, (D,), 1.0),
        'norm1_g': 1.0 + nrm(10, (DEPTH, D), 0.02),
        'norm2_g': 1.0 + nrm(11, (DEPTH, D), 0.02),
        'w_ada': nrm(12, (DEPTH, D, 6 * D), 0.5 * D ** -0.5),
        'b_ada': nrm(13, (DEPTH, 6 * D), 0.01),
        'w_in': nrm(14, (DEPTH, D, IN_WIDTH), D ** -0.5),
        'w_out': nrm(15, (DEPTH, MIX_WIDTH, D), MIX_WIDTH ** -0.5),
        'na_rpb': nrm(16, (DEPTH, NA_HEADS, 2 * NA_KH - 1, 2 * NA_KW - 1), 0.1),
        'rw_mu': uni(17, (DEPTH, RW_SHIFT_WIDTH), 0.0, 1.0),
        'rw_w0': uni(18, (DEPTH, 2, RW_WIDTH), -3.0, 1.0),
        'rw_w2': nrm(19, (DEPTH, 2, RW_DECAY_LORA, RW_WIDTH), 0.1 * RW_DECAY_LORA ** -0.5),
        'rw_a0': nrm(20, (DEPTH, 2, RW_WIDTH), 0.1),
        'rw_a2': nrm(21, (DEPTH, 2, RW_ICLR_LORA, RW_WIDTH), 0.1 * RW_ICLR_LORA ** -0.5),
        'rw_g2': nrm(22, (DEPTH, RW_GATE_LORA, RW_WIDTH), RW_GATE_LORA ** -0.5),
        'rw_k_k': 0.85 + nrm(23, (DEPTH, RW_WIDTH), 0.1),
        'rw_k_a': 1.0 + nrm(24, (DEPTH, RW_WIDTH), 0.1),
        'rw_r_k': nrm(25, (DEPTH, RW_HEADS, RW_HEAD_DIM), 0.1),
        'rw_ln_g': 1.0 + nrm(26, (DEPTH, RW_WIDTH), 0.02),
        'rw_ln_b': nrm(27, (DEPTH, RW_WIDTH), 0.01),
        'hg_lb_raw': nrm(28, (DEPTH, 2, HG_WIDTH), 0.5),
        'hg_norm_g': 1.0 + nrm(29, (DEPTH, HG_WIDTH), 0.02),
        'router_w': nrm(30, (DEPTH, D, N_EXPERTS), D ** -0.5),
        'router_b': nrm(31, (DEPTH, N_EXPERTS), 0.01),
        'exp_w_gu': nrm(32, (DEPTH, N_EXPERTS, D, 2 * D_EXPERT), D ** -0.5),
        'exp_b_gu': nrm(33, (DEPTH, N_EXPERTS, 2 * D_EXPERT), 0.01),
        'exp_w_down': nrm(34, (DEPTH, N_EXPERTS, D_EXPERT, D), D_EXPERT ** -0.5),
        'exp_b_down': nrm(35, (DEPTH, N_EXPERTS, D), 0.01),
        'final_g': 1.0 + nrm(36, (D,), 0.02),
    }


def reference(x_prompt, x_sample, c, cache_k, cache_v, state_rwkv_fwd, state_rwkv_bwd, state_hgrn_fwd, state_hgrn_bwd,
              c_ctx, norm1_g, norm2_g, w_ada, b_ada, w_in, w_out, na_rpb, rw_mu, rw_w0, rw_w2, rw_a0, rw_a2, rw_g2,
              rw_k_k, rw_k_a, rw_r_k, rw_ln_g, rw_ln_b, hg_lb_raw, hg_norm_g, router_w, router_b,
              exp_w_gu, exp_b_gu, exp_w_down, exp_b_down, final_g):
    p_lb = jax.nn.softmax(hg_lb_raw.astype(jnp.float32), axis=0)
    hg_lb = jnp.cumsum(p_lb, axis=0) - p_lb[:1]
    xp, xs = x_prompt, x_sample
    ctx_layers = []
    for l in range(DEPTH):
        lp = {
            'norm1_g': norm1_g[l], 'norm2_g': norm2_g[l], 'w_ada': w_ada[l], 'b_ada': b_ada[l],
            'w_in': w_in[l], 'w_out': w_out[l], 'na_rpb': na_rpb[l],
            'rw_mu': rw_mu[l], 'rw_w0': rw_w0[l], 'rw_w2': rw_w2[l], 'rw_a0': rw_a0[l], 'rw_a2': rw_a2[l],
            'rw_g2': rw_g2[l], 'rw_k_k': rw_k_k[l], 'rw_k_a': rw_k_a[l], 'rw_r_k': rw_r_k[l],
            'rw_ln_g': rw_ln_g[l], 'rw_ln_b': rw_ln_b[l], 'hg_lb': hg_lb[l], 'hg_norm_g': hg_norm_g[l],
            'router_w': router_w[l], 'router_b': router_b[l], 'exp_w_gu': exp_w_gu[l], 'exp_b_gu': exp_b_gu[l],
            'exp_w_down': exp_w_down[l], 'exp_b_down': exp_b_down[l],
        }
        xp, ctx_l = block(xp, c_ctx, lp, None)
        ctx_layers.append(ctx_l)
        cache_l = (cache_k[:, l], cache_v[:, l], state_rwkv_fwd[:, l], state_rwkv_bwd[:, l],
                   state_hgrn_fwd[:, l], state_hgrn_bwd[:, l])
        xs, _ = block(xs, c, lp, cache_l)
    y_prompt = rmsnorm(xp, final_g)
    y_sample = rmsnorm(xs, final_g)
    new_cache_k = jnp.stack([t[0] for t in ctx_layers], axis=1)
    new_cache_v = jnp.stack([t[1] for t in ctx_layers], axis=1)
    new_state_rwkv_fwd = jnp.stack([t[2] for t in ctx_layers], axis=1)
    new_state_rwkv_bwd = jnp.stack([t[3] for t in ctx_layers], axis=1)
    new_state_hgrn_fwd = jnp.stack([t[4] for t in ctx_layers], axis=1)
    new_state_hgrn_bwd = jnp.stack([t[5] for t in ctx_layers], axis=1)
    return (y_prompt, y_sample, new_cache_k, new_cache_v, new_state_rwkv_fwd, new_state_rwkv_bwd, new_state_hgrn_fwd, new_state_hgrn_bwd)
```

```python
import functools

import numpy as np
import jax
import jax.numpy as jnp
from jax import lax
from jax.experimental import pallas as pl
from jax.experimental.pallas import tpu as pltpu

F32 = jnp.float32
BF16 = jnp.bfloat16

D_MODEL = 2048
GRID_W = 64
NA_HEADS = 16
NA_HEAD_DIM = 64
NA_WIDTH = NA_HEADS * NA_HEAD_DIM
NA_KH = 8
NA_KW = 16
RW_HEADS = 8
RW_HEAD_DIM = 64
RW_WIDTH = RW_HEADS * RW_HEAD_DIM
RW_DECAY_LORA = 64
RW_ICLR_LORA = 64
RW_GATE_LORA = 128
RW_SHIFT_WIDTH = 3 * RW_WIDTH + 2 * RW_DECAY_LORA + 2 * RW_ICLR_LORA + RW_GATE_LORA
RW_GN_EPS = 64e-5
HG_HEADS = 4
HG_KEY_DIM = 128
HG_VAL_DIM = 128
HG_WIDTH = HG_HEADS * HG_VAL_DIM
HG_MIN_F = 1e-6
IN_WIDTH = 3 * NA_WIDTH + RW_SHIFT_WIDTH + 5 * HG_WIDTH
N_EXPERTS = 32
TOP_K = 4
D_EXPERT = 2048
SWIGLU_LIMIT = 7.0
SWIGLU_ALPHA = 1.702
RMS_EPS = 1e-6
NEG_INF = -1e30

LANES = 128
PCOL_Q = 0
PCOL_K = NA_WIDTH
PCOL_V = 2 * NA_WIDTH
PCOL_RW = 3 * NA_WIDTH
RW_PAD = -RW_SHIFT_WIDTH % HG_WIDTH
PCOL_HQ = PCOL_RW + RW_SHIFT_WIDTH + RW_PAD
PCOL_HFF = PCOL_HQ + HG_WIDTH
PCOL_HFB = PCOL_HFF + HG_WIDTH
PCOL_HI = PCOL_HFB + HG_WIDTH
PCOL_HG = PCOL_HI + HG_WIDTH
IN_WIDTH_PAD = PCOL_HG + HG_WIDTH
SCAN_CHUNK = 64
TOKEN_TILE = 256
VMEM_LIMIT = 56 * 1024 * 1024


def _cparams(n_axes, vmem=VMEM_LIMIT):
    return pltpu.CompilerParams(dimension_semantics=("arbitrary",) * n_axes, vmem_limit_bytes=vmem)


def _dot(a, b):
    return jnp.dot(a.astype(BF16), b.astype(BF16), preferred_element_type=F32)


def _dot_nt(a, b):
    return lax.dot_general(a.astype(BF16), b.astype(BF16), (((1,), (1,)), ((), ())), preferred_element_type=F32)


def _dot_tn(a, b):
    return lax.dot_general(a.astype(BF16), b.astype(BF16), (((0,), (0,)), ((), ())), preferred_element_type=F32)


def _split3(x):
    hi = x.astype(BF16)
    r1 = x - hi.astype(F32)
    mid = r1.astype(BF16)
    lo = (r1 - mid.astype(F32)).astype(BF16)
    return hi, mid, lo


def _dot01_left(m01, x):
    hi, mid, lo = _split3(x)
    d = lambda p: jnp.dot(m01, p, preferred_element_type=F32)
    return d(hi) + (d(mid) + d(lo))


def _dot01_right(x, m01):
    hi, mid, lo = _split3(x)
    d = lambda p: jnp.dot(p, m01, preferred_element_type=F32)
    return d(hi) + (d(mid) + d(lo))


def _sigmoid(x):
    return 1.0 / (1.0 + jnp.exp(-x))


def _softplus(x):
    return jnp.maximum(x, 0.0) + jnp.log(1.0 + jnp.exp(-jnp.abs(x)))


def _head_sum_matrix(width, head_dim):
    i = np.arange(width)
    return jnp.asarray((i[:, None] // head_dim) == (i[None, :] // head_dim), dtype=BF16)


def _order_masks(n, reverse):
    row = lax.broadcasted_iota(jnp.int32, (n, n), 0)
    col = lax.broadcasted_iota(jnp.int32, (n, n), 1)
    diff = jnp.where(reverse, col - row, row - col)
    return diff >= 0, diff > 0


def _rwkv_prep_kernel(x_ref, w0_ref, w2_ref, a0_ref, a2_ref, g2_ref, kkw_ref, kaw_ref, hsum_ref,
                      kkn_ref, g_ref, lw_ref, kd_ref, av_ref):
    x = x_ref[...]
    k = x[:, RW_WIDTH:2 * RW_WIDTH]
    o = 3 * RW_WIDTH
    wd = (x[:, o:o + 64], x[:, o + 64:o + 128])
    ad = (x[:, o + 128:o + 192], x[:, o + 192:o + 256])
    gd = x[:, o + 256:o + 384]
    g_ref[...] = _dot(_sigmoid(gd), g2_ref[...])
    kk = k * kkw_ref[...]
    ssq = _dot01_right(kk * kk, hsum_ref[...])
    kkn_ref[...] = kk / jnp.maximum(jnp.sqrt(ssq), 1e-12)
    for d in range(2):
        wl = -_softplus(-(w0_ref[d] + _dot(jnp.tanh(wd[d]), w2_ref[d]))) - 0.5
        lw_ref[d] = -jnp.exp(wl)
        a = _sigmoid(a0_ref[d] + _dot(ad[d], a2_ref[d]))
        kd_ref[d] = k * (1.0 + (a - 1.0) * kaw_ref[...])
        av_ref[d] = a


def _rwkv_scan_kernel(r_ref, v_ref, kkn_ref, lw_ref, kd_ref, av_ref, s0_ref, y_ref, sout_ref, s_scr, *, n_chunks):
    reverse = pl.program_id(1) == 1
    c = pl.program_id(2)
    n = SCAN_CHUNK

    @pl.when(c == 0)
    def _():
        s_scr[...] = s0_ref[...]

    m_incl, m_strict = _order_masks(n, reverse)
    m_incl01 = m_incl.astype(BF16)
    ys = []
    for h in range(RW_HEADS):
        sl = slice(h * RW_HEAD_DIM, (h + 1) * RW_HEAD_DIM)
        r, v, kkn = r_ref[:, sl], v_ref[:, sl], kkn_ref[:, sl]
        lw, kd, av = lw_ref[:, sl], kd_ref[:, sl], av_ref[:, sl]
        cum = _dot01_left(m_incl01, lw)
        p_tot = jnp.exp(jnp.sum(lw, axis=0, keepdims=True))
        p_inv = jnp.exp(-cum)
        a_t = -kkn * jnp.exp(cum - lw)
        b_t = kkn * av * p_inv
        k_t = kd * p_inv
        r_t = r * jnp.exp(cum)
        a_ab = jnp.where(m_strict, _dot_nt(a_t, b_t), 0.0)
        a_ak = jnp.where(m_strict, _dot_nt(a_t, k_t), 0.0)
        a_rb = jnp.where(m_incl, _dot_nt(r_t, b_t), 0.0)
        a_rk = jnp.where(m_incl, _dot_nt(r_t, k_t), 0.0)
        t_m = a_ab
        n_pow = a_ab
        for _ in range(int(np.log2(n)) - 1):
            n_pow = _dot(n_pow, n_pow)
            t_m = t_m + n_pow + _dot(t_m, n_pow)
        x1 = _dot(a_ak, v)
        w1 = x1 + _dot(t_m, x1)
        w2 = a_t + _dot(t_m, a_t)
        y0 = _dot(a_rk, v) + _dot(a_rb, w1)
        q = r_t + _dot(a_rb, w2)
        g_m = _dot_tn(w2, b_t)
        h_m = _dot_tn(w1, b_t) + _dot_tn(v, k_t)
        s0 = s_scr[h]
        ys.append(y0 + _dot_nt(q, s0))
        s_scr[h] = (s0 + _dot(s0, g_m) + h_m) * p_tot
    y_ref[...] = jnp.concatenate(ys, axis=-1)

    @pl.when(c == n_chunks - 1)
    def _():
        sout_ref[...] = s_scr[...]


def _rwkv_post_kernel(y_ref, x_ref, kd_ref, g_ref, rk_ref, lng_ref, lnb_ref, hsum_ref, o_ref):
    x = x_ref[...]
    r = x[:, 0:RW_WIDTH]
    v = x[:, 2 * RW_WIDTH:3 * RW_WIDTH]
    y = y_ref[0] + y_ref[1]
    hsum = hsum_ref[...]
    inv_n = 1.0 / RW_HEAD_DIM
    mu = _dot01_right(y, hsum) * inv_n
    yc = y - mu
    var = _dot01_right(yc * yc, hsum) * inv_n
    yn = yc * lax.rsqrt(var + RW_GN_EPS) * lng_ref[...] + lnb_ref[...]
    bonus = _dot01_right(r * (0.5 * (kd_ref[0] + kd_ref[1])) * rk_ref[...], hsum) * v
    o_ref[...] = (yn + bonus) * g_ref[...]


def _rwkv_mixer(xs, groups, lw):
    m = xs.shape[0]
    tm = TOKEN_TILE
    hsum = _head_sum_matrix(RW_WIDTH, RW_HEAD_DIM)
    row = lambda a: a.reshape(1, -1)
    full = lambda shape: pl.BlockSpec(shape, lambda i: (0,) * len(shape))
    tok = lambda w: pl.BlockSpec((tm, w), lambda i: (i, 0))
    tok2 = pl.BlockSpec((2, tm, RW_WIDTH), lambda i: (0, i, 0))
    d2 = jax.ShapeDtypeStruct((2, m, RW_WIDTH), F32)
    d1 = jax.ShapeDtypeStruct((m, RW_WIDTH), F32)
    kkn, g, lwd, kd, av = pl.pallas_call(
        _rwkv_prep_kernel,
        grid=(m // tm,),
        in_specs=[tok(RW_SHIFT_WIDTH), full((2, 1, RW_WIDTH)), full((2, RW_DECAY_LORA, RW_WIDTH)),
                  full((2, 1, RW_WIDTH)), full((2, RW_ICLR_LORA, RW_WIDTH)), full((RW_GATE_LORA, RW_WIDTH)),
                  full((1, RW_WIDTH)), full((1, RW_WIDTH)), full((RW_WIDTH, RW_WIDTH))],
        out_specs=[tok(RW_WIDTH), tok(RW_WIDTH), tok2, tok2, tok2],
        out_shape=[d1, d1, d2, d2, d2],
        compiler_params=_cparams(1),
        name="rwkv_prep",
    )(xs, lw['rw_w0'].reshape(2, 1, RW_WIDTH), lw['rw_w2'], lw['rw_a0'].reshape(2, 1, RW_WIDTH), lw['rw_a2'],
      lw['rw_g2'], row(lw['rw_k_k']), row(lw['rw_k_a']), hsum)

    n = SCAN_CHUNK
    ys, states = [], []
    for row_off, n_seq, seq_len, s0 in groups:
        nc = seq_len // n
        base = row_off // n

        def tmap(b, d, c, col=0, nc=nc, base=base):
            return (base + b * nc + c + d * (nc - 1 - 2 * c), col)

        def tmap2(b, d, c, nc=nc, base=base):
            return (d, base + b * nc + c + d * (nc - 1 - 2 * c), 0)

        x_spec = lambda col: pl.BlockSpec((n, RW_WIDTH), functools.partial(tmap, col=col))
        d_spec = pl.BlockSpec((None, n, RW_WIDTH), tmap2)
        y_spec = pl.BlockSpec((None, n, RW_WIDTH), functools.partial(tmap2, base=0))
        s_spec = pl.BlockSpec((None, None, RW_HEADS, RW_HEAD_DIM, RW_HEAD_DIM), lambda b, d, c: (d, b, 0, 0, 0))
        y, s_out = pl.pallas_call(
            functools.partial(_rwkv_scan_kernel, n_chunks=nc),
            grid=(n_seq, 2, nc),
            in_specs=[x_spec(0), x_spec(2), x_spec(0), d_spec, d_spec, d_spec, s_spec],
            out_specs=[y_spec, s_spec],
            out_shape=[jax.ShapeDtypeStruct((2, n_seq * seq_len, RW_WIDTH), F32),
                       jax.ShapeDtypeStruct((2, n_seq, RW_HEADS, RW_HEAD_DIM, RW_HEAD_DIM), F32)],
            scratch_shapes=[pltpu.VMEM((RW_HEADS, RW_HEAD_DIM, RW_HEAD_DIM), F32)],
            compiler_params=_cparams(3),
            name="rwkv_scan",
        )(xs, xs, kkn, lwd, kd, av, s0)
        ys.append(y)
        states.append(s_out)
    y = jnp.concatenate(ys, axis=1)

    out = pl.pallas_call(
        _rwkv_post_kernel,
        grid=(m // tm,),
        in_specs=[tok2, tok(RW_SHIFT_WIDTH), tok2, tok(RW_WIDTH), full((1, RW_WIDTH)), full((1, RW_WIDTH)),
                  full((1, RW_WIDTH)), full((RW_WIDTH, RW_WIDTH))],
        out_specs=tok(RW_WIDTH),
        out_shape=d1,
        compiler_params=_cparams(1),
        name="rwkv_post",
    )(y, xs, kd, g, row(lw['rw_r_k']), row(lw['rw_ln_g']), row(lw['rw_ln_b']), hsum)
    return out, states


HG_LEVELS = (32, 16, 8, 4, 2, 1)


def _hgrn_consts():
    n = SCAN_CHUNK
    cums, masks = [], []
    for rev in (False, True):
        p = np.arange(n)[::-1] if rev else np.arange(n)
        pt, pj = p[:, None], p[None, :]
        rows = [pj <= pt, pj > pt]
        rows += [((pt // m) % 2 == 1) & (pj // m == pt // m) & (pj <= pt) for m in HG_LEVELS]
        rows += [((pt // m) % 2 == 0) & (pj // m == pt // m) & (pj > pt) for m in HG_LEVELS]
        cums.append(np.concatenate(rows, 0))
        mk = [pt == pj]
        mk += [(pt // (2 * m) == pj // (2 * m)) & ((pt // m) % 2 == 1) & ((pj // m) % 2 == 0) for m in HG_LEVELS]
        masks.append(np.stack(mk))
    return jnp.asarray(np.stack(cums), BF16), jnp.asarray(np.stack(masks), F32)


def _hgrn_scan_kernel(q_ref, z_ref, v_ref, lb_ref, cm_ref, mask_ref, s0_ref, o_ref, sout_ref, s_scr, *, n_chunks):
    c = pl.program_id(2)
    n = SCAN_CHUNK
    nl = len(HG_LEVELS)

    @pl.when(c == 0)
    def _():
        s_scr[...] = s0_ref[...]

    cm = cm_ref[...]
    ones = jnp.ones((n, HG_VAL_DIM), BF16)
    outs = []
    for h in range(HG_HEADS):
        sl = slice(h * HG_KEY_DIM, (h + 1) * HG_KEY_DIM)
        hq, z, v, lb = q_ref[:, sl], z_ref[:, sl], v_ref[:, sl], lb_ref[:, sl]
        q = hq * _sigmoid(hq)
        f = lb + (1.0 - lb) * _sigmoid(z)
        lf = jnp.log(jnp.maximum(f, HG_MIN_F))
        k = (1.0 - lb) * _sigmoid(-z)
        ex = jnp.exp(_dot01_left(cm, lf))
        e_b, e_tail = ex[0:n], ex[n:2 * n]
        att = mask_ref[0] * _dot_nt(q, k)
        for i in range(nl):
            e_q = ex[(2 + i) * n:(3 + i) * n]
            e_k = ex[(2 + nl + i) * n:(3 + nl + i) * n]
            att = att + mask_ref[i + 1] * _dot_nt(q * e_q, k * e_k)
        s0 = s_scr[h]
        outs.append(_dot(q * e_b, s0) + _dot(att, v))
        hi, mid, lo = _split3(lf)
        tn = lambda p: lax.dot_general(p, ones, (((0,), (0,)), ((), ())), preferred_element_type=F32)
        p_col = jnp.exp(tn(hi) + (tn(mid) + tn(lo)))
        s_scr[h] = s0 * p_col + _dot_tn(k * e_tail, v)
    o_ref[...] = jnp.concatenate(outs, axis=-1)

    @pl.when(c == n_chunks - 1)
    def _():
        sout_ref[...] = s_scr[...]


def _hgrn_post_kernel(o_ref, g_ref, gain_ref, hsum_ref, out_ref):
    o = o_ref[0] + o_ref[1]
    ms = _dot01_right(o * o, hsum_ref[...]) * (1.0 / HG_VAL_DIM)
    on = o * lax.rsqrt(ms + RMS_EPS) * gain_ref[...]
    g = g_ref[...]
    out_ref[...] = on * (g * _sigmoid(g))


def _hgrn_mixer(proj, groups, lb, gain):
    m = proj.shape[0]
    n = SCAN_CHUNK
    cm, masks = _hgrn_consts()
    cq, cff, ci, cg = (col // HG_WIDTH for col in (PCOL_HQ, PCOL_HFF, PCOL_HI, PCOL_HG))
    os_, states = [], []
    for row_off, n_seq, seq_len, s0 in groups:
        nc = seq_len // n
        base = row_off // n

        def tmap(b, d, c, col=0, per_dir=0, nc=nc, base=base):
            return (base + b * nc + c + d * (nc - 1 - 2 * c), col + per_dir * d)

        x_spec = lambda col, per_dir=0: pl.BlockSpec((n, HG_WIDTH), functools.partial(tmap, col=col, per_dir=per_dir))
        o_spec = pl.BlockSpec((None, n, HG_WIDTH), lambda b, d, c, nc=nc: (d, b * nc + c + d * (nc - 1 - 2 * c), 0))
        s_spec = pl.BlockSpec((None, None, HG_HEADS, HG_KEY_DIM, HG_VAL_DIM), lambda b, d, c: (d, b, 0, 0, 0))
        o, s_out = pl.pallas_call(
            functools.partial(_hgrn_scan_kernel, n_chunks=nc),
            grid=(n_seq, 2, nc),
            in_specs=[x_spec(cq), x_spec(cff, 1), x_spec(ci),
                      pl.BlockSpec((None, 1, HG_WIDTH), lambda b, d, c: (d, 0, 0)),
                      pl.BlockSpec((None,) + cm.shape[1:], lambda b, d, c: (d, 0, 0)),
                      pl.BlockSpec((None,) + masks.shape[1:], lambda b, d, c: (d, 0, 0, 0)),
                      s_spec],
            out_specs=[o_spec, s_spec],
            out_shape=[jax.ShapeDtypeStruct((2, n_seq * seq_len, HG_WIDTH), F32),
                       jax.ShapeDtypeStruct((2, n_seq, HG_HEADS, HG_KEY_DIM, HG_VAL_DIM), F32)],
            scratch_shapes=[pltpu.VMEM((HG_HEADS, HG_KEY_DIM, HG_VAL_DIM), F32)],
            compiler_params=_cparams(3),
            name="hgrn_scan",
        )(proj, proj, proj, lb.reshape(2, 1, HG_WIDTH), cm, masks, s0)
        os_.append(o)
        states.append(s_out)
    o = jnp.concatenate(os_, axis=1)

    tm = TOKEN_TILE
    out = pl.pallas_call(
        _hgrn_post_kernel,
        grid=(m // tm,),
        in_specs=[pl.BlockSpec((2, tm, HG_WIDTH), lambda i: (0, i, 0)),
                  pl.BlockSpec((tm, HG_WIDTH), lambda i: (i, cg)),
                  pl.BlockSpec((1, HG_WIDTH), lambda i: (0, 0)),
                  pl.BlockSpec((HG_WIDTH, HG_WIDTH), lambda i: (0, 0))],
        out_specs=pl.BlockSpec((tm, HG_WIDTH), lambda i: (i, 0)),
        out_shape=jax.ShapeDtypeStruct((m, HG_WIDTH), F32),
        compiler_params=_cparams(1),
        name="hgrn_post",
    )(o, proj, gain.reshape(1, HG_WIDTH), _head_sum_matrix(HG_WIDTH, HG_VAL_DIM))
    return out, states


NA_HEADS_PER_STEP = LANES // NA_HEAD_DIM


def _ctx_attn_kernel(q_ref, k_ref, v_ref, o_ref, kc_ref, vc_ref):
    scale = NA_HEAD_DIM ** -0.5
    outs = []
    for h in range(NA_HEADS_PER_STEP):
        sl = slice(h * NA_HEAD_DIM, (h + 1) * NA_HEAD_DIM)
        q, k, v = q_ref[:, sl], k_ref[:, sl], v_ref[:, sl]
        kc_ref[h] = k
        vc_ref[h] = v
        s = _dot_nt(q, k) * scale
        p = jnp.exp(s - jnp.max(s, axis=-1, keepdims=True))
        outs.append(_dot(p, v) / jnp.sum(p, axis=-1, keepdims=True))
    o_ref[...] = jnp.concatenate(outs, axis=-1)


def _ctx_attention(proj, n_seq, seq_len):
    nb = NA_WIDTH // LANES
    blk = lambda off: pl.BlockSpec((seq_len, LANES), lambda b, j: (b, off + j))
    c_spec = pl.BlockSpec((None, NA_HEADS_PER_STEP, seq_len, NA_HEAD_DIM), lambda b, j: (b, j, 0, 0))
    c_shape = jax.ShapeDtypeStruct((n_seq, NA_HEADS, seq_len, NA_HEAD_DIM), F32)
    return pl.pallas_call(
        _ctx_attn_kernel,
        grid=(n_seq, nb),
        in_specs=[blk(PCOL_Q // LANES), blk(PCOL_K // LANES), blk(PCOL_V // LANES)],
        out_specs=[pl.BlockSpec((seq_len, LANES), lambda b, j: (b, j)), c_spec, c_spec],
        out_shape=[jax.ShapeDtypeStruct((n_seq * seq_len, NA_WIDTH), F32), c_shape, c_shape],
        compiler_params=_cparams(2),
        name="ctx_attention",
    )(proj, proj, proj)


def _na_bias_table(rpb, rows):
    assert rows >= NA_KH
    o = np.arange(NA_KH)[:, None]
    i = np.arange(NA_KH)[None, :]
    dr = i - o + (NA_KH - 1)
    cq = np.arange(GRID_W)
    ck = np.arange(GRID_W)
    col_start = np.clip(cq - NA_KW // 2, 0, GRID_W - NA_KW)
    col_mask = (ck[None, :] >= col_start[:, None]) & (ck[None, :] < col_start[:, None] + NA_KW)
    dc = np.clip(ck[None, :] - cq[:, None], -(NA_KW - 1), NA_KW - 1) + (NA_KW - 1)
    b = rpb.astype(F32)[:, dr[:, None, :, None], dc[None, :, None, :]]
    b = jnp.where(col_mask[None, None, :, None, :], b, NEG_INF)
    return b.transpose(1, 0, 2, 3, 4).reshape(NA_KH, rpb.shape[0], GRID_W, NA_KH * GRID_W)


def _na_kernel(q_ref, k_ref, v_ref, ck_ref, cv_ref, bias_ref, o_ref, *, rows):
    scale = NA_HEAD_DIM ** -0.5
    n_win = NA_KH * GRID_W

    def one_row(r, carry):
        r0 = jnp.clip(r - NA_KH // 2, 0, rows - NA_KH)
        off = r - r0
        q_at = pl.multiple_of(r * GRID_W, GRID_W)
        k_at = pl.multiple_of(r0 * GRID_W, GRID_W)
        for h in range(NA_HEADS_PER_STEP):
            sl = slice(h * NA_HEAD_DIM, (h + 1) * NA_HEAD_DIM)
            q = q_ref[pl.ds(q_at, GRID_W), sl]
            kw = k_ref[pl.ds(k_at, n_win), sl]
            vw = v_ref[pl.ds(k_at, n_win), sl]
            s_w = _dot_nt(q, kw) * scale + bias_ref[off, h]
            s_c = _dot_nt(q, ck_ref[h]) * scale
            m = jnp.maximum(jnp.max(s_w, axis=-1, keepdims=True), jnp.max(s_c, axis=-1, keepdims=True))
            p_w = jnp.exp(s_w - m)
            p_c = jnp.exp(s_c - m)
            den = jnp.sum(p_w, axis=-1, keepdims=True) + jnp.sum(p_c, axis=-1, keepdims=True)
            o_ref[pl.ds(q_at, GRID_W), sl] = (_dot(p_w, vw) + _dot(p_c, cv_ref[h])) / den
        return carry

    lax.fori_loop(0, rows, one_row, 0)


def _na_attention(proj, row_off, n_seq, seq_len, cache_k, cache_v, layer, bias):
    assert row_off % seq_len == 0
    nb = NA_WIDTH // LANES
    rows = seq_len // GRID_W
    rb = row_off // seq_len
    past = cache_k.shape[3]
    blk = lambda off: pl.BlockSpec((seq_len, LANES), lambda b, j: (rb + b, off + j))
    c_spec = pl.BlockSpec((None, None, NA_HEADS_PER_STEP, past, NA_HEAD_DIM), lambda b, j: (b, layer, j, 0, 0))
    return pl.pallas_call(
        functools.partial(_na_kernel, rows=rows),
        grid=(n_seq, nb),
        in_specs=[blk(PCOL_Q // LANES), blk(PCOL_K // LANES), blk(PCOL_V // LANES), c_spec, c_spec,
                  pl.BlockSpec((NA_KH, NA_HEADS_PER_STEP, GRID_W, NA_KH * GRID_W), lambda b, j: (0, j, 0, 0))],
        out_specs=pl.BlockSpec((seq_len, LANES), lambda b, j: (b, j)),
        out_shape=jax.ShapeDtypeStruct((n_seq * seq_len, NA_WIDTH), F32),
        compiler_params=_cparams(2),
        name="na_attention",
    )(proj, proj, proj, cache_k, cache_v, bias)


ADA_TN = 1024
IN_TM = 512
IN_TN = 1536
OUT_TM = 256


def _mod_row_fn(tm, n_ctx_rows, latent_len):
    def f(i):
        start = i * tm
        return jnp.where(start < n_ctx_rows, 0, 1 + (start - n_ctx_rows) // latent_len)
    return f


def _ada_kernel(c_ref, w_ref, b_ref, o_ref):
    x = c_ref[...]
    o_ref[...] = _dot(x * _sigmoid(x), w_ref[...]) + b_ref[...]


def _ada(cond, w_ada, b_ada, layer):
    n_rows, d = cond.shape
    n = w_ada.shape[2]
    return pl.pallas_call(
        _ada_kernel,
        grid=(n // ADA_TN,),
        in_specs=[pl.BlockSpec((n_rows, d), lambda j: (0, 0)),
                  pl.BlockSpec((None, d, ADA_TN), lambda j: (layer, 0, j)),
                  pl.BlockSpec((None, 1, ADA_TN), lambda j: (layer, 0, j))],
        out_specs=pl.BlockSpec((n_rows, ADA_TN), lambda j: (0, j)),
        out_shape=jax.ShapeDtypeStruct((n_rows, n), F32),
        compiler_params=_cparams(1),
        name="ada_mod",
    )(cond, w_ada, b_ada.reshape(b_ada.shape[0], 1, n))


def _rms_mod(x, g, sc, sh):
    y = x * lax.rsqrt(jnp.mean(x * x, axis=-1, keepdims=True) + RMS_EPS) * g
    return y * (1.0 + sc) + sh


def _in_proj_kernel(x_ref, g_ref, sh_ref, sc_ref, w_ref, o_ref, h_scr):
    @pl.when(pl.program_id(1) == 0)
    def _():
        h_scr[...] = _rms_mod(x_ref[...], g_ref[...], sc_ref[...], sh_ref[...]).astype(BF16)

    o_ref[...] = jnp.dot(h_scr[...], w_ref[...], preferred_element_type=F32)


def _in_proj(x, norm_g, mod, w, mod_row):
    m, d = x.shape
    n = w.shape[1]
    row = mod_row(IN_TM)
    return pl.pallas_call(
        _in_proj_kernel,
        grid=(m // IN_TM, n // IN_TN),
        in_specs=[pl.BlockSpec((IN_TM, d), lambda i, j: (i, 0)),
                  pl.BlockSpec((1, d), lambda i, j: (0, 0)),
                  pl.BlockSpec((None, 1, d), lambda i, j: (row(i), 0, 0)),
                  pl.BlockSpec((None, 1, d), lambda i, j: (row(i), 0, 1)),
                  pl.BlockSpec((d, IN_TN), lambda i, j: (0, j))],
        out_specs=pl.BlockSpec((IN_TM, IN_TN), lambda i, j: (i, j)),
        out_shape=jax.ShapeDtypeStruct((m, n), F32),
        scratch_shapes=[pltpu.VMEM((IN_TM, d), BF16)],
        compiler_params=_cparams(2),
        name="in_proj",
    )(x, norm_g.reshape(1, d), mod, mod, w)


def _out_proj_kernel(na_ref, rw_ref, hg_ref, x_ref, w_ref, g1_ref, n2_ref, sh2_ref, sc2_ref, rtw_ref, rtb_ref,
                     xo_ref, h2_ref, idx_ref, gate_ref):
    o1, o2 = NA_WIDTH, NA_WIDTH + RW_WIDTH
    mix = _dot(na_ref[...], w_ref[0:o1]) + _dot(rw_ref[...], w_ref[o1:o2]) + _dot(hg_ref[...], w_ref[o2:])
    x = x_ref[...] + g1_ref[...] * mix
    xo_ref[...] = x
    h = _rms_mod(x, n2_ref[...], sc2_ref[...], sh2_ref[...])
    h2_ref[...] = h.astype(BF16)
    logits = jnp.dot(h, rtw_ref[...], precision=lax.Precision.HIGHEST, preferred_element_type=F32) + rtb_ref[...]
    lane = lax.broadcasted_iota(jnp.int32, logits.shape, 1).astype(F32)
    idx = jnp.zeros_like(logits)
    vals = []
    for k in range(TOP_K):
        mx = jnp.max(logits, axis=-1, keepdims=True)
        sel = jnp.min(jnp.where(logits == mx, lane, float(LANES)), axis=-1, keepdims=True)
        idx = jnp.where(lane == k, sel, idx)
        vals.append(mx)
        logits = jnp.where(lane == sel, 3.0 * NEG_INF, logits)
    es = [jnp.exp(v - vals[0]) for v in vals]
    den = es[0] + es[1] + es[2] + es[3]
    gate = jnp.zeros_like(idx)
    for k in range(TOP_K):
        gate = jnp.where(lane == k, es[k] / den, gate)
    idx_ref[...] = idx.astype(jnp.int32)
    gate_ref[...] = gate


def _out_proj(na_o, rw_o, hg_o, x, w_out, mod, norm2_g, router_w, router_b, mod_row):
    m, d = x.shape
    tm = OUT_TM
    row = mod_row(tm)
    n_exp = router_w.shape[1]
    rtw = jnp.pad(router_w, ((0, 0), (0, LANES - n_exp)))
    rtb = jnp.pad(router_b, (0, LANES - n_exp), constant_values=NEG_INF).reshape(1, LANES)
    tok = lambda w: pl.BlockSpec((tm, w), lambda i: (i, 0))
    mod_blk = lambda part: pl.BlockSpec((None, 1, d), lambda i: (row(i), 0, part))
    return pl.pallas_call(
        _out_proj_kernel,
        grid=(m // tm,),
        in_specs=[tok(NA_WIDTH), tok(RW_WIDTH), tok(HG_WIDTH), tok(d),
                  pl.BlockSpec((d, d), lambda i: (0, 0)),
                  mod_blk(2), pl.BlockSpec((1, d), lambda i: (0, 0)), mod_blk(3), mod_blk(4),
                  pl.BlockSpec((d, LANES), lambda i: (0, 0)), pl.BlockSpec((1, LANES), lambda i: (0, 0))],
        out_specs=[tok(d), tok(d), tok(LANES), tok(LANES)],
        out_shape=[jax.ShapeDtypeStruct((m, d), F32), jax.ShapeDtypeStruct((m, d), BF16),
                   jax.ShapeDtypeStruct((m, LANES), jnp.int32), jax.ShapeDtypeStruct((m, LANES), F32)],
        compiler_params=_cparams(1),
        name="out_proj_router",
    )(na_o, rw_o, hg_o, x, w_out, mod, norm2_g.reshape(1, d), mod, mod, rtw, rtb)


def _final_norm_kernel(x_ref, g_ref, o_ref):
    x = x_ref[...]
    o_ref[...] = x * lax.rsqrt(jnp.mean(x * x, axis=-1, keepdims=True) + RMS_EPS) * g_ref[...]


def _final_norm(x, g):
    m, d = x.shape
    tm = TOKEN_TILE
    return pl.pallas_call(
        _final_norm_kernel,
        grid=(m // tm,),
        in_specs=[pl.BlockSpec((tm, d), lambda i: (i, 0)), pl.BlockSpec((1, d), lambda i: (0, 0))],
        out_specs=pl.BlockSpec((tm, d), lambda i: (i, 0)),
        out_shape=jax.ShapeDtypeStruct((m, d), F32),
        compiler_params=_cparams(1),
        name="final_norm",
    )(x, g.reshape(1, d))


MOE_TM = 256
GU_TN = 512
DOWN_TN = 1024


def _moe_dispatch(top_e, n_exp):
    n_tok, top_k = top_e.shape
    n_assign = n_tok * top_k
    flat_e = top_e.reshape(-1)
    order = jnp.argsort(flat_e)
    sorted_e = flat_e[order]
    counts = jnp.bincount(flat_e, length=n_exp)
    padded = (counts + MOE_TM - 1) // MOE_TM * MOE_TM
    pad_end = jnp.cumsum(padded)
    pad_start = pad_end - padded
    start = jnp.cumsum(counts) - counts
    slot_sorted = (pad_start[sorted_e] + jnp.arange(n_assign) - start[sorted_e]).astype(jnp.int32)
    n_blocks = -(-n_assign // MOE_TM) + n_exp
    slot_token = jnp.full((n_blocks * MOE_TM,), n_tok, jnp.int32).at[slot_sorted].set((order // top_k).astype(jnp.int32))
    slot_of_assign = jnp.zeros((n_assign,), jnp.int32).at[order].set(slot_sorted)
    blk = jnp.arange(n_blocks)
    block_expert = jnp.minimum(jnp.searchsorted(pad_end, blk * MOE_TM, side='right'), n_exp - 1).astype(jnp.int32)
    n_active = (pad_end[-1] // MOE_TM).astype(jnp.int32)
    block_expert = jnp.where(blk < n_active, block_expert, block_expert[jnp.maximum(n_active - 1, 0)])
    return slot_token, slot_of_assign, block_expert, n_active.reshape(1)


def _expert_changed(be_ref, m):
    return (m == 0) | (be_ref[m] != be_ref[jnp.maximum(m - 1, 0)])


def _moe_gu_kernel(be_ref, na_ref, x_ref, wg_ref, wu_ref, bg_ref, bu_ref, o_ref, wg_s, wu_s):
    m = pl.program_id(1)

    @pl.when(_expert_changed(be_ref, m))
    def _():
        wg_s[...] = wg_ref[...].astype(BF16)
        wu_s[...] = wu_ref[...].astype(BF16)

    @pl.when(m < na_ref[0])
    def _():
        x = x_ref[...]
        gate = jnp.dot(x, wg_s[...], preferred_element_type=F32) + bg_ref[...]
        up = jnp.dot(x, wu_s[...], preferred_element_type=F32) + bu_ref[...]
        gate = jnp.minimum(gate, SWIGLU_LIMIT)
        up = jnp.clip(up, -SWIGLU_LIMIT, SWIGLU_LIMIT)
        o_ref[...] = ((up + 1.0) * gate * _sigmoid(SWIGLU_ALPHA * gate)).astype(BF16)

    @pl.when(m >= na_ref[0])
    def _():
        o_ref[...] = jnp.zeros_like(o_ref)


def _moe_down_kernel(be_ref, na_ref, a_ref, w_ref, b_ref, o_ref, w_s):
    m = pl.program_id(1)

    @pl.when(_expert_changed(be_ref, m))
    def _():
        w_s[...] = w_ref[...].astype(BF16)

    @pl.when(m < na_ref[0])
    def _():
        o_ref[...] = jnp.dot(a_ref[...], w_s[...], preferred_element_type=F32) + b_ref[...]

    @pl.when(m >= na_ref[0])
    def _():
        o_ref[...] = jnp.zeros_like(o_ref)


def _moe_experts(xb, block_expert, n_active, w_gu, b_gu, w_down, b_down, layer):
    n_slots, d = xb.shape
    n_blocks = n_slots // MOE_TM
    n_exp, _, two_f = w_gu.shape[1:]
    f = two_f // 2
    nj = f // GU_TN
    act = pl.pallas_call(
        _moe_gu_kernel,
        grid_spec=pltpu.PrefetchScalarGridSpec(
            num_scalar_prefetch=2,
            grid=(nj, n_blocks),
            in_specs=[pl.BlockSpec((MOE_TM, d), lambda j, m, be, na: (m, 0)),
                      pl.BlockSpec((None, None, d, GU_TN), lambda j, m, be, na: (layer, be[m], 0, j)),
                      pl.BlockSpec((None, None, d, GU_TN), lambda j, m, be, na: (layer, be[m], 0, nj + j)),
                      pl.BlockSpec((None, None, 1, GU_TN), lambda j, m, be, na: (layer, be[m], 0, j)),
                      pl.BlockSpec((None, None, 1, GU_TN), lambda j, m, be, na: (layer, be[m], 0, nj + j))],
            out_specs=pl.BlockSpec((MOE_TM, GU_TN), lambda j, m, be, na: (m, j)),
            scratch_shapes=[pltpu.VMEM((d, GU_TN), BF16), pltpu.VMEM((d, GU_TN), BF16)]),
        out_shape=jax.ShapeDtypeStruct((n_slots, f), BF16),
        compiler_params=_cparams(2),
        name="moe_gate_up",
    )(block_expert, n_active, xb, w_gu, w_gu, b_gu.reshape(b_gu.shape[0], n_exp, 1, two_f),
      b_gu.reshape(b_gu.shape[0], n_exp, 1, two_f))
    nd = d // DOWN_TN
    return pl.pallas_call(
        _moe_down_kernel,
        grid_spec=pltpu.PrefetchScalarGridSpec(
            num_scalar_prefetch=2,
            grid=(nd, n_blocks),
            in_specs=[pl.BlockSpec((MOE_TM, f), lambda j, m, be, na: (m, 0)),
                      pl.BlockSpec((None, None, f, DOWN_TN), lambda j, m, be, na: (layer, be[m], 0, j)),
                      pl.BlockSpec((None, None, 1, DOWN_TN), lambda j, m, be, na: (layer, be[m], 0, j))],
            out_specs=pl.BlockSpec((MOE_TM, DOWN_TN), lambda j, m, be, na: (m, j)),
            scratch_shapes=[pltpu.VMEM((f, DOWN_TN), BF16)]),
        out_shape=jax.ShapeDtypeStruct((n_slots, d), F32),
        compiler_params=_cparams(2),
        name="moe_down",
    )(block_expert, n_active, act, w_down, b_down.reshape(b_down.shape[0], n_exp, 1, d))


def _moe(h2, top_e, gates, w_gu, b_gu, w_down, b_down, layer):
    n_tok, d = h2.shape
    n_exp = w_gu.shape[1]
    slot_token, slot_of_assign, block_expert, n_active = _moe_dispatch(top_e, n_exp)
    h_pad = jnp.concatenate([h2, jnp.zeros((1, d), h2.dtype)], axis=0)
    xb = h_pad[slot_token]
    yb = _moe_experts(xb, block_expert, n_active, w_gu, b_gu, w_down, b_down, layer)
    y = yb[slot_of_assign].reshape(n_tok, top_e.shape[1], d)
    return jnp.einsum('nkd,nk->nd', y, gates)


def _token_shift_mix(xr, mu, groups):
    parts = []
    for row_off, n_seq, seq_len in groups:
        x = xr[row_off:row_off + n_seq * seq_len].reshape(n_seq, seq_len, -1)
        xp = jnp.pad(x, ((0, 0), (1, 1), (0, 0)))
        sh = 0.5 * (xp[:, :-2] + xp[:, 2:])
        parts.append((x + (sh - x) * mu).reshape(n_seq * seq_len, -1))
    return jnp.concatenate(parts, axis=0)


def _pad_w_in(w):
    wb = w.astype(BF16)
    cut = 3 * NA_WIDTH + RW_SHIFT_WIDTH
    return jnp.concatenate([wb[:, :cut], jnp.zeros((w.shape[0], RW_PAD), BF16), wb[:, cut:]], axis=1)


def kernel(x_prompt, x_sample, c, cache_k, cache_v, state_rwkv_fwd, state_rwkv_bwd, state_hgrn_fwd, state_hgrn_bwd, c_ctx, norm1_g, norm2_g, w_ada, b_ada, w_in, w_out, na_rpb, rw_mu, rw_w0, rw_w2, rw_a0, rw_a2, rw_g2, rw_k_k, rw_k_a, rw_r_k, rw_ln_g, rw_ln_b, hg_lb_raw, hg_norm_g, router_w, router_b, exp_w_gu, exp_b_gu, exp_w_down, exp_b_down, final_g):
    n_p, t_p, d = x_prompt.shape
    n_s, t_s, _ = x_sample.shape
    depth = w_in.shape[0]
    mp, ms = n_p * t_p, n_s * t_s
    x = jnp.concatenate([x_prompt.reshape(mp, d), x_sample.reshape(ms, d)], axis=0)
    n_mod_rows = 8
    cond = jnp.concatenate([c_ctx[None], c, jnp.zeros((n_mod_rows - 1 - n_s, d), F32)], axis=0)
    mod_row = lambda tm: _mod_row_fn(tm, mp, t_s)
    p_lb = jax.nn.softmax(hg_lb_raw.astype(F32), axis=0)
    hg_lb = jnp.cumsum(p_lb, axis=0) - p_lb[:1]
    seqs = [(0, n_p, t_p), (mp, n_s, t_s)]
    ctx_out = []
    for l in range(depth):
        mod2d = _ada(cond, w_ada, b_ada, l)
        mod = mod2d.reshape(n_mod_rows, 1, 6 * d)
        proj = _in_proj(x, norm1_g[l], mod, _pad_w_in(w_in[l]), mod_row)
        na_p, k_ctx, v_ctx = _ctx_attention(proj, n_p, t_p)
        na_s = _na_attention(proj, mp, n_s, t_s, cache_k, cache_v, l, _na_bias_table(na_rpb[l], t_s // GRID_W))
        na_o = jnp.concatenate([na_p, na_s], axis=0)
        xr = _token_shift_mix(proj[:, PCOL_RW:PCOL_RW + RW_SHIFT_WIDTH], rw_mu[l], seqs)
        lw = dict(rw_w0=rw_w0[l], rw_w2=rw_w2[l], rw_a0=rw_a0[l], rw_a2=rw_a2[l], rw_g2=rw_g2[l], rw_k_k=rw_k_k[l],
                  rw_k_a=rw_k_a[l], rw_r_k=rw_r_k[l], rw_ln_g=rw_ln_g[l], rw_ln_b=rw_ln_b[l])
        rw_groups = [(0, n_p, t_p, jnp.zeros((2, n_p, RW_HEADS, RW_HEAD_DIM, RW_HEAD_DIM), F32)),
                     (mp, n_s, t_s, jnp.stack([state_rwkv_fwd[:, l], state_rwkv_bwd[:, l]]))]
        rw_o, rw_states = _rwkv_mixer(xr, rw_groups, lw)
        hg_groups = [(0, n_p, t_p, jnp.zeros((2, n_p, HG_HEADS, HG_KEY_DIM, HG_VAL_DIM), F32)),
                     (mp, n_s, t_s, jnp.stack([state_hgrn_fwd[:, l], state_hgrn_bwd[:, l]]))]
        hg_o, hg_states = _hgrn_mixer(proj, hg_groups, hg_lb[l], hg_norm_g[l])
        x, h2, idx, gate = _out_proj(na_o, rw_o, hg_o, x, w_out[l].astype(BF16), mod, norm2_g[l], router_w[l],
                                     router_b[l], mod_row)
        y = _moe(h2, idx[:, :TOP_K], gate[:, :TOP_K], exp_w_gu, exp_b_gu, exp_w_down, exp_b_down, l)
        g2 = mod2d[:, 5 * d:]
        g2_tok = jnp.concatenate([jnp.broadcast_to(g2[0], (mp, d)), jnp.repeat(g2[1:1 + n_s], t_s, axis=0)], axis=0)
        x = x + g2_tok * y
        ctx_out.append((k_ctx, v_ctx, rw_states[0][0], rw_states[0][1], hg_states[0][0], hg_states[0][1]))
    y = _final_norm(x, final_g)
    outs = [y[:mp].reshape(n_p, t_p, d), y[mp:].reshape(n_s, t_s, d)]
    outs += [jnp.stack([t[i] for t in ctx_out], axis=1) for i in range(6)]
    return tuple(outs)
```

```python
import functools

import numpy as np
import jax
import jax.numpy as jnp
from jax import lax
from jax.experimental import pallas as pl
from jax.experimental.pallas import tpu as pltpu

F32 = jnp.float32
BF16 = jnp.bfloat16

D_MODEL = 2048
GRID_W = 64
NA_HEADS = 16
NA_HEAD_DIM = 64
NA_WIDTH = NA_HEADS * NA_HEAD_DIM
NA_KH = 8
NA_KW = 16
RW_HEADS = 8
RW_HEAD_DIM = 64
RW_WIDTH = RW_HEADS * RW_HEAD_DIM
RW_DECAY_LORA = 64
RW_ICLR_LORA = 64
RW_GATE_LORA = 128
RW_SHIFT_WIDTH = 3 * RW_WIDTH + 2 * RW_DECAY_LORA + 2 * RW_ICLR_LORA + RW_GATE_LORA
RW_GN_EPS = 64e-5
HG_HEADS = 4
HG_KEY_DIM = 128
HG_VAL_DIM = 128
HG_WIDTH = HG_HEADS * HG_VAL_DIM
HG_MIN_F = 1e-6
IN_WIDTH = 3 * NA_WIDTH + RW_SHIFT_WIDTH + 5 * HG_WIDTH
N_EXPERTS = 32
TOP_K = 4
D_EXPERT = 2048
SWIGLU_LIMIT = 7.0
SWIGLU_ALPHA = 1.702
RMS_EPS = 1e-6
NEG_INF = -1e30

LANES = 128
PCOL_Q = 0
PCOL_K = NA_WIDTH
PCOL_V = 2 * NA_WIDTH
PCOL_RW = 3 * NA_WIDTH
RW_PAD = -RW_SHIFT_WIDTH % HG_WIDTH
PCOL_HQ = PCOL_RW + RW_SHIFT_WIDTH + RW_PAD
PCOL_HFF = PCOL_HQ + HG_WIDTH
PCOL_HFB = PCOL_HFF + HG_WIDTH
PCOL_HI = PCOL_HFB + HG_WIDTH
PCOL_HG = PCOL_HI + HG_WIDTH
IN_WIDTH_PAD = PCOL_HG + HG_WIDTH
SCAN_CHUNK = 64
TOKEN_TILE = 256
VMEM_LIMIT = 56 * 1024 * 1024


def _cparams(n_axes, vmem=VMEM_LIMIT):
    return pltpu.CompilerParams(dimension_semantics=("arbitrary",) * n_axes, vmem_limit_bytes=vmem)


def _dot(a, b):
    return jnp.dot(a.astype(BF16), b.astype(BF16), preferred_element_type=F32)


def _dot_nt(a, b):
    return lax.dot_general(a.astype(BF16), b.astype(BF16), (((1,), (1,)), ((), ())), preferred_element_type=F32)


def _dot_tn(a, b):
    return lax.dot_general(a.astype(BF16), b.astype(BF16), (((0,), (0,)), ((), ())), preferred_element_type=F32)


def _split3(x):
    hi = x.astype(BF16)
    r1 = x - hi.astype(F32)
    mid = r1.astype(BF16)
    lo = (r1 - mid.astype(F32)).astype(BF16)
    return hi, mid, lo


def _dot01_left(m01, x):
    hi, mid, lo = _split3(x)
    d = lambda p: jnp.dot(m01, p, preferred_element_type=F32)
    return d(hi) + (d(mid) + d(lo))


def _dot01_right(x, m01):
    hi, mid, lo = _split3(x)
    d = lambda p: jnp.dot(p, m01, preferred_element_type=F32)
    return d(hi) + (d(mid) + d(lo))


def _sigmoid(x):
    return 1.0 / (1.0 + jnp.exp(-x))


def _softplus(x):
    return jnp.maximum(x, 0.0) + jnp.log(1.0 + jnp.exp(-jnp.abs(x)))


def _head_sum_matrix(width, head_dim):
    i = np.arange(width)
    return jnp.asarray((i[:, None] // head_dim) == (i[None, :] // head_dim), dtype=BF16)


def _order_masks(n, reverse):
    row = lax.broadcasted_iota(jnp.int32, (n, n), 0)
    col = lax.broadcasted_iota(jnp.int32, (n, n), 1)
    diff = jnp.where(reverse, col - row, row - col)
    return diff >= 0, diff > 0


def _rwkv_prep_kernel(x_ref, w0_ref, w2_ref, a0_ref, a2_ref, g2_ref, kkw_ref, kaw_ref, hsum_ref,
                      kkn_ref, g_ref, lw_ref, kd_ref, av_ref):
    x = x_ref[...]
    k = x[:, RW_WIDTH:2 * RW_WIDTH]
    o = 3 * RW_WIDTH
    wd = (x[:, o:o + 64], x[:, o + 64:o + 128])
    ad = (x[:, o + 128:o + 192], x[:, o + 192:o + 256])
    gd = x[:, o + 256:o + 384]
    g_ref[...] = _dot(_sigmoid(gd), g2_ref[...])
    kk = k * kkw_ref[...]
    ssq = _dot01_right(kk * kk, hsum_ref[...])
    kkn_ref[...] = kk / jnp.maximum(jnp.sqrt(ssq), 1e-12)
    for d in range(2):
        wl = -_softplus(-(w0_ref[d] + _dot(jnp.tanh(wd[d]), w2_ref[d]))) - 0.5
        lw_ref[d] = -jnp.exp(wl)
        a = _sigmoid(a0_ref[d] + _dot(ad[d], a2_ref[d]))
        kd_ref[d] = k * (1.0 + (a - 1.0) * kaw_ref[...])
        av_ref[d] = a


def _rwkv_scan_kernel(r_ref, v_ref, kkn_ref, lw_ref, kd_ref, av_ref, s0_ref, y_ref, sout_ref, s_scr, *, n_chunks):
    reverse = pl.program_id(1) == 1
    c = pl.program_id(2)
    n = SCAN_CHUNK

    @pl.when(c == 0)
    def _():
        s_scr[...] = s0_ref[...]

    m_incl, m_strict = _order_masks(n, reverse)
    m_incl01 = m_incl.astype(BF16)
    hs = range(RW_HEADS)
    sl = [slice(h * RW_HEAD_DIM, (h + 1) * RW_HEAD_DIM) for h in hs]
    r = [r_ref[:, s] for s in sl]
    v = [v_ref[:, s] for s in sl]
    kkn = [kkn_ref[:, s] for s in sl]
    lw = [lw_ref[:, s] for s in sl]
    kd = [kd_ref[:, s] for s in sl]
    av = [av_ref[:, s] for s in sl]
    cum = [_dot01_left(m_incl01, lw[h]) for h in hs]
    p_tot = [jnp.exp(jnp.sum(lw[h], axis=0, keepdims=True)) for h in hs]
    p_inv = [jnp.exp(-cum[h]) for h in hs]
    a_t = [-kkn[h] * jnp.exp(cum[h] - lw[h]) for h in hs]
    b_t = [kkn[h] * av[h] * p_inv[h] for h in hs]
    k_t = [kd[h] * p_inv[h] for h in hs]
    r_t = [r[h] * jnp.exp(cum[h]) for h in hs]
    a_ab = [jnp.where(m_strict, _dot_nt(a_t[h], b_t[h]), 0.0) for h in hs]
    a_ak = [jnp.where(m_strict, _dot_nt(a_t[h], k_t[h]), 0.0) for h in hs]
    a_rb = [jnp.where(m_incl, _dot_nt(r_t[h], b_t[h]), 0.0) for h in hs]
    a_rk = [jnp.where(m_incl, _dot_nt(r_t[h], k_t[h]), 0.0) for h in hs]
    t_m = list(a_ab)
    n_pow = list(a_ab)
    for _ in range(int(np.log2(n)) - 1):
        n_pow = [_dot(n_pow[h], n_pow[h]) for h in hs]
        t_m = [t_m[h] + n_pow[h] + _dot(t_m[h], n_pow[h]) for h in hs]
    x1 = [_dot(a_ak[h], v[h]) for h in hs]
    w1 = [x1[h] + _dot(t_m[h], x1[h]) for h in hs]
    w2 = [a_t[h] + _dot(t_m[h], a_t[h]) for h in hs]
    y0 = [_dot(a_rk[h], v[h]) + _dot(a_rb[h], w1[h]) for h in hs]
    q = [r_t[h] + _dot(a_rb[h], w2[h]) for h in hs]
    g_m = [_dot_tn(w2[h], b_t[h]) for h in hs]
    h_m = [_dot_tn(w1[h], b_t[h]) + _dot_tn(v[h], k_t[h]) for h in hs]
    s0 = [s_scr[h] for h in hs]
    ys = [y0[h] + _dot_nt(q[h], s0[h]) for h in hs]
    s_new = [(s0[h] + _dot(s0[h], g_m[h]) + h_m[h]) * p_tot[h] for h in hs]
    for h in hs:
        s_scr[h] = s_new[h]
    y_ref[...] = jnp.concatenate(ys, axis=-1)

    @pl.when(c == n_chunks - 1)
    def _():
        sout_ref[...] = s_scr[...]


def _rwkv_post_kernel(y_ref, x_ref, kd_ref, g_ref, rk_ref, lng_ref, lnb_ref, hsum_ref, o_ref):
    x = x_ref[...]
    r = x[:, 0:RW_WIDTH]
    v = x[:, 2 * RW_WIDTH:3 * RW_WIDTH]
    y = y_ref[0] + y_ref[1]
    hsum = hsum_ref[...]
    inv_n = 1.0 / RW_HEAD_DIM
    mu = _dot01_right(y, hsum) * inv_n
    yc = y - mu
    var = _dot01_right(yc * yc, hsum) * inv_n
    yn = yc * lax.rsqrt(var + RW_GN_EPS) * lng_ref[...] + lnb_ref[...]
    bonus = _dot01_right(r * (0.5 * (kd_ref[0] + kd_ref[1])) * rk_ref[...], hsum) * v
    o_ref[...] = (yn + bonus) * g_ref[...]


def _rwkv_mixer(xs, groups, lw):
    m = xs.shape[0]
    tm = TOKEN_TILE
    hsum = _head_sum_matrix(RW_WIDTH, RW_HEAD_DIM)
    row = lambda a: a.reshape(1, -1)
    full = lambda shape: pl.BlockSpec(shape, lambda i: (0,) * len(shape))
    tok = lambda w: pl.BlockSpec((tm, w), lambda i: (i, 0))
    tok2 = pl.BlockSpec((2, tm, RW_WIDTH), lambda i: (0, i, 0))
    d2 = jax.ShapeDtypeStruct((2, m, RW_WIDTH), F32)
    d1 = jax.ShapeDtypeStruct((m, RW_WIDTH), F32)
    kkn, g, lwd, kd, av = pl.pallas_call(
        _rwkv_prep_kernel,
        grid=(m // tm,),
        in_specs=[tok(RW_SHIFT_WIDTH), full((2, 1, RW_WIDTH)), full((2, RW_DECAY_LORA, RW_WIDTH)),
                  full((2, 1, RW_WIDTH)), full((2, RW_ICLR_LORA, RW_WIDTH)), full((RW_GATE_LORA, RW_WIDTH)),
                  full((1, RW_WIDTH)), full((1, RW_WIDTH)), full((RW_WIDTH, RW_WIDTH))],
        out_specs=[tok(RW_WIDTH), tok(RW_WIDTH), tok2, tok2, tok2],
        out_shape=[d1, d1, d2, d2, d2],
        compiler_params=_cparams(1),
        name="rwkv_prep",
    )(xs, lw['rw_w0'].reshape(2, 1, RW_WIDTH), lw['rw_w2'], lw['rw_a0'].reshape(2, 1, RW_WIDTH), lw['rw_a2'],
      lw['rw_g2'], row(lw['rw_k_k']), row(lw['rw_k_a']), hsum)

    n = SCAN_CHUNK
    ys, states = [], []
    for row_off, n_seq, seq_len, s0 in groups:
        nc = seq_len // n
        base = row_off // n

        def tmap(b, d, c, col=0, nc=nc, base=base):
            return (base + b * nc + c + d * (nc - 1 - 2 * c), col)

        def tmap2(b, d, c, nc=nc, base=base):
            return (d, base + b * nc + c + d * (nc - 1 - 2 * c), 0)

        x_spec = lambda col: pl.BlockSpec((n, RW_WIDTH), functools.partial(tmap, col=col))
        d_spec = pl.BlockSpec((None, n, RW_WIDTH), tmap2)
        y_spec = pl.BlockSpec((None, n, RW_WIDTH), functools.partial(tmap2, base=0))
        s_spec = pl.BlockSpec((None, None, RW_HEADS, RW_HEAD_DIM, RW_HEAD_DIM), lambda b, d, c: (d, b, 0, 0, 0))
        y, s_out = pl.pallas_call(
            functools.partial(_rwkv_scan_kernel, n_chunks=nc),
            grid=(n_seq, 2, nc),
            in_specs=[x_spec(0), x_spec(2), x_spec(0), d_spec, d_spec, d_spec, s_spec],
            out_specs=[y_spec, s_spec],
            out_shape=[jax.ShapeDtypeStruct((2, n_seq * seq_len, RW_WIDTH), F32),
                       jax.ShapeDtypeStruct((2, n_seq, RW_HEADS, RW_HEAD_DIM, RW_HEAD_DIM), F32)],
            scratch_shapes=[pltpu.VMEM((RW_HEADS, RW_HEAD_DIM, RW_HEAD_DIM), F32)],
            compiler_params=_cparams(3),
            name="rwkv_scan",
        )(xs, xs, kkn, lwd, kd, av, s0)
        ys.append(y)
        states.append(s_out)
    y = jnp.concatenate(ys, axis=1)

    out = pl.pallas_call(
        _rwkv_post_kernel,
        grid=(m // tm,),
        in_specs=[tok2, tok(RW_SHIFT_WIDTH), tok2, tok(RW_WIDTH), full((1, RW_WIDTH)), full((1, RW_WIDTH)),
                  full((1, RW_WIDTH)), full((RW_WIDTH, RW_WIDTH))],
        out_specs=tok(RW_WIDTH),
        out_shape=d1,
        compiler_params=_cparams(1),
        name="rwkv_post",
    )(y, xs, kd, g, row(lw['rw_r_k']), row(lw['rw_ln_g']), row(lw['rw_ln_b']), hsum)
    return out, states


HG_LEVELS = (32, 16, 8, 4, 2, 1)


def _hgrn_consts():
    n = SCAN_CHUNK
    cums, masks = [], []
    for rev in (False, True):
        p = np.arange(n)[::-1] if rev else np.arange(n)
        pt, pj = p[:, None], p[None, :]
        rows = [pj <= pt, pj > pt]
        rows += [((pt // m) % 2 == 1) & (pj // m == pt // m) & (pj <= pt) for m in HG_LEVELS]
        rows += [((pt // m) % 2 == 0) & (pj // m == pt // m) & (pj > pt) for m in HG_LEVELS]
        cums.append(np.concatenate(rows, 0))
        mk = [pt == pj]
        mk += [(pt // (2 * m) == pj // (2 * m)) & ((pt // m) % 2 == 1) & ((pj // m) % 2 == 0) for m in HG_LEVELS]
        masks.append(np.stack(mk))
    return jnp.asarray(np.stack(cums), BF16), jnp.asarray(np.stack(masks), F32)


def _hgrn_scan_kernel(q_ref, z_ref, v_ref, lb_ref, cm_ref, mask_ref, s0_ref, o_ref, sout_ref, s_scr, *, n_chunks):
    c = pl.program_id(2)
    n = SCAN_CHUNK
    nl = len(HG_LEVELS)

    @pl.when(c == 0)
    def _():
        s_scr[...] = s0_ref[...]

    cm = cm_ref[...]
    ones = jnp.ones((n, HG_VAL_DIM), BF16)
    tn01 = lambda p: lax.dot_general(p, ones, (((0,), (0,)), ((), ())), preferred_element_type=F32)
    hs = range(HG_HEADS)
    sl = [slice(h * HG_KEY_DIM, (h + 1) * HG_KEY_DIM) for h in hs]
    hq = [q_ref[:, s] for s in sl]
    z = [z_ref[:, s] for s in sl]
    v = [v_ref[:, s] for s in sl]
    lb = [lb_ref[:, s] for s in sl]
    q = [hq[h] * _sigmoid(hq[h]) for h in hs]
    lf = [jnp.log(jnp.maximum(lb[h] + (1.0 - lb[h]) * _sigmoid(z[h]), HG_MIN_F)) for h in hs]
    k = [(1.0 - lb[h]) * _sigmoid(-z[h]) for h in hs]
    parts = [_split3(lf[h]) for h in hs]
    ex = [jnp.exp(sum(jnp.dot(cm, p, preferred_element_type=F32) for p in parts[h][::-1])) for h in hs]
    p_col = [jnp.exp(sum(tn01(p) for p in parts[h][::-1])) for h in hs]
    att = [mask_ref[0] * _dot_nt(q[h], k[h]) for h in hs]
    for i in range(nl):
        lo_q, lo_k = (2 + i) * n, (2 + nl + i) * n
        lvl = [_dot_nt(q[h] * ex[h][lo_q:lo_q + n], k[h] * ex[h][lo_k:lo_k + n]) for h in hs]
        att = [att[h] + mask_ref[i + 1] * lvl[h] for h in hs]
    s0 = [s_scr[h] for h in hs]
    outs = [_dot(q[h] * ex[h][0:n], s0[h]) + _dot(att[h], v[h]) for h in hs]
    s_new = [s0[h] * p_col[h] + _dot_tn(k[h] * ex[h][n:2 * n], v[h]) for h in hs]
    for h in hs:
        s_scr[h] = s_new[h]
    o_ref[...] = jnp.concatenate(outs, axis=-1)

    @pl.when(c == n_chunks - 1)
    def _():
        sout_ref[...] = s_scr[...]


def _hgrn_post_kernel(o_ref, g_ref, gain_ref, hsum_ref, out_ref):
    o = o_ref[0] + o_ref[1]
    ms = _dot01_right(o * o, hsum_ref[...]) * (1.0 / HG_VAL_DIM)
    on = o * lax.rsqrt(ms + RMS_EPS) * gain_ref[...]
    g = g_ref[...]
    out_ref[...] = on * (g * _sigmoid(g))


def _hgrn_mixer(proj, groups, lb, gain):
    m = proj.shape[0]
    n = SCAN_CHUNK
    cm, masks = _hgrn_consts()
    cq, cff, ci, cg = (col // HG_WIDTH for col in (PCOL_HQ, PCOL_HFF, PCOL_HI, PCOL_HG))
    os_, states = [], []
    for row_off, n_seq, seq_len, s0 in groups:
        nc = seq_len // n
        base = row_off // n

        def tmap(b, d, c, col=0, per_dir=0, nc=nc, base=base):
            return (base + b * nc + c + d * (nc - 1 - 2 * c), col + per_dir * d)

        x_spec = lambda col, per_dir=0: pl.BlockSpec((n, HG_WIDTH), functools.partial(tmap, col=col, per_dir=per_dir))
        o_spec = pl.BlockSpec((None, n, HG_WIDTH), lambda b, d, c, nc=nc: (d, b * nc + c + d * (nc - 1 - 2 * c), 0))
        s_spec = pl.BlockSpec((None, None, HG_HEADS, HG_KEY_DIM, HG_VAL_DIM), lambda b, d, c: (d, b, 0, 0, 0))
        o, s_out = pl.pallas_call(
            functools.partial(_hgrn_scan_kernel, n_chunks=nc),
            grid=(n_seq, 2, nc),
            in_specs=[x_spec(cq), x_spec(cff, 1), x_spec(ci),
                      pl.BlockSpec((None, 1, HG_WIDTH), lambda b, d, c: (d, 0, 0)),
                      pl.BlockSpec((None,) + cm.shape[1:], lambda b, d, c: (d, 0, 0)),
                      pl.BlockSpec((None,) + masks.shape[1:], lambda b, d, c: (d, 0, 0, 0)),
                      s_spec],
            out_specs=[o_spec, s_spec],
            out_shape=[jax.ShapeDtypeStruct((2, n_seq * seq_len, HG_WIDTH), F32),
                       jax.ShapeDtypeStruct((2, n_seq, HG_HEADS, HG_KEY_DIM, HG_VAL_DIM), F32)],
            scratch_shapes=[pltpu.VMEM((HG_HEADS, HG_KEY_DIM, HG_VAL_DIM), F32)],
            compiler_params=_cparams(3),
            name="hgrn_scan",
        )(proj, proj, proj, lb.reshape(2, 1, HG_WIDTH), cm, masks, s0)
        os_.append(o)
        states.append(s_out)
    o = jnp.concatenate(os_, axis=1)

    tm = TOKEN_TILE
    out = pl.pallas_call(
        _hgrn_post_kernel,
        grid=(m // tm,),
        in_specs=[pl.BlockSpec((2, tm, HG_WIDTH), lambda i: (0, i, 0)),
                  pl.BlockSpec((tm, HG_WIDTH), lambda i: (i, cg)),
                  pl.BlockSpec((1, HG_WIDTH), lambda i: (0, 0)),
                  pl.BlockSpec((HG_WIDTH, HG_WIDTH), lambda i: (0, 0))],
        out_specs=pl.BlockSpec((tm, HG_WIDTH), lambda i: (i, 0)),
        out_shape=jax.ShapeDtypeStruct((m, HG_WIDTH), F32),
        compiler_params=_cparams(1),
        name="hgrn_post",
    )(o, proj, gain.reshape(1, HG_WIDTH), _head_sum_matrix(HG_WIDTH, HG_VAL_DIM))
    return out, states


NA_HEADS_PER_STEP = LANES // NA_HEAD_DIM
NA_ROWS_PER_ITER = 4


def _ctx_attn_kernel(q_ref, k_ref, v_ref, o_ref, kc_ref, vc_ref):
    scale = NA_HEAD_DIM ** -0.5
    hs = range(NA_HEADS_PER_STEP)
    sl = [slice(h * NA_HEAD_DIM, (h + 1) * NA_HEAD_DIM) for h in hs]
    k = [k_ref[:, s] for s in sl]
    v = [v_ref[:, s] for s in sl]
    for h in hs:
        kc_ref[h] = k[h]
        vc_ref[h] = v[h]
    s = [_dot_nt(q_ref[:, sl[h]], k[h]) * scale for h in hs]
    p = [jnp.exp(s[h] - jnp.max(s[h], axis=-1, keepdims=True)) for h in hs]
    o = [_dot(p[h], v[h]) / jnp.sum(p[h], axis=-1, keepdims=True) for h in hs]
    o_ref[...] = jnp.concatenate(o, axis=-1)


def _ctx_attention(proj, n_seq, seq_len):
    nb = NA_WIDTH // LANES
    blk = lambda off: pl.BlockSpec((seq_len, LANES), lambda b, j: (b, off + j))
    c_spec = pl.BlockSpec((None, NA_HEADS_PER_STEP, seq_len, NA_HEAD_DIM), lambda b, j: (b, j, 0, 0))
    c_shape = jax.ShapeDtypeStruct((n_seq, NA_HEADS, seq_len, NA_HEAD_DIM), F32)
    return pl.pallas_call(
        _ctx_attn_kernel,
        grid=(n_seq, nb),
        in_specs=[blk(PCOL_Q // LANES), blk(PCOL_K // LANES), blk(PCOL_V // LANES)],
        out_specs=[pl.BlockSpec((seq_len, LANES), lambda b, j: (b, j)), c_spec, c_spec],
        out_shape=[jax.ShapeDtypeStruct((n_seq * seq_len, NA_WIDTH), F32), c_shape, c_shape],
        compiler_params=_cparams(2),
        name="ctx_attention",
    )(proj, proj, proj)


def _na_bias_table(rpb, rows):
    assert rows >= NA_KH
    o = np.arange(NA_KH)[:, None]
    i = np.arange(NA_KH)[None, :]
    dr = i - o + (NA_KH - 1)
    cq = np.arange(GRID_W)
    ck = np.arange(GRID_W)
    col_start = np.clip(cq - NA_KW // 2, 0, GRID_W - NA_KW)
    col_mask = (ck[None, :] >= col_start[:, None]) & (ck[None, :] < col_start[:, None] + NA_KW)
    dc = np.clip(ck[None, :] - cq[:, None], -(NA_KW - 1), NA_KW - 1) + (NA_KW - 1)
    onehot = jnp.asarray(dc.reshape(-1)[:, None] == np.arange(2 * NA_KW - 1)[None, :], F32)
    full = jnp.einsum('hrj,pj->hrp', rpb.astype(F32), onehot, precision=lax.Precision.HIGHEST)
    full = jnp.where(col_mask.reshape(-1)[None, None, :], full, NEG_INF).reshape(rpb.shape[0], -1, GRID_W, GRID_W)
    lo = lambda off: int(dr[off, 0])
    tab = jnp.stack([full[:, lo(off):lo(off) + NA_KH] for off in range(NA_KH)])
    return tab.transpose(0, 1, 3, 2, 4).reshape(NA_KH, rpb.shape[0], GRID_W, NA_KH * GRID_W)


def _na_kernel(q_ref, k_ref, v_ref, ck_ref, cv_ref, bias_ref, o_ref, *, rows):
    scale = NA_HEAD_DIM ** -0.5
    n_win = NA_KH * GRID_W

    def row_group(g, carry):
        ch = []
        for i in range(NA_ROWS_PER_ITER):
            r = g * NA_ROWS_PER_ITER + i
            r0 = jnp.clip(r - NA_KH // 2, 0, rows - NA_KH)
            q_at = pl.multiple_of(r * GRID_W, GRID_W)
            k_at = pl.multiple_of(r0 * GRID_W, GRID_W)
            for h in range(NA_HEADS_PER_STEP):
                ch.append((q_at, k_at, r - r0, h, slice(h * NA_HEAD_DIM, (h + 1) * NA_HEAD_DIM)))
        q = [q_ref[pl.ds(q_at, GRID_W), sl] for q_at, _, _, _, sl in ch]
        s_w = [_dot_nt(q[i], k_ref[pl.ds(k_at, n_win), sl]) * scale + bias_ref[off, h]
               for i, (_, k_at, off, h, sl) in enumerate(ch)]
        s_c = [_dot_nt(q[i], ck_ref[h]) * scale for i, (_, _, _, h, _) in enumerate(ch)]
        m = [jnp.maximum(jnp.max(a, axis=-1, keepdims=True), jnp.max(b, axis=-1, keepdims=True))
             for a, b in zip(s_w, s_c)]
        p_w = [jnp.exp(a - mm) for a, mm in zip(s_w, m)]
        p_c = [jnp.exp(b - mm) for b, mm in zip(s_c, m)]
        den = [jnp.sum(a, axis=-1, keepdims=True) + jnp.sum(b, axis=-1, keepdims=True) for a, b in zip(p_w, p_c)]
        o = [(_dot(p_w[i], v_ref[pl.ds(k_at, n_win), sl]) + _dot(p_c[i], cv_ref[h])) / den[i]
             for i, (_, k_at, _, h, sl) in enumerate(ch)]
        for i, (q_at, _, _, _, sl) in enumerate(ch):
            o_ref[pl.ds(q_at, GRID_W), sl] = o[i]
        return carry

    lax.fori_loop(0, rows // NA_ROWS_PER_ITER, row_group, 0)


def _na_attention(proj, row_off, n_seq, seq_len, cache_k, cache_v, layer, bias):
    assert row_off % seq_len == 0
    nb = NA_WIDTH // LANES
    rows = seq_len // GRID_W
    assert rows % NA_ROWS_PER_ITER == 0
    rb = row_off // seq_len
    past = cache_k.shape[3]
    blk = lambda off: pl.BlockSpec((seq_len, LANES), lambda b, j: (rb + b, off + j))
    c_spec = pl.BlockSpec((None, None, NA_HEADS_PER_STEP, past, NA_HEAD_DIM), lambda b, j: (b, layer, j, 0, 0))
    return pl.pallas_call(
        functools.partial(_na_kernel, rows=rows),
        grid=(n_seq, nb),
        in_specs=[blk(PCOL_Q // LANES), blk(PCOL_K // LANES), blk(PCOL_V // LANES), c_spec, c_spec,
                  pl.BlockSpec((NA_KH, NA_HEADS_PER_STEP, GRID_W, NA_KH * GRID_W), lambda b, j: (0, j, 0, 0))],
        out_specs=pl.BlockSpec((seq_len, LANES), lambda b, j: (b, j)),
        out_shape=jax.ShapeDtypeStruct((n_seq * seq_len, NA_WIDTH), F32),
        compiler_params=_cparams(2),
        name="na_attention",
    )(proj, proj, proj, cache_k, cache_v, bias)


ADA_TN = 1024
IN_TM = 512
IN_TN = 1536
OUT_TM = 256


def _mod_row_fn(tm, n_ctx_rows, latent_len):
    def f(i):
        start = i * tm
        return jnp.where(start < n_ctx_rows, 0, 1 + (start - n_ctx_rows) // latent_len)
    return f


def _ada_kernel(c_ref, w_ref, b_ref, o_ref):
    x = c_ref[...]
    o_ref[...] = _dot(x * _sigmoid(x), w_ref[...]) + b_ref[...]


def _ada(cond, w_ada, b_ada, layer):
    n_rows, d = cond.shape
    n = w_ada.shape[2]
    return pl.pallas_call(
        _ada_kernel,
        grid=(n // ADA_TN,),
        in_specs=[pl.BlockSpec((n_rows, d), lambda j: (0, 0)),
                  pl.BlockSpec((None, d, ADA_TN), lambda j: (layer, 0, j)),
                  pl.BlockSpec((None, 1, ADA_TN), lambda j: (layer, 0, j))],
        out_specs=pl.BlockSpec((n_rows, ADA_TN), lambda j: (0, j)),
        out_shape=jax.ShapeDtypeStruct((n_rows, n), F32),
        compiler_params=_cparams(1),
        name="ada_mod",
    )(cond, w_ada, b_ada.reshape(b_ada.shape[0], 1, n))


def _rms_mod(x, g, sc, sh):
    y = x * lax.rsqrt(jnp.mean(x * x, axis=-1, keepdims=True) + RMS_EPS) * g
    return y * (1.0 + sc) + sh


def _in_proj_kernel(x_ref, g_ref, sh_ref, sc_ref, w_ref, o_ref, h_scr):
    @pl.when(pl.program_id(1) == 0)
    def _():
        h_scr[...] = _rms_mod(x_ref[...], g_ref[...], sc_ref[...], sh_ref[...]).astype(BF16)

    o_ref[...] = jnp.dot(h_scr[...], w_ref[...], preferred_element_type=F32)


def _in_proj(x, norm_g, mod, w, mod_row):
    m, d = x.shape
    n = w.shape[1]
    row = mod_row(IN_TM)
    return pl.pallas_call(
        _in_proj_kernel,
        grid=(m // IN_TM, n // IN_TN),
        in_specs=[pl.BlockSpec((IN_TM, d), lambda i, j: (i, 0)),
                  pl.BlockSpec((1, d), lambda i, j: (0, 0)),
                  pl.BlockSpec((None, 1, d), lambda i, j: (row(i), 0, 0)),
                  pl.BlockSpec((None, 1, d), lambda i, j: (row(i), 0, 1)),
                  pl.BlockSpec((d, IN_TN), lambda i, j: (0, j))],
        out_specs=pl.BlockSpec((IN_TM, IN_TN), lambda i, j: (i, j)),
        out_shape=jax.ShapeDtypeStruct((m, n), F32),
        scratch_shapes=[pltpu.VMEM((IN_TM, d), BF16)],
        compiler_params=_cparams(2),
        name="in_proj",
    )(x, norm_g.reshape(1, d), mod, mod, w)


def _out_proj_kernel(na_ref, rw_ref, hg_ref, x_ref, w_ref, g1_ref, n2_ref, sh2_ref, sc2_ref, rtw_ref, rtb_ref,
                     xo_ref, h2_ref, idx_ref, gate_ref):
    o1, o2 = NA_WIDTH, NA_WIDTH + RW_WIDTH
    mix = _dot(na_ref[...], w_ref[0:o1]) + _dot(rw_ref[...], w_ref[o1:o2]) + _dot(hg_ref[...], w_ref[o2:])
    x = x_ref[...] + g1_ref[...] * mix
    xo_ref[...] = x
    h = _rms_mod(x, n2_ref[...], sc2_ref[...], sh2_ref[...])
    h2_ref[...] = h.astype(BF16)
    logits = jnp.dot(h, rtw_ref[...], precision=lax.Precision.HIGHEST, preferred_element_type=F32) + rtb_ref[...]
    lane = lax.broadcasted_iota(jnp.int32, logits.shape, 1).astype(F32)
    idx = jnp.zeros_like(logits)
    vals = []
    for k in range(TOP_K):
        mx = jnp.max(logits, axis=-1, keepdims=True)
        sel = jnp.min(jnp.where(logits == mx, lane, float(LANES)), axis=-1, keepdims=True)
        idx = jnp.where(lane == k, sel, idx)
        vals.append(mx)
        logits = jnp.where(lane == sel, 3.0 * NEG_INF, logits)
    es = [jnp.exp(v - vals[0]) for v in vals]
    den = es[0] + es[1] + es[2] + es[3]
    gate = jnp.zeros_like(idx)
    for k in range(TOP_K):
        gate = jnp.where(lane == k, es[k] / den, gate)
    idx_ref[...] = idx.astype(jnp.int32)
    gate_ref[...] = gate


def _out_proj(na_o, rw_o, hg_o, x, w_out, mod, norm2_g, router_w, router_b, mod_row):
    m, d = x.shape
    tm = OUT_TM
    row = mod_row(tm)
    n_exp = router_w.shape[1]
    rtw = jnp.pad(router_w, ((0, 0), (0, LANES - n_exp)))
    rtb = jnp.pad(router_b, (0, LANES - n_exp), constant_values=NEG_INF).reshape(1, LANES)
    tok = lambda w: pl.BlockSpec((tm, w), lambda i: (i, 0))
    mod_blk = lambda part: pl.BlockSpec((None, 1, d), lambda i: (row(i), 0, part))
    return pl.pallas_call(
        _out_proj_kernel,
        grid=(m // tm,),
        in_specs=[tok(NA_WIDTH), tok(RW_WIDTH), tok(HG_WIDTH), tok(d),
                  pl.BlockSpec((d, d), lambda i: (0, 0)),
                  mod_blk(2), pl.BlockSpec((1, d), lambda i: (0, 0)), mod_blk(3), mod_blk(4),
                  pl.BlockSpec((d, LANES), lambda i: (0, 0)), pl.BlockSpec((1, LANES), lambda i: (0, 0))],
        out_specs=[tok(d), tok(d), tok(LANES), tok(LANES)],
        out_shape=[jax.ShapeDtypeStruct((m, d), F32), jax.ShapeDtypeStruct((m, d), BF16),
                   jax.ShapeDtypeStruct((m, LANES), jnp.int32), jax.ShapeDtypeStruct((m, LANES), F32)],
        compiler_params=_cparams(1),
        name="out_proj_router",
    )(na_o, rw_o, hg_o, x, w_out, mod, norm2_g.reshape(1, d), mod, mod, rtw, rtb)


def _final_norm_kernel(x_ref, g_ref, o_ref):
    x = x_ref[...]
    o_ref[...] = x * lax.rsqrt(jnp.mean(x * x, axis=-1, keepdims=True) + RMS_EPS) * g_ref[...]


def _final_norm(x, g):
    m, d = x.shape
    tm = TOKEN_TILE
    return pl.pallas_call(
        _final_norm_kernel,
        grid=(m // tm,),
        in_specs=[pl.BlockSpec((tm, d), lambda i: (i, 0)), pl.BlockSpec((1, d), lambda i: (0, 0))],
        out_specs=pl.BlockSpec((tm, d), lambda i: (i, 0)),
        out_shape=jax.ShapeDtypeStruct((m, d), F32),
        compiler_params=_cparams(1),
        name="final_norm",
    )(x, g.reshape(1, d))


MOE_TM = 256
GU_TN = 512
DOWN_TN = 1024


def _moe_dispatch(top_e, n_exp):
    n_tok, top_k = top_e.shape
    n_assign = n_tok * top_k
    flat_e = top_e.reshape(-1)
    order = jnp.argsort(flat_e)
    sorted_e = flat_e[order]
    counts = jnp.bincount(flat_e, length=n_exp)
    padded = (counts + MOE_TM - 1) // MOE_TM * MOE_TM
    pad_end = jnp.cumsum(padded)
    pad_start = pad_end - padded
    start = jnp.cumsum(counts) - counts
    slot_sorted = (pad_start[sorted_e] + jnp.arange(n_assign) - start[sorted_e]).astype(jnp.int32)
    n_blocks = -(-n_assign // MOE_TM) + n_exp
    slot_token = jnp.zeros((n_blocks * MOE_TM,), jnp.int32).at[slot_sorted].set((order // top_k).astype(jnp.int32))
    slot_of_assign = jnp.zeros((n_assign,), jnp.int32).at[order].set(slot_sorted)
    blk = jnp.arange(n_blocks)
    block_expert = jnp.minimum(jnp.searchsorted(pad_end, blk * MOE_TM, side='right'), n_exp - 1).astype(jnp.int32)
    n_active = (pad_end[-1] // MOE_TM).astype(jnp.int32)
    block_expert = jnp.where(blk < n_active, block_expert, block_expert[jnp.maximum(n_active - 1, 0)])
    return slot_token, slot_of_assign, block_expert, n_active.reshape(1)


def _expert_changed(be_ref, m):
    return (m == 0) | (be_ref[m] != be_ref[jnp.maximum(m - 1, 0)])


def _moe_gu_kernel(be_ref, na_ref, x_ref, wg_ref, wu_ref, bg_ref, bu_ref, o_ref, wg_s, wu_s):
    m = pl.program_id(1)

    @pl.when(_expert_changed(be_ref, m))
    def _():
        wg_s[...] = wg_ref[...].astype(BF16)
        wu_s[...] = wu_ref[...].astype(BF16)

    @pl.when(m < na_ref[0])
    def _():
        x = x_ref[...]
        gate = jnp.dot(x, wg_s[...], preferred_element_type=F32) + bg_ref[...]
        up = jnp.dot(x, wu_s[...], preferred_element_type=F32) + bu_ref[...]
        gate = jnp.minimum(gate, SWIGLU_LIMIT)
        up = jnp.clip(up, -SWIGLU_LIMIT, SWIGLU_LIMIT)
        o_ref[...] = ((up + 1.0) * gate * _sigmoid(SWIGLU_ALPHA * gate)).astype(BF16)

    @pl.when(m >= na_ref[0])
    def _():
        o_ref[...] = jnp.zeros_like(o_ref)


def _moe_down_kernel(be_ref, na_ref, a_ref, w_ref, b_ref, o_ref, w_s):
    m = pl.program_id(1)

    @pl.when(_expert_changed(be_ref, m))
    def _():
        w_s[...] = w_ref[...].astype(BF16)

    @pl.when(m < na_ref[0])
    def _():
        o_ref[...] = jnp.dot(a_ref[...], w_s[...], preferred_element_type=F32) + b_ref[...]

    @pl.when(m >= na_ref[0])
    def _():
        o_ref[...] = jnp.zeros_like(o_ref)


def _moe_experts(xb, block_expert, n_active, w_gu, b_gu, w_down, b_down, layer):
    n_slots, d = xb.shape
    n_blocks = n_slots // MOE_TM
    n_exp, _, two_f = w_gu.shape[1:]
    f = two_f // 2
    nj = f // GU_TN
    act = pl.pallas_call(
        _moe_gu_kernel,
        grid_spec=pltpu.PrefetchScalarGridSpec(
            num_scalar_prefetch=2,
            grid=(nj, n_blocks),
            in_specs=[pl.BlockSpec((MOE_TM, d), lambda j, m, be, na: (m, 0)),
                      pl.BlockSpec((None, None, d, GU_TN), lambda j, m, be, na: (layer, be[m], 0, j)),
                      pl.BlockSpec((None, None, d, GU_TN), lambda j, m, be, na: (layer, be[m], 0, nj + j)),
                      pl.BlockSpec((None, None, 1, GU_TN), lambda j, m, be, na: (layer, be[m], 0, j)),
                      pl.BlockSpec((None, None, 1, GU_TN), lambda j, m, be, na: (layer, be[m], 0, nj + j))],
            out_specs=pl.BlockSpec((MOE_TM, GU_TN), lambda j, m, be, na: (m, j)),
            scratch_shapes=[pltpu.VMEM((d, GU_TN), BF16), pltpu.VMEM((d, GU_TN), BF16)]),
        out_shape=jax.ShapeDtypeStruct((n_slots, f), BF16),
        compiler_params=_cparams(2),
        name="moe_gate_up",
    )(block_expert, n_active, xb, w_gu, w_gu, b_gu.reshape(b_gu.shape[0], n_exp, 1, two_f),
      b_gu.reshape(b_gu.shape[0], n_exp, 1, two_f))
    nd = d // DOWN_TN
    return pl.pallas_call(
        _moe_down_kernel,
        grid_spec=pltpu.PrefetchScalarGridSpec(
            num_scalar_prefetch=2,
            grid=(nd, n_blocks),
            in_specs=[pl.BlockSpec((MOE_TM, f), lambda j, m, be, na: (m, 0)),
                      pl.BlockSpec((None, None, f, DOWN_TN), lambda j, m, be, na: (layer, be[m], 0, j)),
                      pl.BlockSpec((None, None, 1, DOWN_TN), lambda j, m, be, na: (layer, be[m], 0, j))],
            out_specs=pl.BlockSpec((MOE_TM, DOWN_TN), lambda j, m, be, na: (m, j)),
            scratch_shapes=[pltpu.VMEM((f, DOWN_TN), BF16)]),
        out_shape=jax.ShapeDtypeStruct((n_slots, d), F32),
        compiler_params=_cparams(2),
        name="moe_down",
    )(block_expert, n_active, act, w_down, b_down.reshape(b_down.shape[0], n_exp, 1, d))


GATHER_ROWS = 512
COMBINE_TM = 64


def _row_copy(src_hbm, dst_hbm, sem, src_row, dst_row):
    return pltpu.make_async_copy(src_hbm.at[pl.ds(src_row, 1)], dst_hbm.at[pl.ds(dst_row, 1)], sem)


def _gather_rows_kernel(idx_ref, src_ref, out_ref, sems):
    i = pl.program_id(0)
    n = pl.num_programs(0)
    slot = i % 2
    base = i * GATHER_ROWS

    def start(t, carry):
        _row_copy(src_ref, out_ref, sems.at[slot], idx_ref[0, t], base + t).start()
        return carry

    def wait_on(s):
        def wait(t, carry):
            _row_copy(src_ref, out_ref, sems.at[s], 0, 0).wait()
            return carry
        lax.fori_loop(0, GATHER_ROWS, wait, 0)

    lax.fori_loop(0, GATHER_ROWS, start, 0, unroll=8)

    @pl.when(i > 0)
    def _():
        wait_on(1 - slot)

    @pl.when(i == n - 1)
    def _():
        wait_on(slot)


def _as_row_tiles(a):
    return a.reshape(a.shape[0], a.shape[1] // LANES, LANES)


def _gather_rows(src, idx):
    n = idx.shape[0]
    nb = n // GATHER_ROWS
    return pl.pallas_call(
        _gather_rows_kernel,
        grid=(nb,),
        in_specs=[pl.BlockSpec((None, 1, GATHER_ROWS), lambda i: (i, 0, 0), memory_space=pltpu.SMEM),
                  pl.BlockSpec(memory_space=pl.ANY)],
        out_specs=pl.BlockSpec(memory_space=pl.ANY),
        out_shape=jax.ShapeDtypeStruct((n,) + src.shape[1:], src.dtype),
        scratch_shapes=[pltpu.SemaphoreType.DMA((2,))],
        compiler_params=_cparams(1),
        name="moe_dispatch_gather",
    )(idx.reshape(nb, 1, GATHER_ROWS), src)


def _combine_kernel(idx_ref, nxt_ref, yb_ref, x_ref, gate_ref, g2_ref, o_ref, buf, sems, *, top_k):
    i = pl.program_id(0)
    n = pl.num_programs(0)
    slot = i % 2

    def copy(ids, s, t, k):
        return pltpu.make_async_copy(yb_ref.at[pl.ds(ids[0, t * top_k + k], 1)], buf.at[s, k, pl.ds(t, 1)], sems.at[s])

    def issue(ids, s):
        def body(t, carry):
            for k in range(top_k):
                copy(ids, s, t, k).start()
            return carry
        lax.fori_loop(0, COMBINE_TM, body, 0, unroll=2)

    @pl.when(i == 0)
    def _():
        issue(idx_ref, 0)

    @pl.when(i + 1 < n)
    def _():
        issue(nxt_ref, 1 - slot)

    def wait(t, carry):
        for k in range(top_k):
            copy(idx_ref, slot, t, k).wait()
        return carry
    lax.fori_loop(0, COMBINE_TM, wait, 0)

    gate = gate_ref[...]
    g = [gate[:, k:k + 1] for k in range(top_k)]
    for c in range(buf.shape[3]):
        acc = g[0] * buf[slot, 0, :, c, :]
        for k in range(1, top_k):
            acc = acc + g[k] * buf[slot, k, :, c, :]
        cs = slice(c * LANES, (c + 1) * LANES)
        o_ref[:, cs] = x_ref[:, cs] + g2_ref[:, cs] * acc


def _moe_combine(yb, slot_of_assign, gate, x, mod, mod_row, top_k):
    m, d = x.shape
    tm = COMBINE_TM
    nt = m // tm
    row = mod_row(tm)
    ids = slot_of_assign.reshape(nt, 1, tm * top_k)
    smem_blk = lambda f: pl.BlockSpec((None, 1, tm * top_k), f, memory_space=pltpu.SMEM)
    return pl.pallas_call(
        functools.partial(_combine_kernel, top_k=top_k),
        grid=(nt,),
        in_specs=[smem_blk(lambda i: (i, 0, 0)), smem_blk(lambda i: (jnp.minimum(i + 1, nt - 1), 0, 0)),
                  pl.BlockSpec(memory_space=pl.ANY),
                  pl.BlockSpec((tm, d), lambda i: (i, 0)),
                  pl.BlockSpec((tm, LANES), lambda i: (i, 0)),
                  pl.BlockSpec((None, 1, d), lambda i: (row(i), 0, 5))],
        out_specs=pl.BlockSpec((tm, d), lambda i: (i, 0)),
        out_shape=jax.ShapeDtypeStruct((m, d), F32),
        scratch_shapes=[pltpu.VMEM((2, top_k, tm) + yb.shape[1:], F32), pltpu.SemaphoreType.DMA((2,))],
        compiler_params=_cparams(1),
        name="moe_combine",
    )(ids, ids, yb, x, gate, mod)


def _moe(h2, idx, gate, x, mod, mod_row, w_gu, b_gu, w_down, b_down, layer):
    n_exp = w_gu.shape[1]
    slot_token, slot_of_assign, block_expert, n_active = _moe_dispatch(idx[:, :TOP_K], n_exp)
    xb = _gather_rows(_as_row_tiles(h2), slot_token).reshape(slot_token.shape[0], h2.shape[1])
    yb = _moe_experts(xb, block_expert, n_active, w_gu, b_gu, w_down, b_down, layer)
    return _moe_combine(_as_row_tiles(yb), slot_of_assign, gate, x, mod, mod_row, TOP_K)


def _token_shift_mix(xr, mu, groups):
    parts = []
    for row_off, n_seq, seq_len in groups:
        x = xr[row_off:row_off + n_seq * seq_len].reshape(n_seq, seq_len, -1)
        xp = jnp.pad(x, ((0, 0), (1, 1), (0, 0)))
        sh = 0.5 * (xp[:, :-2] + xp[:, 2:])
        parts.append((x + (sh - x) * mu).reshape(n_seq * seq_len, -1))
    return jnp.concatenate(parts, axis=0)


def _pad_w_in(w):
    wb = w.astype(BF16)
    cut = 3 * NA_WIDTH + RW_SHIFT_WIDTH
    return jnp.concatenate([wb[:, :cut], jnp.zeros((w.shape[0], RW_PAD), BF16), wb[:, cut:]], axis=1)


def kernel(x_prompt, x_sample, c, cache_k, cache_v, state_rwkv_fwd, state_rwkv_bwd, state_hgrn_fwd, state_hgrn_bwd, c_ctx, norm1_g, norm2_g, w_ada, b_ada, w_in, w_out, na_rpb, rw_mu, rw_w0, rw_w2, rw_a0, rw_a2, rw_g2, rw_k_k, rw_k_a, rw_r_k, rw_ln_g, rw_ln_b, hg_lb_raw, hg_norm_g, router_w, router_b, exp_w_gu, exp_b_gu, exp_w_down, exp_b_down, final_g):
    n_p, t_p, d = x_prompt.shape
    n_s, t_s, _ = x_sample.shape
    depth = w_in.shape[0]
    mp, ms = n_p * t_p, n_s * t_s
    x = jnp.concatenate([x_prompt.reshape(mp, d), x_sample.reshape(ms, d)], axis=0)
    n_mod_rows = 8
    cond = jnp.concatenate([c_ctx[None], c, jnp.zeros((n_mod_rows - 1 - n_s, d), F32)], axis=0)
    mod_row = lambda tm: _mod_row_fn(tm, mp, t_s)
    p_lb = jax.nn.softmax(hg_lb_raw.astype(F32), axis=0)
    hg_lb = jnp.cumsum(p_lb, axis=0) - p_lb[:1]
    seqs = [(0, n_p, t_p), (mp, n_s, t_s)]
    ctx_out = []
    for l in range(depth):
        mod2d = _ada(cond, w_ada, b_ada, l)
        mod = mod2d.reshape(n_mod_rows, 1, 6 * d)
        proj = _in_proj(x, norm1_g[l], mod, _pad_w_in(w_in[l]), mod_row)
        na_p, k_ctx, v_ctx = _ctx_attention(proj, n_p, t_p)
        na_s = _na_attention(proj, mp, n_s, t_s, cache_k, cache_v, l, _na_bias_table(na_rpb[l], t_s // GRID_W))
        na_o = jnp.concatenate([na_p, na_s], axis=0)
        xr = _token_shift_mix(proj[:, PCOL_RW:PCOL_RW + RW_SHIFT_WIDTH], rw_mu[l], seqs)
        lw = dict(rw_w0=rw_w0[l], rw_w2=rw_w2[l], rw_a0=rw_a0[l], rw_a2=rw_a2[l], rw_g2=rw_g2[l], rw_k_k=rw_k_k[l],
                  rw_k_a=rw_k_a[l], rw_r_k=rw_r_k[l], rw_ln_g=rw_ln_g[l], rw_ln_b=rw_ln_b[l])
        rw_groups = [(0, n_p, t_p, jnp.zeros((2, n_p, RW_HEADS, RW_HEAD_DIM, RW_HEAD_DIM), F32)),
                     (mp, n_s, t_s, jnp.stack([state_rwkv_fwd[:, l], state_rwkv_bwd[:, l]]))]
        rw_o, rw_states = _rwkv_mixer(xr, rw_groups, lw)
        hg_groups = [(0, n_p, t_p, jnp.zeros((2, n_p, HG_HEADS, HG_KEY_DIM, HG_VAL_DIM), F32)),
                     (mp, n_s, t_s, jnp.stack([state_hgrn_fwd[:, l], state_hgrn_bwd[:, l]]))]
        hg_o, hg_states = _hgrn_mixer(proj, hg_groups, hg_lb[l], hg_norm_g[l])
        x, h2, idx, gate = _out_proj(na_o, rw_o, hg_o, x, w_out[l].astype(BF16), mod, norm2_g[l], router_w[l],
                                     router_b[l], mod_row)
        x = _moe(h2, idx, gate, x, mod, mod_row, exp_w_gu, exp_b_gu, exp_w_down, exp_b_down, l)
        ctx_out.append((k_ctx, v_ctx, rw_states[0][0], rw_states[0][1], hg_states[0][0], hg_states[0][1]))
    y = _final_norm(x, final_g)
    outs = [y[:mp].reshape(n_p, t_p, d), y[mp:].reshape(n_s, t_s, d)]
    outs += [jnp.stack([t[i] for t in ctx_out], axis=1) for i in range(6)]
    return tuple(outs)
```

```python
import functools

import numpy as np
import jax
import jax.numpy as jnp
from jax import lax
from jax.experimental import pallas as pl
from jax.experimental.pallas import tpu as pltpu

F32 = jnp.float32
BF16 = jnp.bfloat16

D_MODEL = 2048
GRID_W = 64
NA_HEADS = 16
NA_HEAD_DIM = 64
NA_WIDTH = NA_HEADS * NA_HEAD_DIM
NA_KH = 8
NA_KW = 16
RW_HEADS = 8
RW_HEAD_DIM = 64
RW_WIDTH = RW_HEADS * RW_HEAD_DIM
RW_DECAY_LORA = 64
RW_ICLR_LORA = 64
RW_GATE_LORA = 128
RW_SHIFT_WIDTH = 3 * RW_WIDTH + 2 * RW_DECAY_LORA + 2 * RW_ICLR_LORA + RW_GATE_LORA
RW_GN_EPS = 64e-5
HG_HEADS = 4
HG_KEY_DIM = 128
HG_VAL_DIM = 128
HG_WIDTH = HG_HEADS * HG_VAL_DIM
HG_MIN_F = 1e-6
IN_WIDTH = 3 * NA_WIDTH + RW_SHIFT_WIDTH + 5 * HG_WIDTH
N_EXPERTS = 32
TOP_K = 4
D_EXPERT = 2048
SWIGLU_LIMIT = 7.0
SWIGLU_ALPHA = 1.702
RMS_EPS = 1e-6
NEG_INF = -1e30

LANES = 128
PCOL_Q = 0
PCOL_K = NA_WIDTH
PCOL_V = 2 * NA_WIDTH
PCOL_RW = 3 * NA_WIDTH
RW_PAD = -RW_SHIFT_WIDTH % HG_WIDTH
PCOL_HQ = PCOL_RW + RW_SHIFT_WIDTH + RW_PAD
PCOL_HFF = PCOL_HQ + HG_WIDTH
PCOL_HFB = PCOL_HFF + HG_WIDTH
PCOL_HI = PCOL_HFB + HG_WIDTH
PCOL_HG = PCOL_HI + HG_WIDTH
IN_WIDTH_PAD = PCOL_HG + HG_WIDTH
SCAN_CHUNK = 64
TOKEN_TILE = 256
VMEM_LIMIT = 56 * 1024 * 1024


def _cparams(n_axes, vmem=VMEM_LIMIT):
    return pltpu.CompilerParams(dimension_semantics=("arbitrary",) * n_axes, vmem_limit_bytes=vmem)


def _dot(a, b):
    return jnp.dot(a.astype(BF16), b.astype(BF16), preferred_element_type=F32)


def _dot_nt(a, b):
    return lax.dot_general(a.astype(BF16), b.astype(BF16), (((1,), (1,)), ((), ())), preferred_element_type=F32)


def _dot_tn(a, b):
    return lax.dot_general(a.astype(BF16), b.astype(BF16), (((0,), (0,)), ((), ())), preferred_element_type=F32)


def _split3(x):
    hi = x.astype(BF16)
    r1 = x - hi.astype(F32)
    mid = r1.astype(BF16)
    lo = (r1 - mid.astype(F32)).astype(BF16)
    return hi, mid, lo


def _dot01_left(m01, x):
    hi, mid, lo = _split3(x)
    d = lambda p: jnp.dot(m01, p, preferred_element_type=F32)
    return d(hi) + (d(mid) + d(lo))


def _dot01_right(x, m01):
    hi, mid, lo = _split3(x)
    d = lambda p: jnp.dot(p, m01, preferred_element_type=F32)
    return d(hi) + (d(mid) + d(lo))


def _sigmoid(x):
    return 1.0 / (1.0 + jnp.exp(-x))


def _softplus(x):
    return jnp.maximum(x, 0.0) + jnp.log(1.0 + jnp.exp(-jnp.abs(x)))


def _head_sum_matrix(width, head_dim):
    i = np.arange(width)
    return jnp.asarray((i[:, None] // head_dim) == (i[None, :] // head_dim), dtype=BF16)


def _order_masks(n, reverse):
    row = lax.broadcasted_iota(jnp.int32, (n, n), 0)
    col = lax.broadcasted_iota(jnp.int32, (n, n), 1)
    diff = jnp.where(reverse, col - row, row - col)
    return diff >= 0, diff > 0


def _rwkv_prep_kernel(x_ref, w0_ref, w2_ref, a0_ref, a2_ref, g2_ref, kkw_ref, kaw_ref, hsum_ref,
                      kkn_ref, g_ref, lw_ref, kd_ref, av_ref):
    x = x_ref[...]
    k = x[:, RW_WIDTH:2 * RW_WIDTH]
    o = 3 * RW_WIDTH
    wd = (x[:, o:o + 64], x[:, o + 64:o + 128])
    ad = (x[:, o + 128:o + 192], x[:, o + 192:o + 256])
    gd = x[:, o + 256:o + 384]
    g_ref[...] = _dot(_sigmoid(gd), g2_ref[...])
    kk = k * kkw_ref[...]
    ssq = _dot01_right(kk * kk, hsum_ref[...])
    kkn_ref[...] = kk / jnp.maximum(jnp.sqrt(ssq), 1e-12)
    for d in range(2):
        wl = -_softplus(-(w0_ref[d] + _dot(jnp.tanh(wd[d]), w2_ref[d]))) - 0.5
        lw_ref[d] = -jnp.exp(wl)
        a = _sigmoid(a0_ref[d] + _dot(ad[d], a2_ref[d]))
        kd_ref[d] = k * (1.0 + (a - 1.0) * kaw_ref[...])
        av_ref[d] = a


def _rwkv_scan_kernel(r_ref, v_ref, kkn_ref, lw_ref, kd_ref, av_ref, s0_ref, y_ref, sout_ref, s_scr, *, n_chunks):
    reverse = pl.program_id(1) == 1
    c = pl.program_id(2)
    n = SCAN_CHUNK

    @pl.when(c == 0)
    def _():
        s_scr[...] = s0_ref[...]

    m_incl, m_strict = _order_masks(n, reverse)
    m_incl01 = m_incl.astype(BF16)
    hs = range(RW_HEADS)
    sl = [slice(h * RW_HEAD_DIM, (h + 1) * RW_HEAD_DIM) for h in hs]
    r = [r_ref[:, s] for s in sl]
    v = [v_ref[:, s] for s in sl]
    kkn = [kkn_ref[:, s] for s in sl]
    lw = [lw_ref[:, s] for s in sl]
    kd = [kd_ref[:, s] for s in sl]
    av = [av_ref[:, s] for s in sl]
    cum = [_dot01_left(m_incl01, lw[h]) for h in hs]
    p_tot = [jnp.exp(jnp.sum(lw[h], axis=0, keepdims=True)) for h in hs]
    p_inv = [jnp.exp(-cum[h]) for h in hs]
    a_t = [-kkn[h] * jnp.exp(cum[h] - lw[h]) for h in hs]
    b_t = [kkn[h] * av[h] * p_inv[h] for h in hs]
    k_t = [kd[h] * p_inv[h] for h in hs]
    r_t = [r[h] * jnp.exp(cum[h]) for h in hs]
    a_ab = [jnp.where(m_strict, _dot_nt(a_t[h], b_t[h]), 0.0) for h in hs]
    a_ak = [jnp.where(m_strict, _dot_nt(a_t[h], k_t[h]), 0.0) for h in hs]
    a_rb = [jnp.where(m_incl, _dot_nt(r_t[h], b_t[h]), 0.0) for h in hs]
    a_rk = [jnp.where(m_incl, _dot_nt(r_t[h], k_t[h]), 0.0) for h in hs]
    t_m = list(a_ab)
    n_pow = list(a_ab)
    for _ in range(int(np.log2(n)) - 1):
        n_pow = [_dot(n_pow[h], n_pow[h]) for h in hs]
        t_m = [t_m[h] + n_pow[h] + _dot(t_m[h], n_pow[h]) for h in hs]
    x1 = [_dot(a_ak[h], v[h]) for h in hs]
    w1 = [x1[h] + _dot(t_m[h], x1[h]) for h in hs]
    w2 = [a_t[h] + _dot(t_m[h], a_t[h]) for h in hs]
    y0 = [_dot(a_rk[h], v[h]) + _dot(a_rb[h], w1[h]) for h in hs]
    q = [r_t[h] + _dot(a_rb[h], w2[h]) for h in hs]
    g_m = [_dot_tn(w2[h], b_t[h]) for h in hs]
    h_m = [_dot_tn(w1[h], b_t[h]) + _dot_tn(v[h], k_t[h]) for h in hs]
    s0 = [s_scr[h] for h in hs]
    ys = [y0[h] + _dot_nt(q[h], s0[h]) for h in hs]
    s_new = [(s0[h] + _dot(s0[h], g_m[h]) + h_m[h]) * p_tot[h] for h in hs]
    for h in hs:
        s_scr[h] = s_new[h]
    y_ref[...] = jnp.concatenate(ys, axis=-1)

    @pl.when(c == n_chunks - 1)
    def _():
        sout_ref[...] = s_scr[...]


def _rwkv_post_kernel(y_ref, x_ref, kd_ref, g_ref, rk_ref, lng_ref, lnb_ref, hsum_ref, o_ref):
    x = x_ref[...]
    r = x[:, 0:RW_WIDTH]
    v = x[:, 2 * RW_WIDTH:3 * RW_WIDTH]
    y = y_ref[0] + y_ref[1]
    hsum = hsum_ref[...]
    inv_n = 1.0 / RW_HEAD_DIM
    mu = _dot01_right(y, hsum) * inv_n
    yc = y - mu
    var = _dot01_right(yc * yc, hsum) * inv_n
    yn = yc * lax.rsqrt(var + RW_GN_EPS) * lng_ref[...] + lnb_ref[...]
    bonus = _dot01_right(r * (0.5 * (kd_ref[0] + kd_ref[1])) * rk_ref[...], hsum) * v
    o_ref[...] = (yn + bonus) * g_ref[...]


def _rwkv_mixer(xs, groups, lw):
    m = xs.shape[0]
    tm = TOKEN_TILE
    hsum = _head_sum_matrix(RW_WIDTH, RW_HEAD_DIM)
    row = lambda a: a.reshape(1, -1)
    full = lambda shape: pl.BlockSpec(shape, lambda i: (0,) * len(shape))
    tok = lambda w: pl.BlockSpec((tm, w), lambda i: (i, 0))
    tok2 = pl.BlockSpec((2, tm, RW_WIDTH), lambda i: (0, i, 0))
    d2 = jax.ShapeDtypeStruct((2, m, RW_WIDTH), F32)
    d1 = jax.ShapeDtypeStruct((m, RW_WIDTH), F32)
    kkn, g, lwd, kd, av = pl.pallas_call(
        _rwkv_prep_kernel,
        grid=(m // tm,),
        in_specs=[tok(RW_SHIFT_WIDTH), full((2, 1, RW_WIDTH)), full((2, RW_DECAY_LORA, RW_WIDTH)),
                  full((2, 1, RW_WIDTH)), full((2, RW_ICLR_LORA, RW_WIDTH)), full((RW_GATE_LORA, RW_WIDTH)),
                  full((1, RW_WIDTH)), full((1, RW_WIDTH)), full((RW_WIDTH, RW_WIDTH))],
        out_specs=[tok(RW_WIDTH), tok(RW_WIDTH), tok2, tok2, tok2],
        out_shape=[d1, d1, d2, d2, d2],
        compiler_params=_cparams(1),
        name="rwkv_prep",
    )(xs, lw['rw_w0'].reshape(2, 1, RW_WIDTH), lw['rw_w2'], lw['rw_a0'].reshape(2, 1, RW_WIDTH), lw['rw_a2'],
      lw['rw_g2'], row(lw['rw_k_k']), row(lw['rw_k_a']), hsum)

    n = SCAN_CHUNK
    ys, states = [], []
    for row_off, n_seq, seq_len, s0 in groups:
        nc = seq_len // n
        base = row_off // n

        def tmap(b, d, c, col=0, nc=nc, base=base):
            return (base + b * nc + c + d * (nc - 1 - 2 * c), col)

        def tmap2(b, d, c, nc=nc, base=base):
            return (d, base + b * nc + c + d * (nc - 1 - 2 * c), 0)

        x_spec = lambda col: pl.BlockSpec((n, RW_WIDTH), functools.partial(tmap, col=col))
        d_spec = pl.BlockSpec((None, n, RW_WIDTH), tmap2)
        y_spec = pl.BlockSpec((None, n, RW_WIDTH), functools.partial(tmap2, base=0))
        s_spec = pl.BlockSpec((None, None, RW_HEADS, RW_HEAD_DIM, RW_HEAD_DIM), lambda b, d, c: (d, b, 0, 0, 0))
        y, s_out = pl.pallas_call(
            functools.partial(_rwkv_scan_kernel, n_chunks=nc),
            grid=(n_seq, 2, nc),
            in_specs=[x_spec(0), x_spec(2), x_spec(0), d_spec, d_spec, d_spec, s_spec],
            out_specs=[y_spec, s_spec],
            out_shape=[jax.ShapeDtypeStruct((2, n_seq * seq_len, RW_WIDTH), F32),
                       jax.ShapeDtypeStruct((2, n_seq, RW_HEADS, RW_HEAD_DIM, RW_HEAD_DIM), F32)],
            scratch_shapes=[pltpu.VMEM((RW_HEADS, RW_HEAD_DIM, RW_HEAD_DIM), F32)],
            compiler_params=_cparams(3),
            name="rwkv_scan",
        )(xs, xs, kkn, lwd, kd, av, s0)
        ys.append(y)
        states.append(s_out)
    y = jnp.concatenate(ys, axis=1)

    out = pl.pallas_call(
        _rwkv_post_kernel,
        grid=(m // tm,),
        in_specs=[tok2, tok(RW_SHIFT_WIDTH), tok2, tok(RW_WIDTH), full((1, RW_WIDTH)), full((1, RW_WIDTH)),
                  full((1, RW_WIDTH)), full((RW_WIDTH, RW_WIDTH))],
        out_specs=tok(RW_WIDTH),
        out_shape=d1,
        compiler_params=_cparams(1),
        name="rwkv_post",
    )(y, xs, kd, g, row(lw['rw_r_k']), row(lw['rw_ln_g']), row(lw['rw_ln_b']), hsum)
    return out, states


HG_LEVELS = (32, 16, 8, 4, 2, 1)


def _hgrn_consts():
    n = SCAN_CHUNK
    cums, masks = [], []
    for rev in (False, True):
        p = np.arange(n)[::-1] if rev else np.arange(n)
        pt, pj = p[:, None], p[None, :]
        rows = [pj <= pt, pj > pt]
        rows += [((pt // m) % 2 == 1) & (pj // m == pt // m) & (pj <= pt) for m in HG_LEVELS]
        rows += [((pt // m) % 2 == 0) & (pj // m == pt // m) & (pj > pt) for m in HG_LEVELS]
        cums.append(np.concatenate(rows, 0))
        mk = [pt == pj]
        mk += [(pt // (2 * m) == pj // (2 * m)) & ((pt // m) % 2 == 1) & ((pj // m) % 2 == 0) for m in HG_LEVELS]
        masks.append(np.stack(mk))
    return jnp.asarray(np.stack(cums), BF16), jnp.asarray(np.stack(masks), F32)


def _hgrn_scan_kernel(q_ref, z_ref, v_ref, lb_ref, cm_ref, mask_ref, s0_ref, o_ref, sout_ref, s_scr, *, n_chunks):
    c = pl.program_id(2)
    n = SCAN_CHUNK
    nl = len(HG_LEVELS)

    @pl.when(c == 0)
    def _():
        s_scr[...] = s0_ref[...]

    cm = cm_ref[...]
    ones = jnp.ones((n, HG_VAL_DIM), BF16)
    tn01 = lambda p: lax.dot_general(p, ones, (((0,), (0,)), ((), ())), preferred_element_type=F32)
    hs = range(HG_HEADS)
    sl = [slice(h * HG_KEY_DIM, (h + 1) * HG_KEY_DIM) for h in hs]
    hq = [q_ref[:, s] for s in sl]
    z = [z_ref[:, s] for s in sl]
    v = [v_ref[:, s] for s in sl]
    lb = [lb_ref[:, s] for s in sl]
    q = [hq[h] * _sigmoid(hq[h]) for h in hs]
    lf = [jnp.log(jnp.maximum(lb[h] + (1.0 - lb[h]) * _sigmoid(z[h]), HG_MIN_F)) for h in hs]
    k = [(1.0 - lb[h]) * _sigmoid(-z[h]) for h in hs]
    parts = [_split3(lf[h]) for h in hs]
    ex = [jnp.exp(sum(jnp.dot(cm, p, preferred_element_type=F32) for p in parts[h][::-1])) for h in hs]
    p_col = [jnp.exp(sum(tn01(p) for p in parts[h][::-1])) for h in hs]
    att = [mask_ref[0] * _dot_nt(q[h], k[h]) for h in hs]
    for i in range(nl):
        lo_q, lo_k = (2 + i) * n, (2 + nl + i) * n
        lvl = [_dot_nt(q[h] * ex[h][lo_q:lo_q + n], k[h] * ex[h][lo_k:lo_k + n]) for h in hs]
        att = [att[h] + mask_ref[i + 1] * lvl[h] for h in hs]
    s0 = [s_scr[h] for h in hs]
    outs = [_dot(q[h] * ex[h][0:n], s0[h]) + _dot(att[h], v[h]) for h in hs]
    s_new = [s0[h] * p_col[h] + _dot_tn(k[h] * ex[h][n:2 * n], v[h]) for h in hs]
    for h in hs:
        s_scr[h] = s_new[h]
    o_ref[...] = jnp.concatenate(outs, axis=-1)

    @pl.when(c == n_chunks - 1)
    def _():
        sout_ref[...] = s_scr[...]


def _hgrn_post_kernel(o_ref, g_ref, gain_ref, hsum_ref, out_ref):
    o = o_ref[0] + o_ref[1]
    ms = _dot01_right(o * o, hsum_ref[...]) * (1.0 / HG_VAL_DIM)
    on = o * lax.rsqrt(ms + RMS_EPS) * gain_ref[...]
    g = g_ref[...]
    out_ref[...] = on * (g * _sigmoid(g))


def _hgrn_mixer(proj, groups, lb, gain):
    m = proj.shape[0]
    n = SCAN_CHUNK
    cm, masks = _hgrn_consts()
    cq, cff, ci, cg = (col // HG_WIDTH for col in (PCOL_HQ, PCOL_HFF, PCOL_HI, PCOL_HG))
    os_, states = [], []
    for row_off, n_seq, seq_len, s0 in groups:
        nc = seq_len // n
        base = row_off // n

        def tmap(b, d, c, col=0, per_dir=0, nc=nc, base=base):
            return (base + b * nc + c + d * (nc - 1 - 2 * c), col + per_dir * d)

        x_spec = lambda col, per_dir=0: pl.BlockSpec((n, HG_WIDTH), functools.partial(tmap, col=col, per_dir=per_dir))
        o_spec = pl.BlockSpec((None, n, HG_WIDTH), lambda b, d, c, nc=nc: (d, b * nc + c + d * (nc - 1 - 2 * c), 0))
        s_spec = pl.BlockSpec((None, None, HG_HEADS, HG_KEY_DIM, HG_VAL_DIM), lambda b, d, c: (d, b, 0, 0, 0))
        o, s_out = pl.pallas_call(
            functools.partial(_hgrn_scan_kernel, n_chunks=nc),
            grid=(n_seq, 2, nc),
            in_specs=[x_spec(cq), x_spec(cff, 1), x_spec(ci),
                      pl.BlockSpec((None, 1, HG_WIDTH), lambda b, d, c: (d, 0, 0)),
                      pl.BlockSpec((None,) + cm.shape[1:], lambda b, d, c: (d, 0, 0)),
                      pl.BlockSpec((None,) + masks.shape[1:], lambda b, d, c: (d, 0, 0, 0)),
                      s_spec],
            out_specs=[o_spec, s_spec],
            out_shape=[jax.ShapeDtypeStruct((2, n_seq * seq_len, HG_WIDTH), F32),
                       jax.ShapeDtypeStruct((2, n_seq, HG_HEADS, HG_KEY_DIM, HG_VAL_DIM), F32)],
            scratch_shapes=[pltpu.VMEM((HG_HEADS, HG_KEY_DIM, HG_VAL_DIM), F32)],
            compiler_params=_cparams(3),
            name="hgrn_scan",
        )(proj, proj, proj, lb.reshape(2, 1, HG_WIDTH), cm, masks, s0)
        os_.append(o)
        states.append(s_out)
    o = jnp.concatenate(os_, axis=1)

    tm = TOKEN_TILE
    out = pl.pallas_call(
        _hgrn_post_kernel,
        grid=(m // tm,),
        in_specs=[pl.BlockSpec((2, tm, HG_WIDTH), lambda i: (0, i, 0)),
                  pl.BlockSpec((tm, HG_WIDTH), lambda i: (i, cg)),
                  pl.BlockSpec((1, HG_WIDTH), lambda i: (0, 0)),
                  pl.BlockSpec((HG_WIDTH, HG_WIDTH), lambda i: (0, 0))],
        out_specs=pl.BlockSpec((tm, HG_WIDTH), lambda i: (i, 0)),
        out_shape=jax.ShapeDtypeStruct((m, HG_WIDTH), F32),
        compiler_params=_cparams(1),
        name="hgrn_post",
    )(o, proj, gain.reshape(1, HG_WIDTH), _head_sum_matrix(HG_WIDTH, HG_VAL_DIM))
    return out, states


NA_HEADS_PER_STEP = LANES // NA_HEAD_DIM
NA_ROWS_PER_ITER = 4


def _ctx_attn_kernel(q_ref, k_ref, v_ref, o_ref, kc_ref, vc_ref):
    scale = NA_HEAD_DIM ** -0.5
    hs = range(NA_HEADS_PER_STEP)
    sl = [slice(h * NA_HEAD_DIM, (h + 1) * NA_HEAD_DIM) for h in hs]
    k = [k_ref[:, s] for s in sl]
    v = [v_ref[:, s] for s in sl]
    for h in hs:
        kc_ref[h] = k[h]
        vc_ref[h] = v[h]
    s = [_dot_nt(q_ref[:, sl[h]], k[h]) * scale for h in hs]
    p = [jnp.exp(s[h] - jnp.max(s[h], axis=-1, keepdims=True)) for h in hs]
    o = [_dot(p[h], v[h]) / jnp.sum(p[h], axis=-1, keepdims=True) for h in hs]
    o_ref[...] = jnp.concatenate(o, axis=-1)


def _ctx_attention(proj, n_seq, seq_len):
    nb = NA_WIDTH // LANES
    blk = lambda off: pl.BlockSpec((seq_len, LANES), lambda b, j: (b, off + j))
    c_spec = pl.BlockSpec((None, NA_HEADS_PER_STEP, seq_len, NA_HEAD_DIM), lambda b, j: (b, j, 0, 0))
    c_shape = jax.ShapeDtypeStruct((n_seq, NA_HEADS, seq_len, NA_HEAD_DIM), F32)
    return pl.pallas_call(
        _ctx_attn_kernel,
        grid=(n_seq, nb),
        in_specs=[blk(PCOL_Q // LANES), blk(PCOL_K // LANES), blk(PCOL_V // LANES)],
        out_specs=[pl.BlockSpec((seq_len, LANES), lambda b, j: (b, j)), c_spec, c_spec],
        out_shape=[jax.ShapeDtypeStruct((n_seq * seq_len, NA_WIDTH), F32), c_shape, c_shape],
        compiler_params=_cparams(2),
        name="ctx_attention",
    )(proj, proj, proj)


def _na_bias_table(rpb, rows):
    assert rows >= NA_KH
    o = np.arange(NA_KH)[:, None]
    i = np.arange(NA_KH)[None, :]
    dr = i - o + (NA_KH - 1)
    cq = np.arange(GRID_W)
    ck = np.arange(GRID_W)
    col_start = np.clip(cq - NA_KW // 2, 0, GRID_W - NA_KW)
    col_mask = (ck[None, :] >= col_start[:, None]) & (ck[None, :] < col_start[:, None] + NA_KW)
    dc = np.clip(ck[None, :] - cq[:, None], -(NA_KW - 1), NA_KW - 1) + (NA_KW - 1)
    onehot = jnp.asarray(dc.reshape(-1)[:, None] == np.arange(2 * NA_KW - 1)[None, :], F32)
    full = jnp.einsum('hrj,pj->hrp', rpb.astype(F32), onehot, precision=lax.Precision.HIGHEST)
    full = jnp.where(col_mask.reshape(-1)[None, None, :], full, NEG_INF).reshape(rpb.shape[0], -1, GRID_W, GRID_W)
    lo = lambda off: int(dr[off, 0])
    tab = jnp.stack([full[:, lo(off):lo(off) + NA_KH] for off in range(NA_KH)])
    return tab.transpose(0, 1, 3, 2, 4).reshape(NA_KH, rpb.shape[0], GRID_W, NA_KH * GRID_W)


def _na_kernel(q_ref, k_ref, v_ref, ck_ref, cv_ref, bias_ref, o_ref, *, rows):
    scale = NA_HEAD_DIM ** -0.5
    n_win = NA_KH * GRID_W

    def row_group(g, carry):
        ch = []
        for i in range(NA_ROWS_PER_ITER):
            r = g * NA_ROWS_PER_ITER + i
            r0 = jnp.clip(r - NA_KH // 2, 0, rows - NA_KH)
            q_at = pl.multiple_of(r * GRID_W, GRID_W)
            k_at = pl.multiple_of(r0 * GRID_W, GRID_W)
            for h in range(NA_HEADS_PER_STEP):
                ch.append((q_at, k_at, r - r0, h, slice(h * NA_HEAD_DIM, (h + 1) * NA_HEAD_DIM)))
        q = [q_ref[pl.ds(q_at, GRID_W), sl] for q_at, _, _, _, sl in ch]
        s_w = [_dot_nt(q[i], k_ref[pl.ds(k_at, n_win), sl]) * scale + bias_ref[off, h]
               for i, (_, k_at, off, h, sl) in enumerate(ch)]
        s_c = [_dot_nt(q[i], ck_ref[h]) * scale for i, (_, _, _, h, _) in enumerate(ch)]
        m = [jnp.maximum(jnp.max(a, axis=-1, keepdims=True), jnp.max(b, axis=-1, keepdims=True))
             for a, b in zip(s_w, s_c)]
        p_w = [jnp.exp(a - mm) for a, mm in zip(s_w, m)]
        p_c = [jnp.exp(b - mm) for b, mm in zip(s_c, m)]
        den = [jnp.sum(a, axis=-1, keepdims=True) + jnp.sum(b, axis=-1, keepdims=True) for a, b in zip(p_w, p_c)]
        o = [(_dot(p_w[i], v_ref[pl.ds(k_at, n_win), sl]) + _dot(p_c[i], cv_ref[h])) / den[i]
             for i, (_, k_at, _, h, sl) in enumerate(ch)]
        for i, (q_at, _, _, _, sl) in enumerate(ch):
            o_ref[pl.ds(q_at, GRID_W), sl] = o[i]
        return carry

    lax.fori_loop(0, rows // NA_ROWS_PER_ITER, row_group, 0)


def _na_attention(proj, row_off, n_seq, seq_len, cache_k, cache_v, layer, bias):
    assert row_off % seq_len == 0
    nb = NA_WIDTH // LANES
    rows = seq_len // GRID_W
    assert rows % NA_ROWS_PER_ITER == 0
    rb = row_off // seq_len
    past = cache_k.shape[3]
    blk = lambda off: pl.BlockSpec((seq_len, LANES), lambda b, j: (rb + b, off + j))
    c_spec = pl.BlockSpec((None, None, NA_HEADS_PER_STEP, past, NA_HEAD_DIM), lambda b, j: (b, layer, j, 0, 0))
    return pl.pallas_call(
        functools.partial(_na_kernel, rows=rows),
        grid=(n_seq, nb),
        in_specs=[blk(PCOL_Q // LANES), blk(PCOL_K // LANES), blk(PCOL_V // LANES), c_spec, c_spec,
                  pl.BlockSpec((NA_KH, NA_HEADS_PER_STEP, GRID_W, NA_KH * GRID_W), lambda b, j: (0, j, 0, 0))],
        out_specs=pl.BlockSpec((seq_len, LANES), lambda b, j: (b, j)),
        out_shape=jax.ShapeDtypeStruct((n_seq * seq_len, NA_WIDTH), F32),
        compiler_params=_cparams(2),
        name="na_attention",
    )(proj, proj, proj, cache_k, cache_v, bias)


ADA_TN = 1024
IN_TM = 512
IN_TN = 1536
OUT_TM = 256


def _mod_row_fn(tm, n_ctx_rows, latent_len):
    def f(i):
        start = i * tm
        return jnp.where(start < n_ctx_rows, 0, 1 + (start - n_ctx_rows) // latent_len)
    return f


def _ada_kernel(c_ref, w_ref, b_ref, o_ref):
    x = c_ref[...]
    o_ref[...] = _dot(x * _sigmoid(x), w_ref[...]) + b_ref[...]


def _ada(cond, w_ada, b_ada, layer):
    n_rows, d = cond.shape
    n = w_ada.shape[2]
    return pl.pallas_call(
        _ada_kernel,
        grid=(n // ADA_TN,),
        in_specs=[pl.BlockSpec((n_rows, d), lambda j: (0, 0)),
                  pl.BlockSpec((None, d, ADA_TN), lambda j: (layer, 0, j)),
                  pl.BlockSpec((None, 1, ADA_TN), lambda j: (layer, 0, j))],
        out_specs=pl.BlockSpec((n_rows, ADA_TN), lambda j: (0, j)),
        out_shape=jax.ShapeDtypeStruct((n_rows, n), F32),
        compiler_params=_cparams(1),
        name="ada_mod",
    )(cond, w_ada, b_ada.reshape(b_ada.shape[0], 1, n))


def _rms_mod(x, g, sc, sh):
    y = x * lax.rsqrt(jnp.mean(x * x, axis=-1, keepdims=True) + RMS_EPS) * g
    return y * (1.0 + sc) + sh


def _in_proj_kernel(x_ref, g_ref, sh_ref, sc_ref, w_ref, o_ref, h_scr):
    @pl.when(pl.program_id(1) == 0)
    def _():
        h_scr[...] = _rms_mod(x_ref[...], g_ref[...], sc_ref[...], sh_ref[...]).astype(BF16)

    o_ref[...] = jnp.dot(h_scr[...], w_ref[...], preferred_element_type=F32)


def _in_proj(x, norm_g, mod, w, mod_row):
    m, d = x.shape
    n = w.shape[1]
    row = mod_row(IN_TM)
    return pl.pallas_call(
        _in_proj_kernel,
        grid=(m // IN_TM, n // IN_TN),
        in_specs=[pl.BlockSpec((IN_TM, d), lambda i, j: (i, 0)),
                  pl.BlockSpec((1, d), lambda i, j: (0, 0)),
                  pl.BlockSpec((None, 1, d), lambda i, j: (row(i), 0, 0)),
                  pl.BlockSpec((None, 1, d), lambda i, j: (row(i), 0, 1)),
                  pl.BlockSpec((d, IN_TN), lambda i, j: (0, j))],
        out_specs=pl.BlockSpec((IN_TM, IN_TN), lambda i, j: (i, j)),
        out_shape=jax.ShapeDtypeStruct((m, n), F32),
        scratch_shapes=[pltpu.VMEM((IN_TM, d), BF16)],
        compiler_params=_cparams(2),
        name="in_proj",
    )(x, norm_g.reshape(1, d), mod, mod, w)


def _out_proj_kernel(na_ref, rw_ref, hg_ref, x_ref, w_ref, g1_ref, n2_ref, sh2_ref, sc2_ref, rtw_ref, rtb_ref,
                     xo_ref, h2_ref, idx_ref, gate_ref):
    o1, o2 = NA_WIDTH, NA_WIDTH + RW_WIDTH
    mix = _dot(na_ref[...], w_ref[0:o1]) + _dot(rw_ref[...], w_ref[o1:o2]) + _dot(hg_ref[...], w_ref[o2:])
    x = x_ref[...] + g1_ref[...] * mix
    xo_ref[...] = x
    h = _rms_mod(x, n2_ref[...], sc2_ref[...], sh2_ref[...])
    h2_ref[...] = h
    logits = jnp.dot(h, rtw_ref[...], precision=lax.Precision.HIGHEST, preferred_element_type=F32) + rtb_ref[...]
    lane = lax.broadcasted_iota(jnp.int32, logits.shape, 1).astype(F32)
    idx = jnp.zeros_like(logits)
    vals = []
    for k in range(TOP_K):
        mx = jnp.max(logits, axis=-1, keepdims=True)
        sel = jnp.min(jnp.where(logits == mx, lane, float(LANES)), axis=-1, keepdims=True)
        idx = jnp.where(lane == k, sel, idx)
        vals.append(mx)
        logits = jnp.where(lane == sel, 3.0 * NEG_INF, logits)
    es = [jnp.exp(v - vals[0]) for v in vals]
    den = es[0] + es[1] + es[2] + es[3]
    gate = jnp.zeros_like(idx)
    for k in range(TOP_K):
        gate = jnp.where(lane == k, es[k] / den, gate)
    idx_ref[...] = idx.astype(jnp.int32)
    gate_ref[...] = gate


def _out_proj(na_o, rw_o, hg_o, x, w_out, mod, norm2_g, router_w, router_b, mod_row):
    m, d = x.shape
    tm = OUT_TM
    row = mod_row(tm)
    n_exp = router_w.shape[1]
    rtw = jnp.pad(router_w, ((0, 0), (0, LANES - n_exp)))
    rtb = jnp.pad(router_b, (0, LANES - n_exp), constant_values=NEG_INF).reshape(1, LANES)
    tok = lambda w: pl.BlockSpec((tm, w), lambda i: (i, 0))
    mod_blk = lambda part: pl.BlockSpec((None, 1, d), lambda i: (row(i), 0, part))
    return pl.pallas_call(
        _out_proj_kernel,
        grid=(m // tm,),
        in_specs=[tok(NA_WIDTH), tok(RW_WIDTH), tok(HG_WIDTH), tok(d),
                  pl.BlockSpec((d, d), lambda i: (0, 0)),
                  mod_blk(2), pl.BlockSpec((1, d), lambda i: (0, 0)), mod_blk(3), mod_blk(4),
                  pl.BlockSpec((d, LANES), lambda i: (0, 0)), pl.BlockSpec((1, LANES), lambda i: (0, 0))],
        out_specs=[tok(d), tok(d), tok(LANES), tok(LANES)],
        out_shape=[jax.ShapeDtypeStruct((m, d), F32), jax.ShapeDtypeStruct((m, d), F32),
                   jax.ShapeDtypeStruct((m, LANES), jnp.int32), jax.ShapeDtypeStruct((m, LANES), F32)],
        compiler_params=_cparams(1),
        name="out_proj_router",
    )(na_o, rw_o, hg_o, x, w_out, mod, norm2_g.reshape(1, d), mod, mod, rtw, rtb)


def _final_norm_kernel(x_ref, g_ref, o_ref):
    x = x_ref[...]
    o_ref[...] = x * lax.rsqrt(jnp.mean(x * x, axis=-1, keepdims=True) + RMS_EPS) * g_ref[...]


def _final_norm(x, g):
    m, d = x.shape
    tm = TOKEN_TILE
    return pl.pallas_call(
        _final_norm_kernel,
        grid=(m // tm,),
        in_specs=[pl.BlockSpec((tm, d), lambda i: (i, 0)), pl.BlockSpec((1, d), lambda i: (0, 0))],
        out_specs=pl.BlockSpec((tm, d), lambda i: (i, 0)),
        out_shape=jax.ShapeDtypeStruct((m, d), F32),
        compiler_params=_cparams(1),
        name="final_norm",
    )(x, g.reshape(1, d))


MOE_TM = 512
GU_TN = 512
DOWN_TN = 1024


def _moe_dispatch(top_e, n_exp):
    n_tok, top_k = top_e.shape
    n_assign = n_tok * top_k
    flat_e = top_e.reshape(-1)
    order = jnp.argsort(flat_e)
    sorted_e = flat_e[order]
    counts = jnp.bincount(flat_e, length=n_exp)
    padded = (counts + MOE_TM - 1) // MOE_TM * MOE_TM
    pad_end = jnp.cumsum(padded)
    pad_start = pad_end - padded
    start = jnp.cumsum(counts) - counts
    slot_sorted = (pad_start[sorted_e] + jnp.arange(n_assign) - start[sorted_e]).astype(jnp.int32)
    n_blocks = -(-n_assign // MOE_TM) + n_exp
    slot_token = jnp.zeros((n_blocks * MOE_TM,), jnp.int32).at[slot_sorted].set((order // top_k).astype(jnp.int32))
    slot_of_assign = jnp.zeros((n_assign,), jnp.int32).at[order].set(slot_sorted)
    blk = jnp.arange(n_blocks)
    block_expert = jnp.sum((pad_end[None, :] <= (blk * MOE_TM)[:, None]).astype(jnp.int32), axis=1)
    block_expert = jnp.minimum(block_expert, n_exp - 1).astype(jnp.int32)
    n_active = (pad_end[-1] // MOE_TM).astype(jnp.int32)
    block_expert = jnp.where(blk < n_active, block_expert, block_expert[jnp.maximum(n_active - 1, 0)])
    return slot_token, slot_of_assign, block_expert, n_active.reshape(1)


def _expert_changed(be_ref, m):
    return (m == 0) | (be_ref[m] != be_ref[jnp.maximum(m - 1, 0)])


def _moe_gu_kernel(be_ref, na_ref, x_ref, wg_ref, wu_ref, bg_ref, bu_ref, o_ref, wg_s, wu_s):
    m = pl.program_id(1)

    @pl.when(_expert_changed(be_ref, m))
    def _():
        wg_s[...] = wg_ref[...].astype(BF16)
        wu_s[...] = wu_ref[...].astype(BF16)

    @pl.when(m < na_ref[0])
    def _():
        x = x_ref[...]
        gate = jnp.dot(x, wg_s[...], preferred_element_type=F32) + bg_ref[...]
        up = jnp.dot(x, wu_s[...], preferred_element_type=F32) + bu_ref[...]
        gate = jnp.minimum(gate, SWIGLU_LIMIT)
        up = jnp.clip(up, -SWIGLU_LIMIT, SWIGLU_LIMIT)
        o_ref[...] = ((up + 1.0) * gate * _sigmoid(SWIGLU_ALPHA * gate)).astype(BF16)

    @pl.when(m >= na_ref[0])
    def _():
        o_ref[...] = jnp.zeros_like(o_ref)


def _moe_down_kernel(be_ref, na_ref, a_ref, w_ref, b_ref, o_ref, w_s):
    m = pl.program_id(1)

    @pl.when(_expert_changed(be_ref, m))
    def _():
        w_s[...] = w_ref[...].astype(BF16)

    @pl.when(m < na_ref[0])
    def _():
        o_ref[...] = jnp.dot(a_ref[...], w_s[...], preferred_element_type=F32) + b_ref[...]

    @pl.when(m >= na_ref[0])
    def _():
        o_ref[...] = jnp.zeros_like(o_ref)


def _moe_experts(xb, block_expert, n_active, w_gu, b_gu, w_down, b_down, layer):
    n_slots, d = xb.shape
    n_blocks = n_slots // MOE_TM
    n_exp, _, two_f = w_gu.shape[1:]
    f = two_f // 2
    nj = f // GU_TN
    act = pl.pallas_call(
        _moe_gu_kernel,
        grid_spec=pltpu.PrefetchScalarGridSpec(
            num_scalar_prefetch=2,
            grid=(nj, n_blocks),
            in_specs=[pl.BlockSpec((MOE_TM, d), lambda j, m, be, na: (m, 0)),
                      pl.BlockSpec((None, None, d, GU_TN), lambda j, m, be, na: (layer, be[m], 0, j)),
                      pl.BlockSpec((None, None, d, GU_TN), lambda j, m, be, na: (layer, be[m], 0, nj + j)),
                      pl.BlockSpec((None, None, 1, GU_TN), lambda j, m, be, na: (layer, be[m], 0, j)),
                      pl.BlockSpec((None, None, 1, GU_TN), lambda j, m, be, na: (layer, be[m], 0, nj + j))],
            out_specs=pl.BlockSpec((MOE_TM, GU_TN), lambda j, m, be, na: (m, j)),
            scratch_shapes=[pltpu.VMEM((d, GU_TN), BF16), pltpu.VMEM((d, GU_TN), BF16)]),
        out_shape=jax.ShapeDtypeStruct((n_slots, f), BF16),
        compiler_params=_cparams(2),
        name="moe_gate_up",
    )(block_expert, n_active, xb, w_gu, w_gu, b_gu.reshape(b_gu.shape[0], n_exp, 1, two_f),
      b_gu.reshape(b_gu.shape[0], n_exp, 1, two_f))
    nd = d // DOWN_TN
    return pl.pallas_call(
        _moe_down_kernel,
        grid_spec=pltpu.PrefetchScalarGridSpec(
            num_scalar_prefetch=2,
            grid=(nd, n_blocks),
            in_specs=[pl.BlockSpec((MOE_TM, f), lambda j, m, be, na: (m, 0)),
                      pl.BlockSpec((None, None, f, DOWN_TN), lambda j, m, be, na: (layer, be[m], 0, j)),
                      pl.BlockSpec((None, None, 1, DOWN_TN), lambda j, m, be, na: (layer, be[m], 0, j))],
            out_specs=pl.BlockSpec((MOE_TM, DOWN_TN), lambda j, m, be, na: (m, j)),
            scratch_shapes=[pltpu.VMEM((f, DOWN_TN), BF16)]),
        out_shape=jax.ShapeDtypeStruct((n_slots, d), F32),
        compiler_params=_cparams(2),
        name="moe_down",
    )(block_expert, n_active, act, w_down, b_down.reshape(b_down.shape[0], n_exp, 1, d))


GATHER_ROWS = 256
COMBINE_TM = 64
DMA_PRIORITIES = 2


def _gather_rows_kernel(idx_ref, nxt_ref, src_ref, o_ref, buf, sems):
    i = pl.program_id(0)
    n = pl.num_programs(0)
    slot = i % 2

    def copy(ids, s, t):
        return pltpu.make_async_copy(src_ref.at[pl.ds(ids[0, t], 1)], buf.at[s, pl.ds(t, 1)], sems.at[s])

    def issue(ids, s):
        def body(g, carry):
            for p in range(DMA_PRIORITIES):
                copy(ids, s, g * DMA_PRIORITIES + p).start(priority=p)
            return carry
        lax.fori_loop(0, GATHER_ROWS // DMA_PRIORITIES, body, 0, unroll=4)

    @pl.when(i == 0)
    def _():
        issue(idx_ref, 0)

    @pl.when(i + 1 < n)
    def _():
        issue(nxt_ref, 1 - slot)

    def wait(t, carry):
        copy(idx_ref, slot, t).wait()
        return carry
    lax.fori_loop(0, GATHER_ROWS, wait, 0)

    for c in range(buf.shape[2]):
        o_ref[:, c * LANES:(c + 1) * LANES] = buf[slot, :, c, :].astype(o_ref.dtype)


def _as_row_tiles(a):
    return a.reshape(a.shape[0], a.shape[1] // LANES, LANES)


def _gather_rows(src, idx, out_dtype):
    n = idx.shape[0]
    nb = n // GATHER_ROWS
    d = src.shape[1] * src.shape[2]
    ids = idx.reshape(nb, 1, GATHER_ROWS)
    smem_blk = lambda f: pl.BlockSpec((None, 1, GATHER_ROWS), f, memory_space=pltpu.SMEM)
    return pl.pallas_call(
        _gather_rows_kernel,
        grid=(nb,),
        in_specs=[smem_blk(lambda i: (i, 0, 0)), smem_blk(lambda i: (jnp.minimum(i + 1, nb - 1), 0, 0)),
                  pl.BlockSpec(memory_space=pl.ANY)],
        out_specs=pl.BlockSpec((GATHER_ROWS, d), lambda i: (i, 0)),
        out_shape=jax.ShapeDtypeStruct((n, d), out_dtype),
        scratch_shapes=[pltpu.VMEM((2, GATHER_ROWS) + src.shape[1:], src.dtype), pltpu.SemaphoreType.DMA((2,))],
        compiler_params=_cparams(1),
        name="moe_dispatch_gather",
    )(ids, ids, src)


def _combine_kernel(idx_ref, nxt_ref, yb_ref, x_ref, gate_ref, g2_ref, o_ref, buf, sems, *, top_k):
    i = pl.program_id(0)
    n = pl.num_programs(0)
    slot = i % 2

    def copy(ids, s, t, k):
        return pltpu.make_async_copy(yb_ref.at[pl.ds(ids[0, t * top_k + k], 1)], buf.at[s, k, pl.ds(t, 1)], sems.at[s])

    def issue(ids, s):
        def body(t, carry):
            for k in range(top_k):
                copy(ids, s, t, k).start(priority=k % DMA_PRIORITIES)
            return carry
        lax.fori_loop(0, COMBINE_TM, body, 0, unroll=2)

    @pl.when(i == 0)
    def _():
        issue(idx_ref, 0)

    @pl.when(i + 1 < n)
    def _():
        issue(nxt_ref, 1 - slot)

    def wait(t, carry):
        for k in range(top_k):
            copy(idx_ref, slot, t, k).wait()
        return carry
    lax.fori_loop(0, COMBINE_TM, wait, 0)

    gate = gate_ref[...]
    g = [gate[:, k:k + 1] for k in range(top_k)]
    for c in range(buf.shape[3]):
        acc = g[0] * buf[slot, 0, :, c, :]
        for k in range(1, top_k):
            acc = acc + g[k] * buf[slot, k, :, c, :]
        cs = slice(c * LANES, (c + 1) * LANES)
        o_ref[:, cs] = x_ref[:, cs] + g2_ref[:, cs] * acc


def _moe_combine(yb, slot_of_assign, gate, x, mod, mod_row, top_k):
    m, d = x.shape
    tm = COMBINE_TM
    nt = m // tm
    row = mod_row(tm)
    ids = slot_of_assign.reshape(nt, 1, tm * top_k)
    smem_blk = lambda f: pl.BlockSpec((None, 1, tm * top_k), f, memory_space=pltpu.SMEM)
    return pl.pallas_call(
        functools.partial(_combine_kernel, top_k=top_k),
        grid=(nt,),
        in_specs=[smem_blk(lambda i: (i, 0, 0)), smem_blk(lambda i: (jnp.minimum(i + 1, nt - 1), 0, 0)),
                  pl.BlockSpec(memory_space=pl.ANY),
                  pl.BlockSpec((tm, d), lambda i: (i, 0)),
                  pl.BlockSpec((tm, LANES), lambda i: (i, 0)),
                  pl.BlockSpec((None, 1, d), lambda i: (row(i), 0, 5))],
        out_specs=pl.BlockSpec((tm, d), lambda i: (i, 0)),
        out_shape=jax.ShapeDtypeStruct((m, d), F32),
        scratch_shapes=[pltpu.VMEM((2, top_k, tm) + yb.shape[1:], F32), pltpu.SemaphoreType.DMA((2,))],
        compiler_params=_cparams(1),
        name="moe_combine",
    )(ids, ids, yb, x, gate, mod)


def _moe(h2, idx, gate, x, mod, mod_row, w_gu, b_gu, w_down, b_down, layer):
    n_exp = w_gu.shape[1]
    slot_token, slot_of_assign, block_expert, n_active = _moe_dispatch(idx[:, :TOP_K], n_exp)
    xb = _gather_rows(_as_row_tiles(h2), slot_token, BF16)
    yb = _moe_experts(xb, block_expert, n_active, w_gu, b_gu, w_down, b_down, layer)
    return _moe_combine(_as_row_tiles(yb), slot_of_assign, gate, x, mod, mod_row, TOP_K)


def _token_shift_mix(xr, mu, groups):
    parts = []
    for row_off, n_seq, seq_len in groups:
        x = xr[row_off:row_off + n_seq * seq_len].reshape(n_seq, seq_len, -1)
        xp = jnp.pad(x, ((0, 0), (1, 1), (0, 0)))
        sh = 0.5 * (xp[:, :-2] + xp[:, 2:])
        parts.append((x + (sh - x) * mu).reshape(n_seq * seq_len, -1))
    return jnp.concatenate(parts, axis=0)


def _pad_w_in(w):
    wb = w.astype(BF16)
    cut = 3 * NA_WIDTH + RW_SHIFT_WIDTH
    return jnp.concatenate([wb[:, :cut], jnp.zeros((w.shape[0], RW_PAD), BF16), wb[:, cut:]], axis=1)


def kernel(x_prompt, x_sample, c, cache_k, cache_v, state_rwkv_fwd, state_rwkv_bwd, state_hgrn_fwd, state_hgrn_bwd, c_ctx, norm1_g, norm2_g, w_ada, b_ada, w_in, w_out, na_rpb, rw_mu, rw_w0, rw_w2, rw_a0, rw_a2, rw_g2, rw_k_k, rw_k_a, rw_r_k, rw_ln_g, rw_ln_b, hg_lb_raw, hg_norm_g, router_w, router_b, exp_w_gu, exp_b_gu, exp_w_down, exp_b_down, final_g):
    n_p, t_p, d = x_prompt.shape
    n_s, t_s, _ = x_sample.shape
    depth = w_in.shape[0]
    mp, ms = n_p * t_p, n_s * t_s
    x = jnp.concatenate([x_prompt.reshape(mp, d), x_sample.reshape(ms, d)], axis=0)
    n_mod_rows = 8
    cond = jnp.concatenate([c_ctx[None], c, jnp.zeros((n_mod_rows - 1 - n_s, d), F32)], axis=0)
    mod_row = lambda tm: _mod_row_fn(tm, mp, t_s)
    p_lb = jax.nn.softmax(hg_lb_raw.astype(F32), axis=0)
    hg_lb = jnp.cumsum(p_lb, axis=0) - p_lb[:1]
    seqs = [(0, n_p, t_p), (mp, n_s, t_s)]
    ctx_out = []
    for l in range(depth):
        mod2d = _ada(cond, w_ada, b_ada, l)
        mod = mod2d.reshape(n_mod_rows, 1, 6 * d)
        proj = _in_proj(x, norm1_g[l], mod, _pad_w_in(w_in[l]), mod_row)
        na_p, k_ctx, v_ctx = _ctx_attention(proj, n_p, t_p)
        na_s = _na_attention(proj, mp, n_s, t_s, cache_k, cache_v, l, _na_bias_table(na_rpb[l], t_s // GRID_W))
        na_o = jnp.concatenate([na_p, na_s], axis=0)
        xr = _token_shift_mix(proj[:, PCOL_RW:PCOL_RW + RW_SHIFT_WIDTH], rw_mu[l], seqs)
        lw = dict(rw_w0=rw_w0[l], rw_w2=rw_w2[l], rw_a0=rw_a0[l], rw_a2=rw_a2[l], rw_g2=rw_g2[l], rw_k_k=rw_k_k[l],
                  rw_k_a=rw_k_a[l], rw_r_k=rw_r_k[l], rw_ln_g=rw_ln_g[l], rw_ln_b=rw_ln_b[l])
        rw_groups = [(0, n_p, t_p, jnp.zeros((2, n_p, RW_HEADS, RW_HEAD_DIM, RW_HEAD_DIM), F32)),
                     (mp, n_s, t_s, jnp.stack([state_rwkv_fwd[:, l], state_rwkv_bwd[:, l]]))]
        rw_o, rw_states = _rwkv_mixer(xr, rw_groups, lw)
        hg_groups = [(0, n_p, t_p, jnp.zeros((2, n_p, HG_HEADS, HG_KEY_DIM, HG_VAL_DIM), F32)),
                     (mp, n_s, t_s, jnp.stack([state_hgrn_fwd[:, l], state_hgrn_bwd[:, l]]))]
        hg_o, hg_states = _hgrn_mixer(proj, hg_groups, hg_lb[l], hg_norm_g[l])
        x, h2, idx, gate = _out_proj(na_o, rw_o, hg_o, x, w_out[l].astype(BF16), mod, norm2_g[l], router_w[l],
                                     router_b[l], mod_row)
        x = _moe(h2, idx, gate, x, mod, mod_row, exp_w_gu, exp_b_gu, exp_w_down, exp_b_down, l)
        ctx_out.append((k_ctx, v_ctx, rw_states[0][0], rw_states[0][1], hg_states[0][0], hg_states[0][1]))
    y = _final_norm(x, final_g)
    outs = [y[:mp].reshape(n_p, t_p, d), y[mp:].reshape(n_s, t_s, d)]
    outs += [jnp.stack([t[i] for t in ctx_out], axis=1) for i in range(6)]
    return tuple(outs)
```

```python
import functools

import numpy as np
import jax
import jax.numpy as jnp
from jax import lax
from jax.experimental import pallas as pl
from jax.experimental.pallas import tpu as pltpu

F32 = jnp.float32
BF16 = jnp.bfloat16

D_MODEL = 2048
GRID_W = 64
NA_HEADS = 16
NA_HEAD_DIM = 64
NA_WIDTH = NA_HEADS * NA_HEAD_DIM
NA_KH = 8
NA_KW = 16
RW_HEADS = 8
RW_HEAD_DIM = 64
RW_WIDTH = RW_HEADS * RW_HEAD_DIM
RW_DECAY_LORA = 64
RW_ICLR_LORA = 64
RW_GATE_LORA = 128
RW_SHIFT_WIDTH = 3 * RW_WIDTH + 2 * RW_DECAY_LORA + 2 * RW_ICLR_LORA + RW_GATE_LORA
RW_GN_EPS = 64e-5
HG_HEADS = 4
HG_KEY_DIM = 128
HG_VAL_DIM = 128
HG_WIDTH = HG_HEADS * HG_VAL_DIM
HG_MIN_F = 1e-6
IN_WIDTH = 3 * NA_WIDTH + RW_SHIFT_WIDTH + 5 * HG_WIDTH
N_EXPERTS = 32
TOP_K = 4
D_EXPERT = 2048
SWIGLU_LIMIT = 7.0
SWIGLU_ALPHA = 1.702
RMS_EPS = 1e-6
NEG_INF = -1e30

LANES = 128
PCOL_Q = 0
PCOL_K = NA_WIDTH
PCOL_V = 2 * NA_WIDTH
PCOL_RW = 3 * NA_WIDTH
RW_PAD = -RW_SHIFT_WIDTH % HG_WIDTH
PCOL_HQ = PCOL_RW + RW_SHIFT_WIDTH + RW_PAD
PCOL_HFF = PCOL_HQ + HG_WIDTH
PCOL_HFB = PCOL_HFF + HG_WIDTH
PCOL_HI = PCOL_HFB + HG_WIDTH
PCOL_HG = PCOL_HI + HG_WIDTH
IN_WIDTH_PAD = PCOL_HG + HG_WIDTH
SCAN_CHUNK = 64
TOKEN_TILE = 256
VMEM_LIMIT = 56 * 1024 * 1024


def _cparams(n_axes, vmem=VMEM_LIMIT):
    return pltpu.CompilerParams(dimension_semantics=("arbitrary",) * n_axes, vmem_limit_bytes=vmem)


def _dot(a, b):
    return jnp.dot(a.astype(BF16), b.astype(BF16), preferred_element_type=F32)


def _dot_nt(a, b):
    return lax.dot_general(a.astype(BF16), b.astype(BF16), (((1,), (1,)), ((), ())), preferred_element_type=F32)


def _dot_tn(a, b):
    return lax.dot_general(a.astype(BF16), b.astype(BF16), (((0,), (0,)), ((), ())), preferred_element_type=F32)


def _split3(x):
    hi = x.astype(BF16)
    r1 = x - hi.astype(F32)
    mid = r1.astype(BF16)
    lo = (r1 - mid.astype(F32)).astype(BF16)
    return hi, mid, lo


def _dot01_left(m01, x):
    hi, mid, lo = _split3(x)
    d = lambda p: jnp.dot(m01, p, preferred_element_type=F32)
    return d(hi) + (d(mid) + d(lo))


def _dot01_right(x, m01):
    hi, mid, lo = _split3(x)
    d = lambda p: jnp.dot(p, m01, preferred_element_type=F32)
    return d(hi) + (d(mid) + d(lo))


def _sigmoid(x):
    return 1.0 / (1.0 + jnp.exp(-x))


def _softplus(x):
    return jnp.maximum(x, 0.0) + jnp.log(1.0 + jnp.exp(-jnp.abs(x)))


def _head_sum_matrix(width, head_dim):
    i = np.arange(width)
    return jnp.asarray((i[:, None] // head_dim) == (i[None, :] // head_dim), dtype=BF16)


def _order_masks(n, reverse):
    row = lax.broadcasted_iota(jnp.int32, (n, n), 0)
    col = lax.broadcasted_iota(jnp.int32, (n, n), 1)
    diff = col - row if reverse else row - col
    return diff >= 0, diff > 0


def _rwkv_prep_kernel(x_ref, w0_ref, w2_ref, a0_ref, a2_ref, g2_ref, kkw_ref, kaw_ref, hsum_ref,
                      kkn_ref, g_ref, lw_ref, kd_ref, av_ref):
    x = x_ref[...]
    k = x[:, RW_WIDTH:2 * RW_WIDTH]
    o = 3 * RW_WIDTH
    wd = (x[:, o:o + 64], x[:, o + 64:o + 128])
    ad = (x[:, o + 128:o + 192], x[:, o + 192:o + 256])
    gd = x[:, o + 256:o + 384]
    g_ref[...] = _dot(_sigmoid(gd), g2_ref[...])
    kk = k * kkw_ref[...]
    ssq = _dot01_right(kk * kk, hsum_ref[...])
    kkn_ref[...] = kk / jnp.maximum(jnp.sqrt(ssq), 1e-12)
    for d in range(2):
        wl = -_softplus(-(w0_ref[d] + _dot(jnp.tanh(wd[d]), w2_ref[d]))) - 0.5
        lw_ref[d] = -jnp.exp(wl)
        a = _sigmoid(a0_ref[d] + _dot(ad[d], a2_ref[d]))
        kd_ref[d] = k * (1.0 + (a - 1.0) * kaw_ref[...])
        av_ref[d] = a


def _rwkv_scan_kernel(*refs, n_chunks):
    ins = (refs[0:6], refs[6:12])
    s0_ref, yf_ref, yb_ref, sout_ref, s_scr = refs[12:]
    c = pl.program_id(1)
    n = SCAN_CHUNK

    @pl.when(c == 0)
    def _():
        s_scr[...] = s0_ref[...]

    masks = [_order_masks(n, False), _order_masks(n, True)]
    incl01 = [m[0].astype(BF16) for m in masks]
    ch = [(d, h) for d in range(2) for h in range(RW_HEADS)]
    cs = range(len(ch))
    m_incl = [masks[d][0] for d, _ in ch]
    m_strict = [masks[d][1] for d, _ in ch]
    col = lambda j: [ins[d][j][:, h * RW_HEAD_DIM:(h + 1) * RW_HEAD_DIM] for d, h in ch]
    r, v, kkn, lw, kd, av = (col(j) for j in range(6))
    cum = [_dot01_left(incl01[ch[i][0]], lw[i]) for i in cs]
    p_tot = [jnp.exp(jnp.sum(lw[i], axis=0, keepdims=True)) for i in cs]
    p_inv = [jnp.exp(-cum[i]) for i in cs]
    a_t = [-kkn[i] * jnp.exp(cum[i] - lw[i]) for i in cs]
    b_t = [kkn[i] * av[i] * p_inv[i] for i in cs]
    k_t = [kd[i] * p_inv[i] for i in cs]
    r_t = [r[i] * jnp.exp(cum[i]) for i in cs]
    a_ab = [jnp.where(m_strict[i], _dot_nt(a_t[i], b_t[i]), 0.0) for i in cs]
    a_ak = [jnp.where(m_strict[i], _dot_nt(a_t[i], k_t[i]), 0.0) for i in cs]
    a_rb = [jnp.where(m_incl[i], _dot_nt(r_t[i], b_t[i]), 0.0) for i in cs]
    a_rk = [jnp.where(m_incl[i], _dot_nt(r_t[i], k_t[i]), 0.0) for i in cs]
    t_m = list(a_ab)
    n_pow = list(a_ab)
    for _ in range(int(np.log2(n)) - 1):
        n_pow = [_dot(n_pow[i], n_pow[i]) for i in cs]
        t_m = [t_m[i] + n_pow[i] + _dot(t_m[i], n_pow[i]) for i in cs]
    x1 = [_dot(a_ak[i], v[i]) for i in cs]
    w1 = [x1[i] + _dot(t_m[i], x1[i]) for i in cs]
    w2 = [a_t[i] + _dot(t_m[i], a_t[i]) for i in cs]
    y0 = [_dot(a_rk[i], v[i]) + _dot(a_rb[i], w1[i]) for i in cs]
    q = [r_t[i] + _dot(a_rb[i], w2[i]) for i in cs]
    g_m = [_dot_tn(w2[i], b_t[i]) for i in cs]
    h_m = [_dot_tn(w1[i], b_t[i]) + _dot_tn(v[i], k_t[i]) for i in cs]
    s0 = [s_scr[d, h] for d, h in ch]
    ys = [y0[i] + _dot_nt(q[i], s0[i]) for i in cs]
    s_new = [(s0[i] + _dot(s0[i], g_m[i]) + h_m[i]) * p_tot[i] for i in cs]
    for i, (d, h) in enumerate(ch):
        s_scr[d, h] = s_new[i]
    yf_ref[...] = jnp.concatenate(ys[:RW_HEADS], axis=-1)
    yb_ref[...] = jnp.concatenate(ys[RW_HEADS:], axis=-1)

    @pl.when(c == n_chunks - 1)
    def _():
        sout_ref[...] = s_scr[...]


def _rwkv_post_kernel(yf_ref, yb_ref, x_ref, kd_ref, g_ref, rk_ref, lng_ref, lnb_ref, hsum_ref, o_ref):
    x = x_ref[...]
    r = x[:, 0:RW_WIDTH]
    v = x[:, 2 * RW_WIDTH:3 * RW_WIDTH]
    y = yf_ref[...] + yb_ref[...]
    hsum = hsum_ref[...]
    inv_n = 1.0 / RW_HEAD_DIM
    mu = _dot01_right(y, hsum) * inv_n
    yc = y - mu
    var = _dot01_right(yc * yc, hsum) * inv_n
    yn = yc * lax.rsqrt(var + RW_GN_EPS) * lng_ref[...] + lnb_ref[...]
    bonus = _dot01_right(r * (0.5 * (kd_ref[0] + kd_ref[1])) * rk_ref[...], hsum) * v
    o_ref[...] = (yn + bonus) * g_ref[...]


def _rwkv_mixer(xs, groups, lw):
    m = xs.shape[0]
    tm = TOKEN_TILE
    hsum = _head_sum_matrix(RW_WIDTH, RW_HEAD_DIM)
    row = lambda a: a.reshape(1, -1)
    full = lambda shape: pl.BlockSpec(shape, lambda i: (0,) * len(shape))
    tok = lambda w: pl.BlockSpec((tm, w), lambda i: (i, 0))
    tok2 = pl.BlockSpec((2, tm, RW_WIDTH), lambda i: (0, i, 0))
    d2 = jax.ShapeDtypeStruct((2, m, RW_WIDTH), F32)
    d1 = jax.ShapeDtypeStruct((m, RW_WIDTH), F32)
    kkn, g, lwd, kd, av = pl.pallas_call(
        _rwkv_prep_kernel,
        grid=(m // tm,),
        in_specs=[tok(RW_SHIFT_WIDTH), full((2, 1, RW_WIDTH)), full((2, RW_DECAY_LORA, RW_WIDTH)),
                  full((2, 1, RW_WIDTH)), full((2, RW_ICLR_LORA, RW_WIDTH)), full((RW_GATE_LORA, RW_WIDTH)),
                  full((1, RW_WIDTH)), full((1, RW_WIDTH)), full((RW_WIDTH, RW_WIDTH))],
        out_specs=[tok(RW_WIDTH), tok(RW_WIDTH), tok2, tok2, tok2],
        out_shape=[d1, d1, d2, d2, d2],
        compiler_params=_cparams(1),
        name="rwkv_prep",
    )(xs, lw['rw_w0'].reshape(2, 1, RW_WIDTH), lw['rw_w2'], lw['rw_a0'].reshape(2, 1, RW_WIDTH), lw['rw_a2'],
      lw['rw_g2'], row(lw['rw_k_k']), row(lw['rw_k_a']), hsum)

    n = SCAN_CHUNK
    yfs, ybs, states = [], [], []
    for row_off, n_seq, seq_len, s0 in groups:
        nc = seq_len // n
        base = row_off // n
        chunk = lambda b, c, d, nc=nc: b * nc + (nc - 1 - c if d else c)
        x_spec = lambda d, col, base=base, chunk=chunk: pl.BlockSpec(
            (n, RW_WIDTH), lambda b, c: (base + chunk(b, c, d), col))
        d_spec = lambda d, base=base, chunk=chunk: pl.BlockSpec(
            (None, n, RW_WIDTH), lambda b, c: (d, base + chunk(b, c, d), 0))
        y_spec = lambda d, chunk=chunk: pl.BlockSpec((n, RW_WIDTH), lambda b, c: (chunk(b, c, d), 0))
        s_spec = pl.BlockSpec((2, None, RW_HEADS, RW_HEAD_DIM, RW_HEAD_DIM), lambda b, c: (0, b, 0, 0, 0))
        per_dir = lambda d: [x_spec(d, 0), x_spec(d, 2), x_spec(d, 0), d_spec(d), d_spec(d), d_spec(d)]
        y_shape = jax.ShapeDtypeStruct((n_seq * seq_len, RW_WIDTH), F32)
        yf, yb, s_out = pl.pallas_call(
            functools.partial(_rwkv_scan_kernel, n_chunks=nc),
            grid=(n_seq, nc),
            in_specs=per_dir(0) + per_dir(1) + [s_spec],
            out_specs=[y_spec(0), y_spec(1), s_spec],
            out_shape=[y_shape, y_shape,
                       jax.ShapeDtypeStruct((2, n_seq, RW_HEADS, RW_HEAD_DIM, RW_HEAD_DIM), F32)],
            scratch_shapes=[pltpu.VMEM((2, RW_HEADS, RW_HEAD_DIM, RW_HEAD_DIM), F32)],
            compiler_params=_cparams(2),
            name="rwkv_scan",
        )(*([xs, xs, kkn, lwd, kd, av] * 2), s0)
        yfs.append(yf)
        ybs.append(yb)
        states.append(s_out)
    yf = jnp.concatenate(yfs, axis=0)
    yb = jnp.concatenate(ybs, axis=0)

    out = pl.pallas_call(
        _rwkv_post_kernel,
        grid=(m // tm,),
        in_specs=[tok(RW_WIDTH), tok(RW_WIDTH), tok(RW_SHIFT_WIDTH), tok2, tok(RW_WIDTH), full((1, RW_WIDTH)),
                  full((1, RW_WIDTH)), full((1, RW_WIDTH)), full((RW_WIDTH, RW_WIDTH))],
        out_specs=tok(RW_WIDTH),
        out_shape=d1,
        compiler_params=_cparams(1),
        name="rwkv_post",
    )(yf, yb, xs, kd, g, row(lw['rw_r_k']), row(lw['rw_ln_g']), row(lw['rw_ln_b']), hsum)
    return out, states


HG_LEVELS = (32, 16, 8, 4, 2, 1)


def _hgrn_consts():
    n = SCAN_CHUNK
    cums, masks = [], []
    for rev in (False, True):
        p = np.arange(n)[::-1] if rev else np.arange(n)
        pt, pj = p[:, None], p[None, :]
        rows = [pj <= pt, pj > pt]
        rows += [((pt // m) % 2 == 1) & (pj // m == pt // m) & (pj <= pt) for m in HG_LEVELS]
        rows += [((pt // m) % 2 == 0) & (pj // m == pt // m) & (pj > pt) for m in HG_LEVELS]
        cums.append(np.concatenate(rows, 0))
        mk = [pt == pj]
        mk += [(pt // (2 * m) == pj // (2 * m)) & ((pt // m) % 2 == 1) & ((pj // m) % 2 == 0) for m in HG_LEVELS]
        masks.append(np.stack(mk))
    return jnp.asarray(np.stack(cums), BF16), jnp.asarray(np.stack(masks), F32)


def _hgrn_scan_kernel(qf_ref, zf_ref, vf_ref, qb_ref, zb_ref, vb_ref, lb_ref, cm_ref, mask_ref, s0_ref,
                      of_ref, ob_ref, sout_ref, s_scr, *, n_chunks):
    c = pl.program_id(1)
    n = SCAN_CHUNK
    nl = len(HG_LEVELS)
    ins = ((qf_ref, zf_ref, vf_ref), (qb_ref, zb_ref, vb_ref))

    @pl.when(c == 0)
    def _():
        s_scr[...] = s0_ref[...]

    ones = jnp.ones((n, HG_VAL_DIM), BF16)
    tn01 = lambda p: lax.dot_general(p, ones, (((0,), (0,)), ((), ())), preferred_element_type=F32)
    ch = [(d, h) for d in range(2) for h in range(HG_HEADS)]
    cs = range(len(ch))
    col = lambda j: [ins[d][j][:, h * HG_KEY_DIM:(h + 1) * HG_KEY_DIM] for d, h in ch]
    hq, z, v = col(0), col(1), col(2)
    lb = [lb_ref[d, :, h * HG_KEY_DIM:(h + 1) * HG_KEY_DIM] for d, h in ch]
    q = [hq[i] * _sigmoid(hq[i]) for i in cs]
    lf = [jnp.log(jnp.maximum(lb[i] + (1.0 - lb[i]) * _sigmoid(z[i]), HG_MIN_F)) for i in cs]
    k = [(1.0 - lb[i]) * _sigmoid(-z[i]) for i in cs]
    parts = [_split3(lf[i]) for i in cs]
    ex = [jnp.exp(sum(jnp.dot(cm_ref[ch[i][0]], p, preferred_element_type=F32) for p in parts[i][::-1])) for i in cs]
    p_col = [jnp.exp(sum(tn01(p) for p in parts[i][::-1])) for i in cs]
    att = [mask_ref[ch[i][0], 0] * _dot_nt(q[i], k[i]) for i in cs]
    for lv in range(nl):
        lo_q, lo_k = (2 + lv) * n, (2 + nl + lv) * n
        lvl = [_dot_nt(q[i] * ex[i][lo_q:lo_q + n], k[i] * ex[i][lo_k:lo_k + n]) for i in cs]
        att = [att[i] + mask_ref[ch[i][0], lv + 1] * lvl[i] for i in cs]
    s0 = [s_scr[d, h] for d, h in ch]
    outs = [_dot(q[i] * ex[i][0:n], s0[i]) + _dot(att[i], v[i]) for i in cs]
    s_new = [s0[i] * p_col[i] + _dot_tn(k[i] * ex[i][n:2 * n], v[i]) for i in cs]
    for i, (d, h) in enumerate(ch):
        s_scr[d, h] = s_new[i]
    of_ref[...] = jnp.concatenate(outs[:HG_HEADS], axis=-1)
    ob_ref[...] = jnp.concatenate(outs[HG_HEADS:], axis=-1)

    @pl.when(c == n_chunks - 1)
    def _():
        sout_ref[...] = s_scr[...]


def _hgrn_post_kernel(of_ref, ob_ref, g_ref, gain_ref, hsum_ref, out_ref):
    o = of_ref[...] + ob_ref[...]
    ms = _dot01_right(o * o, hsum_ref[...]) * (1.0 / HG_VAL_DIM)
    on = o * lax.rsqrt(ms + RMS_EPS) * gain_ref[...]
    g = g_ref[...]
    out_ref[...] = on * (g * _sigmoid(g))


def _hgrn_mixer(proj, groups, lb, gain):
    m = proj.shape[0]
    n = SCAN_CHUNK
    cm, masks = _hgrn_consts()
    cq, cff, ci, cg = (col // HG_WIDTH for col in (PCOL_HQ, PCOL_HFF, PCOL_HI, PCOL_HG))
    ofs, obs, states = [], [], []
    for row_off, n_seq, seq_len, s0 in groups:
        nc = seq_len // n
        base = row_off // n
        chunk = lambda b, c, d, nc=nc: b * nc + (nc - 1 - c if d else c)
        x_spec = lambda d, col, base=base, chunk=chunk: pl.BlockSpec(
            (n, HG_WIDTH), lambda b, c: (base + chunk(b, c, d), col))
        o_spec = lambda d, chunk=chunk: pl.BlockSpec((n, HG_WIDTH), lambda b, c: (chunk(b, c, d), 0))
        s_spec = pl.BlockSpec((2, None, HG_HEADS, HG_KEY_DIM, HG_VAL_DIM), lambda b, c: (0, b, 0, 0, 0))
        whole = lambda a: pl.BlockSpec(a.shape, lambda b, c: (0,) * a.ndim)
        lb3 = lb.reshape(2, 1, HG_WIDTH)
        o_shape = jax.ShapeDtypeStruct((n_seq * seq_len, HG_WIDTH), F32)
        of, ob, s_out = pl.pallas_call(
            functools.partial(_hgrn_scan_kernel, n_chunks=nc),
            grid=(n_seq, nc),
            in_specs=[x_spec(0, cq), x_spec(0, cff), x_spec(0, ci), x_spec(1, cq), x_spec(1, cff + 1), x_spec(1, ci),
                      whole(lb3), whole(cm), whole(masks), s_spec],
            out_specs=[o_spec(0), o_spec(1), s_spec],
            out_shape=[o_shape, o_shape,
                       jax.ShapeDtypeStruct((2, n_seq, HG_HEADS, HG_KEY_DIM, HG_VAL_DIM), F32)],
            scratch_shapes=[pltpu.VMEM((2, HG_HEADS, HG_KEY_DIM, HG_VAL_DIM), F32)],
            compiler_params=_cparams(2),
            name="hgrn_scan",
        )(proj, proj, proj, proj, proj, proj, lb3, cm, masks, s0)
        ofs.append(of)
        obs.append(ob)
        states.append(s_out)
    of = jnp.concatenate(ofs, axis=0)
    ob = jnp.concatenate(obs, axis=0)

    tm = TOKEN_TILE
    tok = pl.BlockSpec((tm, HG_WIDTH), lambda i: (i, 0))
    out = pl.pallas_call(
        _hgrn_post_kernel,
        grid=(m // tm,),
        in_specs=[tok, tok,
                  pl.BlockSpec((tm, HG_WIDTH), lambda i: (i, cg)),
                  pl.BlockSpec((1, HG_WIDTH), lambda i: (0, 0)),
                  pl.BlockSpec((HG_WIDTH, HG_WIDTH), lambda i: (0, 0))],
        out_specs=pl.BlockSpec((tm, HG_WIDTH), lambda i: (i, 0)),
        out_shape=jax.ShapeDtypeStruct((m, HG_WIDTH), F32),
        compiler_params=_cparams(1),
        name="hgrn_post",
    )(of, ob, proj, gain.reshape(1, HG_WIDTH), _head_sum_matrix(HG_WIDTH, HG_VAL_DIM))
    return out, states


NA_HEADS_PER_STEP = LANES // NA_HEAD_DIM
NA_ROWS_PER_ITER = 4


def _ctx_attn_kernel(q_ref, k_ref, v_ref, o_ref, kc_ref, vc_ref):
    scale = NA_HEAD_DIM ** -0.5
    hs = range(NA_HEADS_PER_STEP)
    sl = [slice(h * NA_HEAD_DIM, (h + 1) * NA_HEAD_DIM) for h in hs]
    k = [k_ref[:, s] for s in sl]
    v = [v_ref[:, s] for s in sl]
    for h in hs:
        kc_ref[h] = k[h]
        vc_ref[h] = v[h]
    s = [_dot_nt(q_ref[:, sl[h]], k[h]) * scale for h in hs]
    p = [jnp.exp(s[h] - jnp.max(s[h], axis=-1, keepdims=True)) for h in hs]
    o = [_dot(p[h], v[h]) / jnp.sum(p[h], axis=-1, keepdims=True) for h in hs]
    o_ref[...] = jnp.concatenate(o, axis=-1)


def _ctx_attention(proj, n_seq, seq_len):
    nb = NA_WIDTH // LANES
    blk = lambda off: pl.BlockSpec((seq_len, LANES), lambda b, j: (b, off + j))
    c_spec = pl.BlockSpec((None, NA_HEADS_PER_STEP, seq_len, NA_HEAD_DIM), lambda b, j: (b, j, 0, 0))
    c_shape = jax.ShapeDtypeStruct((n_seq, NA_HEADS, seq_len, NA_HEAD_DIM), F32)
    return pl.pallas_call(
        _ctx_attn_kernel,
        grid=(n_seq, nb),
        in_specs=[blk(PCOL_Q // LANES), blk(PCOL_K // LANES), blk(PCOL_V // LANES)],
        out_specs=[pl.BlockSpec((seq_len, LANES), lambda b, j: (b, j)), c_spec, c_spec],
        out_shape=[jax.ShapeDtypeStruct((n_seq * seq_len, NA_WIDTH), F32), c_shape, c_shape],
        compiler_params=_cparams(2),
        name="ctx_attention",
    )(proj, proj, proj)


def _na_bias_table(rpb, rows):
    assert rows >= NA_KH
    o = np.arange(NA_KH)[:, None]
    i = np.arange(NA_KH)[None, :]
    dr = i - o + (NA_KH - 1)
    cq = np.arange(GRID_W)
    ck = np.arange(GRID_W)
    col_start = np.clip(cq - NA_KW // 2, 0, GRID_W - NA_KW)
    col_mask = (ck[None, :] >= col_start[:, None]) & (ck[None, :] < col_start[:, None] + NA_KW)
    dc = np.clip(ck[None, :] - cq[:, None], -(NA_KW - 1), NA_KW - 1) + (NA_KW - 1)
    onehot = jnp.asarray(dc.reshape(-1)[:, None] == np.arange(2 * NA_KW - 1)[None, :], F32)
    full = jnp.einsum('hrj,pj->hrp', rpb.astype(F32), onehot, precision=lax.Precision.HIGHEST)
    full = jnp.where(col_mask.reshape(-1)[None, None, :], full, NEG_INF).reshape(rpb.shape[0], -1, GRID_W, GRID_W)
    lo = lambda off: int(dr[off, 0])
    tab = jnp.stack([full[:, lo(off):lo(off) + NA_KH] for off in range(NA_KH)])
    return tab.transpose(0, 1, 3, 2, 4).reshape(NA_KH, rpb.shape[0], GRID_W, NA_KH * GRID_W)


def _na_kernel(q_ref, k_ref, v_ref, ck_ref, cv_ref, bias_ref, o_ref, *, rows):
    scale = NA_HEAD_DIM ** -0.5
    n_win = NA_KH * GRID_W

    def row_group(g, carry):
        ch = []
        for i in range(NA_ROWS_PER_ITER):
            r = g * NA_ROWS_PER_ITER + i
            r0 = jnp.clip(r - NA_KH // 2, 0, rows - NA_KH)
            q_at = pl.multiple_of(r * GRID_W, GRID_W)
            k_at = pl.multiple_of(r0 * GRID_W, GRID_W)
            for h in range(NA_HEADS_PER_STEP):
                ch.append((q_at, k_at, r - r0, h, slice(h * NA_HEAD_DIM, (h + 1) * NA_HEAD_DIM)))
        q = [q_ref[pl.ds(q_at, GRID_W), sl] for q_at, _, _, _, sl in ch]
        s_w = [_dot_nt(q[i], k_ref[pl.ds(k_at, n_win), sl]) * scale + bias_ref[off, h]
               for i, (_, k_at, off, h, sl) in enumerate(ch)]
        s_c = [_dot_nt(q[i], ck_ref[h]) * scale for i, (_, _, _, h, _) in enumerate(ch)]
        m = [jnp.maximum(jnp.max(a, axis=-1, keepdims=True), jnp.max(b, axis=-1, keepdims=True))
             for a, b in zip(s_w, s_c)]
        p_w = [jnp.exp(a - mm) for a, mm in zip(s_w, m)]
        p_c = [jnp.exp(b - mm) for b, mm in zip(s_c, m)]
        den = [jnp.sum(a, axis=-1, keepdims=True) + jnp.sum(b, axis=-1, keepdims=True) for a, b in zip(p_w, p_c)]
        o = [(_dot(p_w[i], v_ref[pl.ds(k_at, n_win), sl]) + _dot(p_c[i], cv_ref[h])) / den[i]
             for i, (_, k_at, _, h, sl) in enumerate(ch)]
        for i, (q_at, _, _, _, sl) in enumerate(ch):
            o_ref[pl.ds(q_at, GRID_W), sl] = o[i]
        return carry

    lax.fori_loop(0, rows // NA_ROWS_PER_ITER, row_group, 0)


def _na_attention(proj, row_off, n_seq, seq_len, cache_k, cache_v, layer, bias):
    assert row_off % seq_len == 0
    nb = NA_WIDTH // LANES
    rows = seq_len // GRID_W
    assert rows % NA_ROWS_PER_ITER == 0
    rb = row_off // seq_len
    past = cache_k.shape[3]
    blk = lambda off: pl.BlockSpec((seq_len, LANES), lambda b, j: (rb + b, off + j))
    c_spec = pl.BlockSpec((None, None, NA_HEADS_PER_STEP, past, NA_HEAD_DIM), lambda b, j: (b, layer, j, 0, 0))
    return pl.pallas_call(
        functools.partial(_na_kernel, rows=rows),
        grid=(n_seq, nb),
        in_specs=[blk(PCOL_Q // LANES), blk(PCOL_K // LANES), blk(PCOL_V // LANES), c_spec, c_spec,
                  pl.BlockSpec((NA_KH, NA_HEADS_PER_STEP, GRID_W, NA_KH * GRID_W), lambda b, j: (0, j, 0, 0))],
        out_specs=pl.BlockSpec((seq_len, LANES), lambda b, j: (b, j)),
        out_shape=jax.ShapeDtypeStruct((n_seq * seq_len, NA_WIDTH), F32),
        compiler_params=_cparams(2),
        name="na_attention",
    )(proj, proj, proj, cache_k, cache_v, bias)


ADA_TN = 1024
IN_TM = 512
IN_TN = 1536
OUT_TM = 256


def _mod_row_fn(tm, n_ctx_rows, latent_len):
    def f(i):
        start = i * tm
        return jnp.where(start < n_ctx_rows, 0, 1 + (start - n_ctx_rows) // latent_len)
    return f


def _ada_kernel(c_ref, w_ref, b_ref, o_ref):
    x = c_ref[...]
    o_ref[...] = _dot(x * _sigmoid(x), w_ref[...]) + b_ref[...]


def _ada(cond, w_ada, b_ada, layer):
    n_rows, d = cond.shape
    n = w_ada.shape[2]
    return pl.pallas_call(
        _ada_kernel,
        grid=(n // ADA_TN,),
        in_specs=[pl.BlockSpec((n_rows, d), lambda j: (0, 0)),
                  pl.BlockSpec((None, d, ADA_TN), lambda j: (layer, 0, j)),
                  pl.BlockSpec((None, 1, ADA_TN), lambda j: (layer, 0, j))],
        out_specs=pl.BlockSpec((n_rows, ADA_TN), lambda j: (0, j)),
        out_shape=jax.ShapeDtypeStruct((n_rows, n), F32),
        compiler_params=_cparams(1),
        name="ada_mod",
    )(cond, w_ada, b_ada.reshape(b_ada.shape[0], 1, n))


def _rms_mod(x, g, sc, sh):
    y = x * lax.rsqrt(jnp.mean(x * x, axis=-1, keepdims=True) + RMS_EPS) * g
    return y * (1.0 + sc) + sh


def _in_proj_kernel(x_ref, g_ref, sh_ref, sc_ref, w_ref, o_ref, h_scr):
    @pl.when(pl.program_id(1) == 0)
    def _():
        h_scr[...] = _rms_mod(x_ref[...], g_ref[...], sc_ref[...], sh_ref[...]).astype(BF16)

    o_ref[...] = jnp.dot(h_scr[...], w_ref[...], preferred_element_type=F32)


def _in_proj(x, norm_g, mod, w, mod_row):
    m, d = x.shape
    n = w.shape[1]
    row = mod_row(IN_TM)
    return pl.pallas_call(
        _in_proj_kernel,
        grid=(m // IN_TM, n // IN_TN),
        in_specs=[pl.BlockSpec((IN_TM, d), lambda i, j: (i, 0)),
                  pl.BlockSpec((1, d), lambda i, j: (0, 0)),
                  pl.BlockSpec((None, 1, d), lambda i, j: (row(i), 0, 0)),
                  pl.BlockSpec((None, 1, d), lambda i, j: (row(i), 0, 1)),
                  pl.BlockSpec((d, IN_TN), lambda i, j: (0, j))],
        out_specs=pl.BlockSpec((IN_TM, IN_TN), lambda i, j: (i, j)),
        out_shape=jax.ShapeDtypeStruct((m, n), F32),
        scratch_shapes=[pltpu.VMEM((IN_TM, d), BF16)],
        compiler_params=_cparams(2),
        name="in_proj",
    )(x, norm_g.reshape(1, d), mod, mod, w)


def _out_proj_kernel(na_ref, rw_ref, hg_ref, x_ref, w_ref, g1_ref, n2_ref, sh2_ref, sc2_ref, rtw_ref, rtb_ref,
                     xo_ref, h2_ref, idx_ref, gate_ref):
    o1, o2 = NA_WIDTH, NA_WIDTH + RW_WIDTH
    mix = _dot(na_ref[...], w_ref[0:o1]) + _dot(rw_ref[...], w_ref[o1:o2]) + _dot(hg_ref[...], w_ref[o2:])
    x = x_ref[...] + g1_ref[...] * mix
    xo_ref[...] = x
    h = _rms_mod(x, n2_ref[...], sc2_ref[...], sh2_ref[...])
    h2_ref[...] = h.astype(BF16)
    logits = jnp.dot(h, rtw_ref[...], precision=lax.Precision.HIGHEST, preferred_element_type=F32) + rtb_ref[...]
    lane = lax.broadcasted_iota(jnp.int32, logits.shape, 1).astype(F32)
    idx = jnp.zeros_like(logits)
    vals = []
    for k in range(TOP_K):
        mx = jnp.max(logits, axis=-1, keepdims=True)
        sel = jnp.min(jnp.where(logits == mx, lane, float(LANES)), axis=-1, keepdims=True)
        idx = jnp.where(lane == k, sel, idx)
        vals.append(mx)
        logits = jnp.where(lane == sel, 3.0 * NEG_INF, logits)
    es = [jnp.exp(v - vals[0]) for v in vals]
    den = es[0] + es[1] + es[2] + es[3]
    gate = jnp.zeros_like(idx)
    for k in range(TOP_K):
        gate = jnp.where(lane == k, es[k] / den, gate)
    idx_ref[...] = idx.astype(jnp.int32)
    gate_ref[...] = gate


def _out_proj(na_o, rw_o, hg_o, x, w_out, mod, norm2_g, router_w, router_b, mod_row):
    m, d = x.shape
    tm = OUT_TM
    row = mod_row(tm)
    n_exp = router_w.shape[1]
    rtw = jnp.pad(router_w, ((0, 0), (0, LANES - n_exp)))
    rtb = jnp.pad(router_b, (0, LANES - n_exp), constant_values=NEG_INF).reshape(1, LANES)
    tok = lambda w: pl.BlockSpec((tm, w), lambda i: (i, 0))
    mod_blk = lambda part: pl.BlockSpec((None, 1, d), lambda i: (row(i), 0, part))
    return pl.pallas_call(
        _out_proj_kernel,
        grid=(m // tm,),
        in_specs=[tok(NA_WIDTH), tok(RW_WIDTH), tok(HG_WIDTH), tok(d),
                  pl.BlockSpec((d, d), lambda i: (0, 0)),
                  mod_blk(2), pl.BlockSpec((1, d), lambda i: (0, 0)), mod_blk(3), mod_blk(4),
                  pl.BlockSpec((d, LANES), lambda i: (0, 0)), pl.BlockSpec((1, LANES), lambda i: (0, 0))],
        out_specs=[tok(d), tok(d), tok(LANES), tok(LANES)],
        out_shape=[jax.ShapeDtypeStruct((m, d), F32), jax.ShapeDtypeStruct((m, d), BF16),
                   jax.ShapeDtypeStruct((m, LANES), jnp.int32), jax.ShapeDtypeStruct((m, LANES), F32)],
        compiler_params=_cparams(1),
        name="out_proj_router",
    )(na_o, rw_o, hg_o, x, w_out, mod, norm2_g.reshape(1, d), mod, mod, rtw, rtb)


def _final_norm_kernel(x_ref, g_ref, o_ref):
    x = x_ref[...]
    o_ref[...] = x * lax.rsqrt(jnp.mean(x * x, axis=-1, keepdims=True) + RMS_EPS) * g_ref[...]


def _final_norm(x, g):
    m, d = x.shape
    tm = TOKEN_TILE
    return pl.pallas_call(
        _final_norm_kernel,
        grid=(m // tm,),
        in_specs=[pl.BlockSpec((tm, d), lambda i: (i, 0)), pl.BlockSpec((1, d), lambda i: (0, 0))],
        out_specs=pl.BlockSpec((tm, d), lambda i: (i, 0)),
        out_shape=jax.ShapeDtypeStruct((m, d), F32),
        compiler_params=_cparams(1),
        name="final_norm",
    )(x, g.reshape(1, d))


MOE_TM = 512
GU_TN = 512
DOWN_TN = 1024


def _moe_dispatch(top_e, n_exp):
    n_tok, top_k = top_e.shape
    n_assign = n_tok * top_k
    flat_e = top_e.reshape(-1)
    order = jnp.argsort(flat_e)
    sorted_e = flat_e[order]
    counts = jnp.bincount(flat_e, length=n_exp)
    padded = (counts + MOE_TM - 1) // MOE_TM * MOE_TM
    pad_end = jnp.cumsum(padded)
    pad_start = pad_end - padded
    start = jnp.cumsum(counts) - counts
    slot_sorted = (pad_start[sorted_e] + jnp.arange(n_assign) - start[sorted_e]).astype(jnp.int32)
    n_blocks = -(-n_assign // MOE_TM) + n_exp
    slot_token = jnp.zeros((n_blocks * MOE_TM,), jnp.int32).at[slot_sorted].set((order // top_k).astype(jnp.int32))
    slot_of_assign = jnp.zeros((n_assign,), jnp.int32).at[order].set(slot_sorted)
    blk = jnp.arange(n_blocks)
    block_expert = jnp.sum((pad_end[None, :] <= (blk * MOE_TM)[:, None]).astype(jnp.int32), axis=1)
    block_expert = jnp.minimum(block_expert, n_exp - 1).astype(jnp.int32)
    n_active = (pad_end[-1] // MOE_TM).astype(jnp.int32)
    block_expert = jnp.where(blk < n_active, block_expert, block_expert[jnp.maximum(n_active - 1, 0)])
    return slot_token, slot_of_assign, block_expert, n_active.reshape(1)


def _expert_changed(be_ref, m):
    return (m == 0) | (be_ref[m] != be_ref[jnp.maximum(m - 1, 0)])


def _moe_gu_kernel(be_ref, na_ref, x_ref, wg_ref, wu_ref, bg_ref, bu_ref, o_ref, wg_s, wu_s):
    m = pl.program_id(1)

    @pl.when(_expert_changed(be_ref, m))
    def _():
        wg_s[...] = wg_ref[...].astype(BF16)
        wu_s[...] = wu_ref[...].astype(BF16)

    @pl.when(m < na_ref[0])
    def _():
        x = x_ref[...]
        gate = jnp.dot(x, wg_s[...], preferred_element_type=F32) + bg_ref[...]
        up = jnp.dot(x, wu_s[...], preferred_element_type=F32) + bu_ref[...]
        gate = jnp.minimum(gate, SWIGLU_LIMIT)
        up = jnp.clip(up, -SWIGLU_LIMIT, SWIGLU_LIMIT)
        o_ref[...] = ((up + 1.0) * gate * _sigmoid(SWIGLU_ALPHA * gate)).astype(BF16)

    @pl.when(m >= na_ref[0])
    def _():
        o_ref[...] = jnp.zeros_like(o_ref)


def _moe_down_kernel(be_ref, na_ref, a_ref, w_ref, b_ref, o_ref, w_s):
    m = pl.program_id(1)

    @pl.when(_expert_changed(be_ref, m))
    def _():
        w_s[...] = w_ref[...].astype(BF16)

    @pl.when(m < na_ref[0])
    def _():
        o_ref[...] = jnp.dot(a_ref[...], w_s[...], preferred_element_type=F32) + b_ref[...]

    @pl.when(m >= na_ref[0])
    def _():
        o_ref[...] = jnp.zeros_like(o_ref)


def _moe_experts(xb, block_expert, n_active, w_gu, b_gu, w_down, b_down, layer):
    n_slots, d = xb.shape
    n_blocks = n_slots // MOE_TM
    n_exp, _, two_f = w_gu.shape[1:]
    f = two_f // 2
    nj = f // GU_TN
    act = pl.pallas_call(
        _moe_gu_kernel,
        grid_spec=pltpu.PrefetchScalarGridSpec(
            num_scalar_prefetch=2,
            grid=(nj, n_blocks),
            in_specs=[pl.BlockSpec((MOE_TM, d), lambda j, m, be, na: (m, 0)),
                      pl.BlockSpec((None, None, d, GU_TN), lambda j, m, be, na: (layer, be[m], 0, j)),
                      pl.BlockSpec((None, None, d, GU_TN), lambda j, m, be, na: (layer, be[m], 0, nj + j)),
                      pl.BlockSpec((None, None, 1, GU_TN), lambda j, m, be, na: (layer, be[m], 0, j)),
                      pl.BlockSpec((None, None, 1, GU_TN), lambda j, m, be, na: (layer, be[m], 0, nj + j))],
            out_specs=pl.BlockSpec((MOE_TM, GU_TN), lambda j, m, be, na: (m, j)),
            scratch_shapes=[pltpu.VMEM((d, GU_TN), BF16), pltpu.VMEM((d, GU_TN), BF16)]),
        out_shape=jax.ShapeDtypeStruct((n_slots, f), BF16),
        compiler_params=_cparams(2),
        name="moe_gate_up",
    )(block_expert, n_active, xb, w_gu, w_gu, b_gu.reshape(b_gu.shape[0], n_exp, 1, two_f),
      b_gu.reshape(b_gu.shape[0], n_exp, 1, two_f))
    nd = d // DOWN_TN
    return pl.pallas_call(
        _moe_down_kernel,
        grid_spec=pltpu.PrefetchScalarGridSpec(
            num_scalar_prefetch=2,
            grid=(nd, n_blocks),
            in_specs=[pl.BlockSpec((MOE_TM, f), lambda j, m, be, na: (m, 0)),
                      pl.BlockSpec((None, None, f, DOWN_TN), lambda j, m, be, na: (layer, be[m], 0, j)),
                      pl.BlockSpec((None, None, 1, DOWN_TN), lambda j, m, be, na: (layer, be[m], 0, j))],
            out_specs=pl.BlockSpec((MOE_TM, DOWN_TN), lambda j, m, be, na: (m, j)),
            scratch_shapes=[pltpu.VMEM((f, DOWN_TN), BF16)]),
        out_shape=jax.ShapeDtypeStruct((n_slots, d), F32),
        compiler_params=_cparams(2),
        name="moe_down",
    )(block_expert, n_active, act, w_down, b_down.reshape(b_down.shape[0], n_exp, 1, d))


GATHER_ROWS = 256
COMBINE_TM = 64
DMA_PRIORITIES = 2


def _gather_rows_kernel(idx_ref, nxt_ref, src_ref, o_ref, buf, sems):
    i = pl.program_id(0)
    n = pl.num_programs(0)
    slot = i % 2

    def copy(ids, s, t):
        return pltpu.make_async_copy(src_ref.at[pl.ds(ids[0, t], 1)], buf.at[s, pl.ds(t, 1)], sems.at[s])

    def issue(ids, s):
        def body(g, carry):
            for p in range(DMA_PRIORITIES):
                copy(ids, s, g * DMA_PRIORITIES + p).start(priority=p)
            return carry
        lax.fori_loop(0, GATHER_ROWS // DMA_PRIORITIES, body, 0, unroll=4)

    @pl.when(i == 0)
    def _():
        issue(idx_ref, 0)

    @pl.when(i + 1 < n)
    def _():
        issue(nxt_ref, 1 - slot)

    pltpu.make_async_copy(src_ref.at[pl.ds(0, GATHER_ROWS)], buf.at[slot], sems.at[slot]).wait()
    o_ref[...] = buf[slot]


def _as_row_tiles(a):
    return a.reshape(a.shape[0], a.shape[1] // LANES, LANES)


def _gather_rows(src, idx):
    n = idx.shape[0]
    nb = n // GATHER_ROWS
    ids = idx.reshape(nb, 1, GATHER_ROWS)
    smem_blk = lambda f: pl.BlockSpec((None, 1, GATHER_ROWS), f, memory_space=pltpu.SMEM)
    blk = (GATHER_ROWS,) + src.shape[1:]
    return pl.pallas_call(
        _gather_rows_kernel,
        grid=(nb,),
        in_specs=[smem_blk(lambda i: (i, 0, 0)), smem_blk(lambda i: (jnp.minimum(i + 1, nb - 1), 0, 0)),
                  pl.BlockSpec(memory_space=pl.ANY)],
        out_specs=pl.BlockSpec(blk, lambda i: (i, 0, 0)),
        out_shape=jax.ShapeDtypeStruct((n,) + src.shape[1:], src.dtype),
        scratch_shapes=[pltpu.VMEM((2,) + blk, src.dtype), pltpu.SemaphoreType.DMA((2,))],
        compiler_params=_cparams(1),
        name="moe_dispatch_gather",
    )(ids, ids, src)


def _combine_kernel(idx_ref, nxt_ref, yb_ref, x_ref, gate_ref, g2_ref, o_ref, buf, sems, *, top_k):
    i = pl.program_id(0)
    n = pl.num_programs(0)
    slot = i % 2

    def copy(ids, s, t, k):
        return pltpu.make_async_copy(yb_ref.at[pl.ds(ids[0, t * top_k + k], 1)], buf.at[s, k, pl.ds(t, 1)], sems.at[s])

    def issue(ids, s):
        def body(t, carry):
            for k in range(top_k):
                copy(ids, s, t, k).start(priority=k % DMA_PRIORITIES)
            return carry
        lax.fori_loop(0, COMBINE_TM, body, 0, unroll=2)

    @pl.when(i == 0)
    def _():
        issue(idx_ref, 0)

    @pl.when(i + 1 < n)
    def _():
        issue(nxt_ref, 1 - slot)

    for k in range(top_k):
        pltpu.make_async_copy(yb_ref.at[pl.ds(0, COMBINE_TM)], buf.at[slot, k], sems.at[slot]).wait()

    gate = gate_ref[...]
    g = [gate[:, k:k + 1] for k in range(top_k)]
    for c in range(buf.shape[3]):
        acc = g[0] * buf[slot, 0, :, c, :]
        for k in range(1, top_k):
            acc = acc + g[k] * buf[slot, k, :, c, :]
        cs = slice(c * LANES, (c + 1) * LANES)
        o_ref[:, cs] = x_ref[:, cs] + g2_ref[:, cs] * acc


def _moe_combine(yb, slot_of_assign, gate, x, mod, mod_row, top_k):
    m, d = x.shape
    tm = COMBINE_TM
    nt = m // tm
    row = mod_row(tm)
    ids = slot_of_assign.reshape(nt, 1, tm * top_k)
    smem_blk = lambda f: pl.BlockSpec((None, 1, tm * top_k), f, memory_space=pltpu.SMEM)
    return pl.pallas_call(
        functools.partial(_combine_kernel, top_k=top_k),
        grid=(nt,),
        in_specs=[smem_blk(lambda i: (i, 0, 0)), smem_blk(lambda i: (jnp.minimum(i + 1, nt - 1), 0, 0)),
                  pl.BlockSpec(memory_space=pl.ANY),
                  pl.BlockSpec((tm, d), lambda i: (i, 0)),
                  pl.BlockSpec((tm, LANES), lambda i: (i, 0)),
                  pl.BlockSpec((None, 1, d), lambda i: (row(i), 0, 5))],
        out_specs=pl.BlockSpec((tm, d), lambda i: (i, 0)),
        out_shape=jax.ShapeDtypeStruct((m, d), F32),
        scratch_shapes=[pltpu.VMEM((2, top_k, tm) + yb.shape[1:], F32), pltpu.SemaphoreType.DMA((2,))],
        compiler_params=_cparams(1),
        name="moe_combine",
    )(ids, ids, yb, x, gate, mod)


def _moe(h2, idx, gate, x, mod, mod_row, w_gu, b_gu, w_down, b_down, layer):
    n_exp = w_gu.shape[1]
    slot_token, slot_of_assign, block_expert, n_active = _moe_dispatch(idx[:, :TOP_K], n_exp)
    xb = _gather_rows(_as_row_tiles(h2), slot_token).reshape(slot_token.shape[0], h2.shape[1])
    yb = _moe_experts(xb, block_expert, n_active, w_gu, b_gu, w_down, b_down, layer)
    return _moe_combine(_as_row_tiles(yb), slot_of_assign, gate, x, mod, mod_row, TOP_K)


def _token_shift_mix(xr, mu, groups):
    parts = []
    for row_off, n_seq, seq_len in groups:
        x = xr[row_off:row_off + n_seq * seq_len].reshape(n_seq, seq_len, -1)
        xp = jnp.pad(x, ((0, 0), (1, 1), (0, 0)))
        sh = 0.5 * (xp[:, :-2] + xp[:, 2:])
        parts.append((x + (sh - x) * mu).reshape(n_seq * seq_len, -1))
    return jnp.concatenate(parts, axis=0)


def _pad_w_in(w):
    wb = w.astype(BF16)
    cut = 3 * NA_WIDTH + RW_SHIFT_WIDTH
    return jnp.concatenate([wb[:, :cut], jnp.zeros((w.shape[0], RW_PAD), BF16), wb[:, cut:]], axis=1)


def kernel(x_prompt, x_sample, c, cache_k, cache_v, state_rwkv_fwd, state_rwkv_bwd, state_hgrn_fwd, state_hgrn_bwd, c_ctx, norm1_g, norm2_g, w_ada, b_ada, w_in, w_out, na_rpb, rw_mu, rw_w0, rw_w2, rw_a0, rw_a2, rw_g2, rw_k_k, rw_k_a, rw_r_k, rw_ln_g, rw_ln_b, hg_lb_raw, hg_norm_g, router_w, router_b, exp_w_gu, exp_b_gu, exp_w_down, exp_b_down, final_g):
    n_p, t_p, d = x_prompt.shape
    n_s, t_s, _ = x_sample.shape
    depth = w_in.shape[0]
    mp, ms = n_p * t_p, n_s * t_s
    x = jnp.concatenate([x_prompt.reshape(mp, d), x_sample.reshape(ms, d)], axis=0)
    n_mod_rows = 8
    cond = jnp.concatenate([c_ctx[None], c, jnp.zeros((n_mod_rows - 1 - n_s, d), F32)], axis=0)
    mod_row = lambda tm: _mod_row_fn(tm, mp, t_s)
    p_lb = jax.nn.softmax(hg_lb_raw.astype(F32), axis=0)
    hg_lb = jnp.cumsum(p_lb, axis=0) - p_lb[:1]
    seqs = [(0, n_p, t_p), (mp, n_s, t_s)]
    ctx_out = []
    for l in range(depth):
        mod2d = _ada(cond, w_ada, b_ada, l)
        mod = mod2d.reshape(n_mod_rows, 1, 6 * d)
        proj = _in_proj(x, norm1_g[l], mod, _pad_w_in(w_in[l]), mod_row)
        na_p, k_ctx, v_ctx = _ctx_attention(proj, n_p, t_p)
        na_s = _na_attention(proj, mp, n_s, t_s, cache_k, cache_v, l, _na_bias_table(na_rpb[l], t_s // GRID_W))
        na_o = jnp.concatenate([na_p, na_s], axis=0)
        xr = _token_shift_mix(proj[:, PCOL_RW:PCOL_RW + RW_SHIFT_WIDTH], rw_mu[l], seqs)
        lw = dict(rw_w0=rw_w0[l], rw_w2=rw_w2[l], rw_a0=rw_a0[l], rw_a2=rw_a2[l], rw_g2=rw_g2[l], rw_k_k=rw_k_k[l],
                  rw_k_a=rw_k_a[l], rw_r_k=rw_r_k[l], rw_ln_g=rw_ln_g[l], rw_ln_b=rw_ln_b[l])
        rw_groups = [(0, n_p, t_p, jnp.zeros((2, n_p, RW_HEADS, RW_HEAD_DIM, RW_HEAD_DIM), F32)),
                     (mp, n_s, t_s, jnp.stack([state_rwkv_fwd[:, l], state_rwkv_bwd[:, l]]))]
        rw_o, rw_states = _rwkv_mixer(xr, rw_groups, lw)
        hg_groups = [(0, n_p, t_p, jnp.zeros((2, n_p, HG_HEADS, HG_KEY_DIM, HG_VAL_DIM), F32)),
                     (mp, n_s, t_s, jnp.stack([state_hgrn_fwd[:, l], state_hgrn_bwd[:, l]]))]
        hg_o, hg_states = _hgrn_mixer(proj, hg_groups, hg_lb[l], hg_norm_g[l])
        x, h2, idx, gate = _out_proj(na_o, rw_o, hg_o, x, w_out[l].astype(BF16), mod, norm2_g[l], router_w[l],
                                     router_b[l], mod_row)
        x = _moe(h2, idx, gate, x, mod, mod_row, exp_w_gu, exp_b_gu, exp_w_down, exp_b_down, l)
        ctx_out.append((k_ctx, v_ctx, rw_states[0][0], rw_states[0][1], hg_states[0][0], hg_states[0][1]))
    y = _final_norm(x, final_g)
    outs = [y[:mp].reshape(n_p, t_p, d), y[mp:].reshape(n_s, t_s, d)]
    outs += [jnp.stack([t[i] for t in ctx_out], axis=1) for i in range(6)]
    return tuple(outs)
```

```python
import functools

import numpy as np
import jax
import jax.numpy as jnp
from jax import lax
from jax.experimental import pallas as pl
from jax.experimental.pallas import tpu as pltpu

F32 = jnp.float32
BF16 = jnp.bfloat16

D_MODEL = 2048
GRID_W = 64
NA_HEADS = 16
NA_HEAD_DIM = 64
NA_WIDTH = NA_HEADS * NA_HEAD_DIM
NA_KH = 8
NA_KW = 16
RW_HEADS = 8
RW_HEAD_DIM = 64
RW_WIDTH = RW_HEADS * RW_HEAD_DIM
RW_DECAY_LORA = 64
RW_ICLR_LORA = 64
RW_GATE_LORA = 128
RW_SHIFT_WIDTH = 3 * RW_WIDTH + 2 * RW_DECAY_LORA + 2 * RW_ICLR_LORA + RW_GATE_LORA
RW_GN_EPS = 64e-5
HG_HEADS = 4
HG_KEY_DIM = 128
HG_VAL_DIM = 128
HG_WIDTH = HG_HEADS * HG_VAL_DIM
HG_MIN_F = 1e-6
IN_WIDTH = 3 * NA_WIDTH + RW_SHIFT_WIDTH + 5 * HG_WIDTH
N_EXPERTS = 32
TOP_K = 4
D_EXPERT = 2048
SWIGLU_LIMIT = 7.0
SWIGLU_ALPHA = 1.702
RMS_EPS = 1e-6
NEG_INF = -1e30

LANES = 128
PCOL_Q = 0
PCOL_K = NA_WIDTH
PCOL_V = 2 * NA_WIDTH
PCOL_RW = 3 * NA_WIDTH
RW_PAD = -RW_SHIFT_WIDTH % HG_WIDTH
PCOL_HQ = PCOL_RW + RW_SHIFT_WIDTH + RW_PAD
PCOL_HFF = PCOL_HQ + HG_WIDTH
PCOL_HFB = PCOL_HFF + HG_WIDTH
PCOL_HI = PCOL_HFB + HG_WIDTH
PCOL_HG = PCOL_HI + HG_WIDTH
IN_WIDTH_PAD = PCOL_HG + HG_WIDTH
SCAN_CHUNK = 64
TOKEN_TILE = 256
VMEM_LIMIT = 56 * 1024 * 1024


def _cparams(n_axes, vmem=VMEM_LIMIT):
    return pltpu.CompilerParams(dimension_semantics=("arbitrary",) * n_axes, vmem_limit_bytes=vmem)


def _dot(a, b):
    return jnp.dot(a.astype(BF16), b.astype(BF16), preferred_element_type=F32)


def _dot_nt(a, b):
    return lax.dot_general(a.astype(BF16), b.astype(BF16), (((1,), (1,)), ((), ())), preferred_element_type=F32)


def _dot_tn(a, b):
    return lax.dot_general(a.astype(BF16), b.astype(BF16), (((0,), (0,)), ((), ())), preferred_element_type=F32)


def _split3(x):
    hi = x.astype(BF16)
    r1 = x - hi.astype(F32)
    mid = r1.astype(BF16)
    lo = (r1 - mid.astype(F32)).astype(BF16)
    return hi, mid, lo


def _dot01_left(m01, x):
    hi, mid, lo = _split3(x)
    d = lambda p: jnp.dot(m01, p, preferred_element_type=F32)
    return d(hi) + (d(mid) + d(lo))


def _dot01_right(x, m01):
    hi, mid, lo = _split3(x)
    d = lambda p: jnp.dot(p, m01, preferred_element_type=F32)
    return d(hi) + (d(mid) + d(lo))


def _sigmoid(x):
    return 1.0 / (1.0 + jnp.exp(-x))


def _softplus(x):
    return jnp.maximum(x, 0.0) + jnp.log(1.0 + jnp.exp(-jnp.abs(x)))


def _head_sum_matrix(width, head_dim):
    i = np.arange(width)
    return jnp.asarray((i[:, None] // head_dim) == (i[None, :] // head_dim), dtype=BF16)


def _order_masks(n, reverse):
    row = lax.broadcasted_iota(jnp.int32, (n, n), 0)
    col = lax.broadcasted_iota(jnp.int32, (n, n), 1)
    diff = col - row if reverse else row - col
    return diff >= 0, diff > 0


def _rwkv_prep_kernel(x_ref, w0_ref, w2_ref, a0_ref, a2_ref, g2_ref, kkw_ref, kaw_ref, hsum_ref,
                      kkn_ref, g_ref, lw_ref, kd_ref, av_ref):
    x = x_ref[...]
    k = x[:, RW_WIDTH:2 * RW_WIDTH]
    o = 3 * RW_WIDTH
    wd = (x[:, o:o + 64], x[:, o + 64:o + 128])
    ad = (x[:, o + 128:o + 192], x[:, o + 192:o + 256])
    gd = x[:, o + 256:o + 384]
    g_ref[...] = _dot(_sigmoid(gd), g2_ref[...])
    kk = k * kkw_ref[...]
    ssq = _dot01_right(kk * kk, hsum_ref[...])
    kkn_ref[...] = kk / jnp.maximum(jnp.sqrt(ssq), 1e-12)
    for d in range(2):
        wl = -_softplus(-(w0_ref[d] + _dot(jnp.tanh(wd[d]), w2_ref[d]))) - 0.5
        lw_ref[d] = -jnp.exp(wl)
        a = _sigmoid(a0_ref[d] + _dot(ad[d], a2_ref[d]))
        kd_ref[d] = k * (1.0 + (a - 1.0) * kaw_ref[...])
        av_ref[d] = a


def _rwkv_scan_kernel(*refs, n_chunks):
    ins = (refs[0:6], refs[6:12])
    s0_ref, yf_ref, yb_ref, sout_ref, s_scr = refs[12:]
    c = pl.program_id(1)
    n = SCAN_CHUNK

    @pl.when(c == 0)
    def _():
        s_scr[...] = s0_ref[...]

    masks = [_order_masks(n, False), _order_masks(n, True)]
    incl01 = [m[0].astype(BF16) for m in masks]
    ch = [(d, h) for d in range(2) for h in range(RW_HEADS)]
    cs = range(len(ch))
    m_incl = [masks[d][0] for d, _ in ch]
    m_strict = [masks[d][1] for d, _ in ch]
    col = lambda j: [ins[d][j][:, h * RW_HEAD_DIM:(h + 1) * RW_HEAD_DIM] for d, h in ch]
    r, v, kkn, lw, kd, av = (col(j) for j in range(6))
    cum = [_dot01_left(incl01[ch[i][0]], lw[i]) for i in cs]
    p_tot = [jnp.exp(jnp.sum(lw[i], axis=0, keepdims=True)) for i in cs]
    p_inv = [jnp.exp(-cum[i]) for i in cs]
    a_t = [-kkn[i] * jnp.exp(cum[i] - lw[i]) for i in cs]
    b_t = [kkn[i] * av[i] * p_inv[i] for i in cs]
    k_t = [kd[i] * p_inv[i] for i in cs]
    r_t = [r[i] * jnp.exp(cum[i]) for i in cs]
    a_ab = [jnp.where(m_strict[i], _dot_nt(a_t[i], b_t[i]), 0.0) for i in cs]
    a_ak = [jnp.where(m_strict[i], _dot_nt(a_t[i], k_t[i]), 0.0) for i in cs]
    a_rb = [jnp.where(m_incl[i], _dot_nt(r_t[i], b_t[i]), 0.0) for i in cs]
    a_rk = [jnp.where(m_incl[i], _dot_nt(r_t[i], k_t[i]), 0.0) for i in cs]
    t_m = list(a_ab)
    n_pow = list(a_ab)
    for _ in range(int(np.log2(n)) - 1):
        n_pow = [_dot(n_pow[i], n_pow[i]) for i in cs]
        t_m = [t_m[i] + n_pow[i] + _dot(t_m[i], n_pow[i]) for i in cs]
    x1 = [_dot(a_ak[i], v[i]) for i in cs]
    w1 = [x1[i] + _dot(t_m[i], x1[i]) for i in cs]
    w2 = [a_t[i] + _dot(t_m[i], a_t[i]) for i in cs]
    y0 = [_dot(a_rk[i], v[i]) + _dot(a_rb[i], w1[i]) for i in cs]
    q = [r_t[i] + _dot(a_rb[i], w2[i]) for i in cs]
    g_m = [_dot_tn(w2[i], b_t[i]) for i in cs]
    h_m = [_dot_tn(w1[i], b_t[i]) + _dot_tn(v[i], k_t[i]) for i in cs]
    s0 = [s_scr[d, h] for d, h in ch]
    ys = [y0[i] + _dot_nt(q[i], s0[i]) for i in cs]
    s_new = [(s0[i] + _dot(s0[i], g_m[i]) + h_m[i]) * p_tot[i] for i in cs]
    for i, (d, h) in enumerate(ch):
        s_scr[d, h] = s_new[i]
    yf_ref[...] = jnp.concatenate(ys[:RW_HEADS], axis=-1)
    yb_ref[...] = jnp.concatenate(ys[RW_HEADS:], axis=-1)

    @pl.when(c == n_chunks - 1)
    def _():
        sout_ref[...] = s_scr[...]


def _rwkv_post_kernel(yf_ref, yb_ref, x_ref, kd_ref, g_ref, rk_ref, lng_ref, lnb_ref, hsum_ref, o_ref):
    x = x_ref[...]
    r = x[:, 0:RW_WIDTH]
    v = x[:, 2 * RW_WIDTH:3 * RW_WIDTH]
    y = yf_ref[...] + yb_ref[...]
    hsum = hsum_ref[...]
    inv_n = 1.0 / RW_HEAD_DIM
    mu = _dot01_right(y, hsum) * inv_n
    yc = y - mu
    var = _dot01_right(yc * yc, hsum) * inv_n
    yn = yc * lax.rsqrt(var + RW_GN_EPS) * lng_ref[...] + lnb_ref[...]
    bonus = _dot01_right(r * (0.5 * (kd_ref[0] + kd_ref[1])) * rk_ref[...], hsum) * v
    o_ref[...] = (yn + bonus) * g_ref[...]


def _rwkv_mixer(xs, groups, lw):
    m = xs.shape[0]
    tm = TOKEN_TILE
    hsum = _head_sum_matrix(RW_WIDTH, RW_HEAD_DIM)
    row = lambda a: a.reshape(1, -1)
    full = lambda shape: pl.BlockSpec(shape, lambda i: (0,) * len(shape))
    tok = lambda w: pl.BlockSpec((tm, w), lambda i: (i, 0))
    tok2 = pl.BlockSpec((2, tm, RW_WIDTH), lambda i: (0, i, 0))
    d2 = jax.ShapeDtypeStruct((2, m, RW_WIDTH), F32)
    d1 = jax.ShapeDtypeStruct((m, RW_WIDTH), F32)
    kkn, g, lwd, kd, av = pl.pallas_call(
        _rwkv_prep_kernel,
        grid=(m // tm,),
        in_specs=[tok(RW_SHIFT_WIDTH), full((2, 1, RW_WIDTH)), full((2, RW_DECAY_LORA, RW_WIDTH)),
                  full((2, 1, RW_WIDTH)), full((2, RW_ICLR_LORA, RW_WIDTH)), full((RW_GATE_LORA, RW_WIDTH)),
                  full((1, RW_WIDTH)), full((1, RW_WIDTH)), full((RW_WIDTH, RW_WIDTH))],
        out_specs=[tok(RW_WIDTH), tok(RW_WIDTH), tok2, tok2, tok2],
        out_shape=[d1, d1, d2, d2, d2],
        compiler_params=_cparams(1),
        name="rwkv_prep",
    )(xs, lw['rw_w0'].reshape(2, 1, RW_WIDTH), lw['rw_w2'], lw['rw_a0'].reshape(2, 1, RW_WIDTH), lw['rw_a2'],
      lw['rw_g2'], row(lw['rw_k_k']), row(lw['rw_k_a']), hsum)

    n = SCAN_CHUNK
    yfs, ybs, states = [], [], []
    for row_off, n_seq, seq_len, s0 in groups:
        nc = seq_len // n
        base = row_off // n
        chunk = lambda b, c, d, nc=nc: b * nc + (nc - 1 - c if d else c)
        x_spec = lambda d, col, base=base, chunk=chunk: pl.BlockSpec(
            (n, RW_WIDTH), lambda b, c: (base + chunk(b, c, d), col))
        d_spec = lambda d, base=base, chunk=chunk: pl.BlockSpec(
            (None, n, RW_WIDTH), lambda b, c: (d, base + chunk(b, c, d), 0))
        y_spec = lambda d, chunk=chunk: pl.BlockSpec((n, RW_WIDTH), lambda b, c: (chunk(b, c, d), 0))
        s_spec = pl.BlockSpec((2, None, RW_HEADS, RW_HEAD_DIM, RW_HEAD_DIM), lambda b, c: (0, b, 0, 0, 0))
        per_dir = lambda d: [x_spec(d, 0), x_spec(d, 2), x_spec(d, 0), d_spec(d), d_spec(d), d_spec(d)]
        y_shape = jax.ShapeDtypeStruct((n_seq * seq_len, RW_WIDTH), F32)
        yf, yb, s_out = pl.pallas_call(
            functools.partial(_rwkv_scan_kernel, n_chunks=nc),
            grid=(n_seq, nc),
            in_specs=per_dir(0) + per_dir(1) + [s_spec],
            out_specs=[y_spec(0), y_spec(1), s_spec],
            out_shape=[y_shape, y_shape,
                       jax.ShapeDtypeStruct((2, n_seq, RW_HEADS, RW_HEAD_DIM, RW_HEAD_DIM), F32)],
            scratch_shapes=[pltpu.VMEM((2, RW_HEADS, RW_HEAD_DIM, RW_HEAD_DIM), F32)],
            compiler_params=_cparams(2),
            name="rwkv_scan",
        )(*([xs, xs, kkn, lwd, kd, av] * 2), s0)
        yfs.append(yf)
        ybs.append(yb)
        states.append(s_out)
    yf = jnp.concatenate(yfs, axis=0)
    yb = jnp.concatenate(ybs, axis=0)

    out = pl.pallas_call(
        _rwkv_post_kernel,
        grid=(m // tm,),
        in_specs=[tok(RW_WIDTH), tok(RW_WIDTH), tok(RW_SHIFT_WIDTH), tok2, tok(RW_WIDTH), full((1, RW_WIDTH)),
                  full((1, RW_WIDTH)), full((1, RW_WIDTH)), full((RW_WIDTH, RW_WIDTH))],
        out_specs=tok(RW_WIDTH),
        out_shape=d1,
        compiler_params=_cparams(1),
        name="rwkv_post",
    )(yf, yb, xs, kd, g, row(lw['rw_r_k']), row(lw['rw_ln_g']), row(lw['rw_ln_b']), hsum)
    return out, states


HG_LEVELS = (32, 16, 8, 4, 2, 1)


def _hgrn_consts():
    n = SCAN_CHUNK
    cums, masks = [], []
    for rev in (False, True):
        p = np.arange(n)[::-1] if rev else np.arange(n)
        pt, pj = p[:, None], p[None, :]
        rows = [pj <= pt, pj > pt]
        rows += [((pt // m) % 2 == 1) & (pj // m == pt // m) & (pj <= pt) for m in HG_LEVELS]
        rows += [((pt // m) % 2 == 0) & (pj // m == pt // m) & (pj > pt) for m in HG_LEVELS]
        cums.append(np.concatenate(rows, 0))
        mk = [pt == pj]
        mk += [(pt // (2 * m) == pj // (2 * m)) & ((pt // m) % 2 == 1) & ((pj // m) % 2 == 0) for m in HG_LEVELS]
        masks.append(np.stack(mk))
    return jnp.asarray(np.stack(cums), BF16), jnp.asarray(np.stack(masks), F32)


def _hgrn_scan_kernel(qf_ref, zf_ref, vf_ref, qb_ref, zb_ref, vb_ref, lb_ref, cm_ref, mask_ref, s0_ref,
                      of_ref, ob_ref, sout_ref, s_scr, *, n_chunks):
    c = pl.program_id(1)
    n = SCAN_CHUNK
    nl = len(HG_LEVELS)
    ins = ((qf_ref, zf_ref, vf_ref), (qb_ref, zb_ref, vb_ref))

    @pl.when(c == 0)
    def _():
        s_scr[...] = s0_ref[...]

    ones = jnp.ones((n, HG_VAL_DIM), BF16)
    tn01 = lambda p: lax.dot_general(p, ones, (((0,), (0,)), ((), ())), preferred_element_type=F32)
    ch = [(d, h) for d in range(2) for h in range(HG_HEADS)]
    cs = range(len(ch))
    col = lambda j: [ins[d][j][:, h * HG_KEY_DIM:(h + 1) * HG_KEY_DIM] for d, h in ch]
    hq, z, v = col(0), col(1), col(2)
    lb = [lb_ref[d, :, h * HG_KEY_DIM:(h + 1) * HG_KEY_DIM] for d, h in ch]
    q = [hq[i] * _sigmoid(hq[i]) for i in cs]
    lf = [jnp.log(jnp.maximum(lb[i] + (1.0 - lb[i]) * _sigmoid(z[i]), HG_MIN_F)) for i in cs]
    k = [(1.0 - lb[i]) * _sigmoid(-z[i]) for i in cs]
    parts = [_split3(lf[i]) for i in cs]
    ex = [jnp.exp(sum(jnp.dot(cm_ref[ch[i][0]], p, preferred_element_type=F32) for p in parts[i][::-1])) for i in cs]
    p_col = [jnp.exp(sum(tn01(p) for p in parts[i][::-1])) for i in cs]
    att = [mask_ref[ch[i][0], 0] * _dot_nt(q[i], k[i]) for i in cs]
    for lv in range(nl):
        lo_q, lo_k = (2 + lv) * n, (2 + nl + lv) * n
        lvl = [_dot_nt(q[i] * ex[i][lo_q:lo_q + n], k[i] * ex[i][lo_k:lo_k + n]) for i in cs]
        att = [att[i] + mask_ref[ch[i][0], lv + 1] * lvl[i] for i in cs]
    s0 = [s_scr[d, h] for d, h in ch]
    outs = [_dot(q[i] * ex[i][0:n], s0[i]) + _dot(att[i], v[i]) for i in cs]
    s_new = [s0[i] * p_col[i] + _dot_tn(k[i] * ex[i][n:2 * n], v[i]) for i in cs]
    for i, (d, h) in enumerate(ch):
        s_scr[d, h] = s_new[i]
    of_ref[...] = jnp.concatenate(outs[:HG_HEADS], axis=-1)
    ob_ref[...] = jnp.concatenate(outs[HG_HEADS:], axis=-1)

    @pl.when(c == n_chunks - 1)
    def _():
        sout_ref[...] = s_scr[...]


def _hgrn_post_kernel(of_ref, ob_ref, g_ref, gain_ref, hsum_ref, out_ref):
    o = of_ref[...] + ob_ref[...]
    ms = _dot01_right(o * o, hsum_ref[...]) * (1.0 / HG_VAL_DIM)
    on = o * lax.rsqrt(ms + RMS_EPS) * gain_ref[...]
    g = g_ref[...]
    out_ref[...] = on * (g * _sigmoid(g))


def _hgrn_mixer(proj, groups, lb, gain):
    m = proj.shape[0]
    n = SCAN_CHUNK
    cm, masks = _hgrn_consts()
    cq, cff, ci, cg = (col // HG_WIDTH for col in (PCOL_HQ, PCOL_HFF, PCOL_HI, PCOL_HG))
    ofs, obs, states = [], [], []
    for row_off, n_seq, seq_len, s0 in groups:
        nc = seq_len // n
        base = row_off // n
        chunk = lambda b, c, d, nc=nc: b * nc + (nc - 1 - c if d else c)
        x_spec = lambda d, col, base=base, chunk=chunk: pl.BlockSpec(
            (n, HG_WIDTH), lambda b, c: (base + chunk(b, c, d), col))
        o_spec = lambda d, chunk=chunk: pl.BlockSpec((n, HG_WIDTH), lambda b, c: (chunk(b, c, d), 0))
        s_spec = pl.BlockSpec((2, None, HG_HEADS, HG_KEY_DIM, HG_VAL_DIM), lambda b, c: (0, b, 0, 0, 0))
        whole = lambda a: pl.BlockSpec(a.shape, lambda b, c: (0,) * a.ndim)
        lb3 = lb.reshape(2, 1, HG_WIDTH)
        o_shape = jax.ShapeDtypeStruct((n_seq * seq_len, HG_WIDTH), F32)
        of, ob, s_out = pl.pallas_call(
            functools.partial(_hgrn_scan_kernel, n_chunks=nc),
            grid=(n_seq, nc),
            in_specs=[x_spec(0, cq), x_spec(0, cff), x_spec(0, ci), x_spec(1, cq), x_spec(1, cff + 1), x_spec(1, ci),
                      whole(lb3), whole(cm), whole(masks), s_spec],
            out_specs=[o_spec(0), o_spec(1), s_spec],
            out_shape=[o_shape, o_shape,
                       jax.ShapeDtypeStruct((2, n_seq, HG_HEADS, HG_KEY_DIM, HG_VAL_DIM), F32)],
            scratch_shapes=[pltpu.VMEM((2, HG_HEADS, HG_KEY_DIM, HG_VAL_DIM), F32)],
            compiler_params=_cparams(2),
            name="hgrn_scan",
        )(proj, proj, proj, proj, proj, proj, lb3, cm, masks, s0)
        ofs.append(of)
        obs.append(ob)
        states.append(s_out)
    of = jnp.concatenate(ofs, axis=0)
    ob = jnp.concatenate(obs, axis=0)

    tm = TOKEN_TILE
    tok = pl.BlockSpec((tm, HG_WIDTH), lambda i: (i, 0))
    out = pl.pallas_call(
        _hgrn_post_kernel,
        grid=(m // tm,),
        in_specs=[tok, tok,
                  pl.BlockSpec((tm, HG_WIDTH), lambda i: (i, cg)),
                  pl.BlockSpec((1, HG_WIDTH), lambda i: (0, 0)),
                  pl.BlockSpec((HG_WIDTH, HG_WIDTH), lambda i: (0, 0))],
        out_specs=pl.BlockSpec((tm, HG_WIDTH), lambda i: (i, 0)),
        out_shape=jax.ShapeDtypeStruct((m, HG_WIDTH), F32),
        compiler_params=_cparams(1),
        name="hgrn_post",
    )(of, ob, proj, gain.reshape(1, HG_WIDTH), _head_sum_matrix(HG_WIDTH, HG_VAL_DIM))
    return out, states


NA_HEADS_PER_STEP = LANES // NA_HEAD_DIM
NA_ROWS_PER_ITER = 4


def _ctx_attn_kernel(q_ref, k_ref, v_ref, o_ref, kc_ref, vc_ref):
    scale = NA_HEAD_DIM ** -0.5
    hs = range(NA_HEADS_PER_STEP)
    sl = [slice(h * NA_HEAD_DIM, (h + 1) * NA_HEAD_DIM) for h in hs]
    k = [k_ref[:, s] for s in sl]
    v = [v_ref[:, s] for s in sl]
    for h in hs:
        kc_ref[h] = k[h]
        vc_ref[h] = v[h]
    s = [_dot_nt(q_ref[:, sl[h]], k[h]) * scale for h in hs]
    p = [jnp.exp(s[h] - jnp.max(s[h], axis=-1, keepdims=True)) for h in hs]
    o = [_dot(p[h], v[h]) / jnp.sum(p[h], axis=-1, keepdims=True) for h in hs]
    o_ref[...] = jnp.concatenate(o, axis=-1)


def _ctx_attention(proj, n_seq, seq_len):
    nb = NA_WIDTH // LANES
    blk = lambda off: pl.BlockSpec((seq_len, LANES), lambda b, j: (b, off + j))
    c_spec = pl.BlockSpec((None, NA_HEADS_PER_STEP, seq_len, NA_HEAD_DIM), lambda b, j: (b, j, 0, 0))
    c_shape = jax.ShapeDtypeStruct((n_seq, NA_HEADS, seq_len, NA_HEAD_DIM), F32)
    return pl.pallas_call(
        _ctx_attn_kernel,
        grid=(n_seq, nb),
        in_specs=[blk(PCOL_Q // LANES), blk(PCOL_K // LANES), blk(PCOL_V // LANES)],
        out_specs=[pl.BlockSpec((seq_len, LANES), lambda b, j: (b, j)), c_spec, c_spec],
        out_shape=[jax.ShapeDtypeStruct((n_seq * seq_len, NA_WIDTH), F32), c_shape, c_shape],
        compiler_params=_cparams(2),
        name="ctx_attention",
    )(proj, proj, proj)


def _na_bias_table(rpb, rows):
    assert rows >= NA_KH
    o = np.arange(NA_KH)[:, None]
    i = np.arange(NA_KH)[None, :]
    dr = i - o + (NA_KH - 1)
    cq = np.arange(GRID_W)
    ck = np.arange(GRID_W)
    col_start = np.clip(cq - NA_KW // 2, 0, GRID_W - NA_KW)
    col_mask = (ck[None, :] >= col_start[:, None]) & (ck[None, :] < col_start[:, None] + NA_KW)
    dc = np.clip(ck[None, :] - cq[:, None], -(NA_KW - 1), NA_KW - 1) + (NA_KW - 1)
    onehot = jnp.asarray(dc.reshape(-1)[:, None] == np.arange(2 * NA_KW - 1)[None, :], F32)
    full = jnp.einsum('hrj,pj->hrp', rpb.astype(F32), onehot, precision=lax.Precision.HIGHEST)
    full = jnp.where(col_mask.reshape(-1)[None, None, :], full, NEG_INF).reshape(rpb.shape[0], -1, GRID_W, GRID_W)
    lo = lambda off: int(dr[off, 0])
    tab = jnp.stack([full[:, lo(off):lo(off) + NA_KH] for off in range(NA_KH)])
    return tab.transpose(0, 1, 3, 2, 4).reshape(NA_KH, rpb.shape[0], GRID_W, NA_KH * GRID_W)


def _na_kernel(q_ref, k_ref, v_ref, ck_ref, cv_ref, bias_ref, o_ref, *, rows):
    scale = NA_HEAD_DIM ** -0.5
    n_win = NA_KH * GRID_W

    def row_group(g, carry):
        ch = []
        for i in range(NA_ROWS_PER_ITER):
            r = g * NA_ROWS_PER_ITER + i
            r0 = jnp.clip(r - NA_KH // 2, 0, rows - NA_KH)
            q_at = pl.multiple_of(r * GRID_W, GRID_W)
            k_at = pl.multiple_of(r0 * GRID_W, GRID_W)
            for h in range(NA_HEADS_PER_STEP):
                ch.append((q_at, k_at, r - r0, h, slice(h * NA_HEAD_DIM, (h + 1) * NA_HEAD_DIM)))
        q = [q_ref[pl.ds(q_at, GRID_W), sl] for q_at, _, _, _, sl in ch]
        s_w = [_dot_nt(q[i], k_ref[pl.ds(k_at, n_win), sl]) * scale + bias_ref[off, h]
               for i, (_, k_at, off, h, sl) in enumerate(ch)]
        s_c = [_dot_nt(q[i], ck_ref[h]) * scale for i, (_, _, _, h, _) in enumerate(ch)]
        m = [jnp.maximum(jnp.max(a, axis=-1, keepdims=True), jnp.max(b, axis=-1, keepdims=True))
             for a, b in zip(s_w, s_c)]
        p_w = [jnp.exp(a - mm) for a, mm in zip(s_w, m)]
        p_c = [jnp.exp(b - mm) for b, mm in zip(s_c, m)]
        den = [jnp.sum(a, axis=-1, keepdims=True) + jnp.sum(b, axis=-1, keepdims=True) for a, b in zip(p_w, p_c)]
        o = [(_dot(p_w[i], v_ref[pl.ds(k_at, n_win), sl]) + _dot(p_c[i], cv_ref[h])) / den[i]
             for i, (_, k_at, _, h, sl) in enumerate(ch)]
        for i, (q_at, _, _, _, sl) in enumerate(ch):
            o_ref[pl.ds(q_at, GRID_W), sl] = o[i]
        return carry

    lax.fori_loop(0, rows // NA_ROWS_PER_ITER, row_group, 0)


def _na_attention(proj, row_off, n_seq, seq_len, cache_k, cache_v, layer, bias):
    assert row_off % seq_len == 0
    nb = NA_WIDTH // LANES
    rows = seq_len // GRID_W
    assert rows % NA_ROWS_PER_ITER == 0
    rb = row_off // seq_len
    past = cache_k.shape[3]
    blk = lambda off: pl.BlockSpec((seq_len, LANES), lambda b, j: (rb + b, off + j))
    c_spec = pl.BlockSpec((None, None, NA_HEADS_PER_STEP, past, NA_HEAD_DIM), lambda b, j: (b, layer, j, 0, 0))
    return pl.pallas_call(
        functools.partial(_na_kernel, rows=rows),
        grid=(n_seq, nb),
        in_specs=[blk(PCOL_Q // LANES), blk(PCOL_K // LANES), blk(PCOL_V // LANES), c_spec, c_spec,
                  pl.BlockSpec((NA_KH, NA_HEADS_PER_STEP, GRID_W, NA_KH * GRID_W), lambda b, j: (0, j, 0, 0))],
        out_specs=pl.BlockSpec((seq_len, LANES), lambda b, j: (b, j)),
        out_shape=jax.ShapeDtypeStruct((n_seq * seq_len, NA_WIDTH), F32),
        compiler_params=_cparams(2),
        name="na_attention",
    )(proj, proj, proj, cache_k, cache_v, bias)


ADA_TN = 1024
IN_TM = 1024
IN_TN = 1536
OUT_TM = 512


def _mod_row_fn(tm, n_ctx_rows, latent_len):
    def f(i):
        start = i * tm
        return jnp.where(start < n_ctx_rows, 0, 1 + (start - n_ctx_rows) // latent_len)
    return f


def _ada_kernel(c_ref, w_ref, b_ref, o_ref):
    x = c_ref[...]
    o_ref[...] = _dot(x * _sigmoid(x), w_ref[...]) + b_ref[...]


def _ada(cond, w_ada, b_ada, layer):
    n_rows, d = cond.shape
    n = w_ada.shape[2]
    return pl.pallas_call(
        _ada_kernel,
        grid=(n // ADA_TN,),
        in_specs=[pl.BlockSpec((n_rows, d), lambda j: (0, 0)),
                  pl.BlockSpec((None, d, ADA_TN), lambda j: (layer, 0, j)),
                  pl.BlockSpec((None, 1, ADA_TN), lambda j: (layer, 0, j))],
        out_specs=pl.BlockSpec((n_rows, ADA_TN), lambda j: (0, j)),
        out_shape=jax.ShapeDtypeStruct((n_rows, n), F32),
        compiler_params=_cparams(1),
        name="ada_mod",
    )(cond, w_ada, b_ada.reshape(b_ada.shape[0], 1, n))


def _rms_mod(x, g, sc, sh):
    y = x * lax.rsqrt(jnp.mean(x * x, axis=-1, keepdims=True) + RMS_EPS) * g
    return y * (1.0 + sc) + sh


def _in_proj_kernel(x_ref, g_ref, sh_ref, sc_ref, w_ref, o_ref, h_scr):
    @pl.when(pl.program_id(1) == 0)
    def _():
        h_scr[...] = _rms_mod(x_ref[...], g_ref[...], sc_ref[...], sh_ref[...]).astype(BF16)

    o_ref[...] = jnp.dot(h_scr[...], w_ref[...], preferred_element_type=F32)


def _in_proj(x, norm_g, mod, w, mod_row):
    m, d = x.shape
    n = w.shape[1]
    row = mod_row(IN_TM)
    return pl.pallas_call(
        _in_proj_kernel,
        grid=(m // IN_TM, n // IN_TN),
        in_specs=[pl.BlockSpec((IN_TM, d), lambda i, j: (i, 0)),
                  pl.BlockSpec((1, d), lambda i, j: (0, 0)),
                  pl.BlockSpec((None, 1, d), lambda i, j: (row(i), 0, 0)),
                  pl.BlockSpec((None, 1, d), lambda i, j: (row(i), 0, 1)),
                  pl.BlockSpec((d, IN_TN), lambda i, j: (0, j))],
        out_specs=pl.BlockSpec((IN_TM, IN_TN), lambda i, j: (i, j)),
        out_shape=jax.ShapeDtypeStruct((m, n), F32),
        scratch_shapes=[pltpu.VMEM((IN_TM, d), BF16)],
        compiler_params=_cparams(2),
        name="in_proj",
    )(x, norm_g.reshape(1, d), mod, mod, w)


def _out_proj_kernel(na_ref, rw_ref, hg_ref, x_ref, w_ref, g1_ref, n2_ref, sh2_ref, sc2_ref, rtw_ref, rtb_ref,
                     xo_ref, h2_ref, idx_ref, gate_ref):
    o1, o2 = NA_WIDTH, NA_WIDTH + RW_WIDTH
    mix = _dot(na_ref[...], w_ref[0:o1]) + _dot(rw_ref[...], w_ref[o1:o2]) + _dot(hg_ref[...], w_ref[o2:])
    x = x_ref[...] + g1_ref[...] * mix
    xo_ref[...] = x
    h = _rms_mod(x, n2_ref[...], sc2_ref[...], sh2_ref[...])
    h_hi = h.astype(BF16)
    h2_ref[...] = h_hi
    h_mid = (h - h_hi.astype(F32)).astype(BF16)
    d = lambda a, b: jnp.dot(a, b, preferred_element_type=F32)
    logits = d(h_hi, rtw_ref[0]) + (d(h_hi, rtw_ref[1]) + d(h_mid, rtw_ref[0])) + rtb_ref[...]
    lane = lax.broadcasted_iota(jnp.int32, logits.shape, 1).astype(F32)
    idx = jnp.zeros_like(logits)
    vals = []
    for k in range(TOP_K):
        mx = jnp.max(logits, axis=-1, keepdims=True)
        sel = jnp.min(jnp.where(logits == mx, lane, float(LANES)), axis=-1, keepdims=True)
        idx = jnp.where(lane == k, sel, idx)
        vals.append(mx)
        logits = jnp.where(lane == sel, 3.0 * NEG_INF, logits)
    es = [jnp.exp(v - vals[0]) for v in vals]
    den = es[0] + es[1] + es[2] + es[3]
    gate = jnp.zeros_like(idx)
    for k in range(TOP_K):
        gate = jnp.where(lane == k, es[k] / den, gate)
    idx_ref[...] = idx.astype(jnp.int32)
    gate_ref[...] = gate


def _out_proj(na_o, rw_o, hg_o, x, w_out, mod, norm2_g, router_w, router_b, mod_row):
    m, d = x.shape
    tm = OUT_TM
    row = mod_row(tm)
    n_exp = router_w.shape[1]
    rtw = jnp.pad(router_w.astype(F32), ((0, 0), (0, LANES - n_exp)))
    rtw_hi = rtw.astype(BF16)
    rtw = jnp.stack([rtw_hi, (rtw - rtw_hi.astype(F32)).astype(BF16)])
    rtb = jnp.pad(router_b, (0, LANES - n_exp), constant_values=NEG_INF).reshape(1, LANES)
    tok = lambda w: pl.BlockSpec((tm, w), lambda i: (i, 0))
    mod_blk = lambda part: pl.BlockSpec((None, 1, d), lambda i: (row(i), 0, part))
    return pl.pallas_call(
        _out_proj_kernel,
        grid=(m // tm,),
        in_specs=[tok(NA_WIDTH), tok(RW_WIDTH), tok(HG_WIDTH), tok(d),
                  pl.BlockSpec((d, d), lambda i: (0, 0)),
                  mod_blk(2), pl.BlockSpec((1, d), lambda i: (0, 0)), mod_blk(3), mod_blk(4),
                  pl.BlockSpec((2, d, LANES), lambda i: (0, 0, 0)), pl.BlockSpec((1, LANES), lambda i: (0, 0))],
        out_specs=[tok(d), tok(d), tok(LANES), tok(LANES)],
        out_shape=[jax.ShapeDtypeStruct((m, d), F32), jax.ShapeDtypeStruct((m, d), BF16),
                   jax.ShapeDtypeStruct((m, LANES), jnp.int32), jax.ShapeDtypeStruct((m, LANES), F32)],
        compiler_params=_cparams(1),
        name="out_proj_router",
    )(na_o, rw_o, hg_o, x, w_out, mod, norm2_g.reshape(1, d), mod, mod, rtw, rtb)


def _final_norm_kernel(x_ref, g_ref, o_ref):
    x = x_ref[...]
    o_ref[...] = x * lax.rsqrt(jnp.mean(x * x, axis=-1, keepdims=True) + RMS_EPS) * g_ref[...]


def _final_norm(x, g):
    m, d = x.shape
    tm = TOKEN_TILE
    return pl.pallas_call(
        _final_norm_kernel,
        grid=(m // tm,),
        in_specs=[pl.BlockSpec((tm, d), lambda i: (i, 0)), pl.BlockSpec((1, d), lambda i: (0, 0))],
        out_specs=pl.BlockSpec((tm, d), lambda i: (i, 0)),
        out_shape=jax.ShapeDtypeStruct((m, d), F32),
        compiler_params=_cparams(1),
        name="final_norm",
    )(x, g.reshape(1, d))


MOE_TM = 512
GU_TN = 1024
DOWN_TN = 1024


def _moe_dispatch(top_e, n_exp):
    n_tok, top_k = top_e.shape
    n_assign = n_tok * top_k
    i32 = jnp.int32
    flat_e = top_e.reshape(-1).astype(i32)
    assign = jnp.arange(n_assign, dtype=i32)
    experts = jnp.arange(n_exp, dtype=i32)
    lookup = lambda table, e: jnp.sum(jnp.where(e[:, None] == experts[None, :], table[None, :], 0), axis=1)
    sorted_e, order = lax.sort((flat_e, assign), num_keys=1)
    counts = jnp.sum((flat_e[:, None] == experts[None, :]).astype(i32), axis=0)
    padded = (counts + MOE_TM - 1) // MOE_TM * MOE_TM
    pad_end = jnp.cumsum(padded)
    pad_start = pad_end - padded
    start = jnp.cumsum(counts) - counts
    shift = pad_start - start
    slot_sorted = assign + lookup(shift, sorted_e)
    _, slot_of_assign = lax.sort((order, slot_sorted), num_keys=1)
    n_blocks = -(-n_assign // MOE_TM) + n_exp
    blk = jnp.arange(n_blocks, dtype=i32)
    block_expert = jnp.sum((pad_end[None, :] <= (blk * MOE_TM)[:, None]).astype(i32), axis=1)
    block_expert = jnp.minimum(block_expert, n_exp - 1)
    n_active = (pad_end[-1] // MOE_TM).astype(i32)
    first = blk * MOE_TM - lookup(shift, block_expert)
    last = lookup(start + counts, block_expert)
    tok_sorted = jnp.concatenate([order // top_k, jnp.zeros((MOE_TM,), i32)])
    rows = jax.vmap(lambda s: lax.dynamic_slice(tok_sorted, (s,), (MOE_TM,)))(jnp.clip(first, 0, n_assign))
    lane = jnp.arange(MOE_TM, dtype=i32)[None, :]
    valid = (first[:, None] + lane < last[:, None]) & (blk[:, None] < n_active)
    slot_token = jnp.where(valid, rows, (blk[:, None] * MOE_TM + lane) % n_tok).reshape(-1)
    last_used = jnp.sum(jnp.where(blk == jnp.maximum(n_active - 1, 0), block_expert, 0))
    block_expert = jnp.where(blk < n_active, block_expert, last_used)
    return slot_token, slot_of_assign, block_expert, n_active.reshape(1)


def _expert_changed(be_ref, m):
    return (m == 0) | (be_ref[m] != be_ref[jnp.maximum(m - 1, 0)])


def _moe_gu_kernel(be_ref, na_ref, x_ref, wg_ref, wu_ref, bg_ref, bu_ref, o_ref, wg_s, wu_s):
    m = pl.program_id(1)

    @pl.when(_expert_changed(be_ref, m))
    def _():
        wg_s[...] = wg_ref[...].astype(BF16)
        wu_s[...] = wu_ref[...].astype(BF16)

    @pl.when(m < na_ref[0])
    def _():
        x = x_ref[...]
        gate = jnp.dot(x, wg_s[...], preferred_element_type=F32) + bg_ref[...]
        up = jnp.dot(x, wu_s[...], preferred_element_type=F32) + bu_ref[...]
        gate = jnp.minimum(gate, SWIGLU_LIMIT)
        up = jnp.clip(up, -SWIGLU_LIMIT, SWIGLU_LIMIT)
        o_ref[...] = ((up + 1.0) * gate * _sigmoid(SWIGLU_ALPHA * gate)).astype(BF16)

    @pl.when(m >= na_ref[0])
    def _():
        o_ref[...] = jnp.zeros_like(o_ref)


def _moe_down_kernel(be_ref, na_ref, a_ref, w_ref, b_ref, o_ref, w_s):
    m = pl.program_id(1)

    @pl.when(_expert_changed(be_ref, m))
    def _():
        w_s[...] = w_ref[...].astype(BF16)

    @pl.when(m < na_ref[0])
    def _():
        o_ref[...] = jnp.dot(a_ref[...], w_s[...], preferred_element_type=F32) + b_ref[...]

    @pl.when(m >= na_ref[0])
    def _():
        o_ref[...] = jnp.zeros_like(o_ref)


def _moe_experts(xb, block_expert, n_active, w_gu, b_gu, w_down, b_down, layer):
    n_slots, d = xb.shape
    n_blocks = n_slots // MOE_TM
    n_exp, _, two_f = w_gu.shape[1:]
    f = two_f // 2
    nj = f // GU_TN
    act = pl.pallas_call(
        _moe_gu_kernel,
        grid_spec=pltpu.PrefetchScalarGridSpec(
            num_scalar_prefetch=2,
            grid=(nj, n_blocks),
            in_specs=[pl.BlockSpec((MOE_TM, d), lambda j, m, be, na: (m, 0)),
                      pl.BlockSpec((None, None, d, GU_TN), lambda j, m, be, na: (layer, be[m], 0, j)),
                      pl.BlockSpec((None, None, d, GU_TN), lambda j, m, be, na: (layer, be[m], 0, nj + j)),
                      pl.BlockSpec((None, None, 1, GU_TN), lambda j, m, be, na: (layer, be[m], 0, j)),
                      pl.BlockSpec((None, None, 1, GU_TN), lambda j, m, be, na: (layer, be[m], 0, nj + j))],
            out_specs=pl.BlockSpec((MOE_TM, GU_TN), lambda j, m, be, na: (m, j)),
            scratch_shapes=[pltpu.VMEM((d, GU_TN), BF16), pltpu.VMEM((d, GU_TN), BF16)]),
        out_shape=jax.ShapeDtypeStruct((n_slots, f), BF16),
        compiler_params=_cparams(2),
        name="moe_gate_up",
    )(block_expert, n_active, xb, w_gu, w_gu, b_gu.reshape(b_gu.shape[0], n_exp, 1, two_f),
      b_gu.reshape(b_gu.shape[0], n_exp, 1, two_f))
    nd = d // DOWN_TN
    return pl.pallas_call(
        _moe_down_kernel,
        grid_spec=pltpu.PrefetchScalarGridSpec(
            num_scalar_prefetch=2,
            grid=(nd, n_blocks),
            in_specs=[pl.BlockSpec((MOE_TM, f), lambda j, m, be, na: (m, 0)),
                      pl.BlockSpec((None, None, f, DOWN_TN), lambda j, m, be, na: (layer, be[m], 0, j)),
                      pl.BlockSpec((None, None, 1, DOWN_TN), lambda j, m, be, na: (layer, be[m], 0, j))],
            out_specs=pl.BlockSpec((MOE_TM, DOWN_TN), lambda j, m, be, na: (m, j)),
            scratch_shapes=[pltpu.VMEM((f, DOWN_TN), BF16)]),
        out_shape=jax.ShapeDtypeStruct((n_slots, d), F32),
        compiler_params=_cparams(2),
        name="moe_down",
    )(block_expert, n_active, act, w_down, b_down.reshape(b_down.shape[0], n_exp, 1, d))


GATHER_ROWS = 256
COMBINE_TM = 64
DMA_PRIORITIES = 2


def _gather_rows_kernel(idx_ref, nxt_ref, src_ref, o_ref, buf, sems):
    i = pl.program_id(0)
    n = pl.num_programs(0)
    slot = i % 2

    def copy(ids, s, t):
        return pltpu.make_async_copy(src_ref.at[pl.ds(ids[0, t], 1)], buf.at[s, pl.ds(t, 1)], sems.at[s])

    def issue(ids, s):
        def body(g, carry):
            for p in range(DMA_PRIORITIES):
                copy(ids, s, g * DMA_PRIORITIES + p).start(priority=p)
            return carry
        lax.fori_loop(0, GATHER_ROWS // DMA_PRIORITIES, body, 0, unroll=4)

    @pl.when(i == 0)
    def _():
        issue(idx_ref, 0)

    @pl.when(i + 1 < n)
    def _():
        issue(nxt_ref, 1 - slot)

    pltpu.make_async_copy(src_ref.at[pl.ds(0, GATHER_ROWS)], buf.at[slot], sems.at[slot]).wait()
    o_ref[...] = buf[slot]


def _as_row_tiles(a):
    return a.reshape(a.shape[0], a.shape[1] // LANES, LANES)


def _gather_rows(src, idx):
    n = idx.shape[0]
    nb = n // GATHER_ROWS
    ids = idx.reshape(nb, 1, GATHER_ROWS)
    smem_blk = lambda f: pl.BlockSpec((None, 1, GATHER_ROWS), f, memory_space=pltpu.SMEM)
    blk = (GATHER_ROWS,) + src.shape[1:]
    return pl.pallas_call(
        _gather_rows_kernel,
        grid=(nb,),
        in_specs=[smem_blk(lambda i: (i, 0, 0)), smem_blk(lambda i: (jnp.minimum(i + 1, nb - 1), 0, 0)),
                  pl.BlockSpec(memory_space=pl.ANY)],
        out_specs=pl.BlockSpec(blk, lambda i: (i, 0, 0)),
        out_shape=jax.ShapeDtypeStruct((n,) + src.shape[1:], src.dtype),
        scratch_shapes=[pltpu.VMEM((2,) + blk, src.dtype), pltpu.SemaphoreType.DMA((2,))],
        compiler_params=_cparams(1),
        name="moe_dispatch_gather",
    )(ids, ids, src)


def _combine_kernel(idx_ref, nxt_ref, yb_ref, x_ref, gate_ref, g2_ref, o_ref, buf, sems, *, top_k):
    i = pl.program_id(0)
    n = pl.num_programs(0)
    slot = i % 2

    def copy(ids, s, t, k):
        return pltpu.make_async_copy(yb_ref.at[pl.ds(ids[0, t * top_k + k], 1)], buf.at[s, k, pl.ds(t, 1)], sems.at[s])

    def issue(ids, s):
        def body(t, carry):
            for k in range(top_k):
                copy(ids, s, t, k).start(priority=k % DMA_PRIORITIES)
            return carry
        lax.fori_loop(0, COMBINE_TM, body, 0, unroll=2)

    @pl.when(i == 0)
    def _():
        issue(idx_ref, 0)

    @pl.when(i + 1 < n)
    def _():
        issue(nxt_ref, 1 - slot)

    for k in range(top_k):
        pltpu.make_async_copy(yb_ref.at[pl.ds(0, COMBINE_TM)], buf.at[slot, k], sems.at[slot]).wait()

    gate = gate_ref[...]
    g = [gate[:, k:k + 1] for k in range(top_k)]
    for c in range(buf.shape[3]):
        acc = g[0] * buf[slot, 0, :, c, :]
        for k in range(1, top_k):
            acc = acc + g[k] * buf[slot, k, :, c, :]
        cs = slice(c * LANES, (c + 1) * LANES)
        o_ref[:, cs] = x_ref[:, cs] + g2_ref[:, cs] * acc


def _moe_combine(yb, slot_of_assign, gate, x, mod, mod_row, top_k):
    m, d = x.shape
    tm = COMBINE_TM
    nt = m // tm
    row = mod_row(tm)
    ids = slot_of_assign.reshape(nt, 1, tm * top_k)
    smem_blk = lambda f: pl.BlockSpec((None, 1, tm * top_k), f, memory_space=pltpu.SMEM)
    return pl.pallas_call(
        functools.partial(_combine_kernel, top_k=top_k),
        grid=(nt,),
        in_specs=[smem_blk(lambda i: (i, 0, 0)), smem_blk(lambda i: (jnp.minimum(i + 1, nt - 1), 0, 0)),
                  pl.BlockSpec(memory_space=pl.ANY),
                  pl.BlockSpec((tm, d), lambda i: (i, 0)),
                  pl.BlockSpec((tm, LANES), lambda i: (i, 0)),
                  pl.BlockSpec((None, 1, d), lambda i: (row(i), 0, 5))],
        out_specs=pl.BlockSpec((tm, d), lambda i: (i, 0)),
        out_shape=jax.ShapeDtypeStruct((m, d), F32),
        scratch_shapes=[pltpu.VMEM((2, top_k, tm) + yb.shape[1:], F32), pltpu.SemaphoreType.DMA((2,))],
        compiler_params=_cparams(1),
        name="moe_combine",
    )(ids, ids, yb, x, gate, mod)


def _moe(h2, idx, gate, x, mod, mod_row, w_gu, b_gu, w_down, b_down, layer):
    n_exp = w_gu.shape[1]
    slot_token, slot_of_assign, block_expert, n_active = _moe_dispatch(idx[:, :TOP_K], n_exp)
    xb = _gather_rows(_as_row_tiles(h2), slot_token).reshape(slot_token.shape[0], h2.shape[1])
    yb = _moe_experts(xb, block_expert, n_active, w_gu, b_gu, w_down, b_down, layer)
    return _moe_combine(_as_row_tiles(yb), slot_of_assign, gate, x, mod, mod_row, TOP_K)


def _token_shift_mix(xr, mu, groups):
    parts = []
    for row_off, n_seq, seq_len in groups:
        x = xr[row_off:row_off + n_seq * seq_len].reshape(n_seq, seq_len, -1)
        xp = jnp.pad(x, ((0, 0), (1, 1), (0, 0)))
        sh = 0.5 * (xp[:, :-2] + xp[:, 2:])
        parts.append((x + (sh - x) * mu).reshape(n_seq * seq_len, -1))
    return jnp.concatenate(parts, axis=0)


def _pad_w_in(w):
    wb = w.astype(BF16)
    cut = 3 * NA_WIDTH + RW_SHIFT_WIDTH
    return jnp.concatenate([wb[:, :cut], jnp.zeros((w.shape[0], RW_PAD), BF16), wb[:, cut:]], axis=1)


def kernel(x_prompt, x_sample, c, cache_k, cache_v, state_rwkv_fwd, state_rwkv_bwd, state_hgrn_fwd, state_hgrn_bwd, c_ctx, norm1_g, norm2_g, w_ada, b_ada, w_in, w_out, na_rpb, rw_mu, rw_w0, rw_w2, rw_a0, rw_a2, rw_g2, rw_k_k, rw_k_a, rw_r_k, rw_ln_g, rw_ln_b, hg_lb_raw, hg_norm_g, router_w, router_b, exp_w_gu, exp_b_gu, exp_w_down, exp_b_down, final_g):
    n_p, t_p, d = x_prompt.shape
    n_s, t_s, _ = x_sample.shape
    depth = w_in.shape[0]
    mp, ms = n_p * t_p, n_s * t_s
    x = jnp.concatenate([x_prompt.reshape(mp, d), x_sample.reshape(ms, d)], axis=0)
    n_mod_rows = 8
    cond = jnp.concatenate([c_ctx[None], c, jnp.zeros((n_mod_rows - 1 - n_s, d), F32)], axis=0)
    mod_row = lambda tm: _mod_row_fn(tm, mp, t_s)
    p_lb = jax.nn.softmax(hg_lb_raw.astype(F32), axis=0)
    hg_lb = jnp.cumsum(p_lb, axis=0) - p_lb[:1]
    seqs = [(0, n_p, t_p), (mp, n_s, t_s)]
    ctx_out = []
    for l in range(depth):
        mod2d = _ada(cond, w_ada, b_ada, l)
        mod = mod2d.reshape(n_mod_rows, 1, 6 * d)
        proj = _in_proj(x, norm1_g[l], mod, _pad_w_in(w_in[l]), mod_row)
        na_p, k_ctx, v_ctx = _ctx_attention(proj, n_p, t_p)
        na_s = _na_attention(proj, mp, n_s, t_s, cache_k, cache_v, l, _na_bias_table(na_rpb[l], t_s // GRID_W))
        na_o = jnp.concatenate([na_p, na_s], axis=0)
        xr = _token_shift_mix(proj[:, PCOL_RW:PCOL_RW + RW_SHIFT_WIDTH], rw_mu[l], seqs)
        lw = dict(rw_w0=rw_w0[l], rw_w2=rw_w2[l], rw_a0=rw_a0[l], rw_a2=rw_a2[l], rw_g2=rw_g2[l], rw_k_k=rw_k_k[l],
                  rw_k_a=rw_k_a[l], rw_r_k=rw_r_k[l], rw_ln_g=rw_ln_g[l], rw_ln_b=rw_ln_b[l])
        rw_groups = [(0, n_p, t_p, jnp.zeros((2, n_p, RW_HEADS, RW_HEAD_DIM, RW_HEAD_DIM), F32)),
                     (mp, n_s, t_s, jnp.stack([state_rwkv_fwd[:, l], state_rwkv_bwd[:, l]]))]
        rw_o, rw_states = _rwkv_mixer(xr, rw_groups, lw)
        hg_groups = [(0, n_p, t_p, jnp.zeros((2, n_p, HG_HEADS, HG_KEY_DIM, HG_VAL_DIM), F32)),
                     (mp, n_s, t_s, jnp.stack([state_hgrn_fwd[:, l], state_hgrn_bwd[:, l]]))]
        hg_o, hg_states = _hgrn_mixer(proj, hg_groups, hg_lb[l], hg_norm_g[l])
        x, h2, idx, gate = _out_proj(na_o, rw_o, hg_o, x, w_out[l].astype(BF16), mod, norm2_g[l], router_w[l],
                                     router_b[l], mod_row)
        x = _moe(h2, idx, gate, x, mod, mod_row, exp_w_gu, exp_b_gu, exp_w_down, exp_b_down, l)
        ctx_out.append((k_ctx, v_ctx, rw_states[0][0], rw_states[0][1], hg_states[0][0], hg_states[0][1]))
    y = _final_norm(x, final_g)
    outs = [y[:mp].reshape(n_p, t_p, d), y[mp:].reshape(n_s, t_s, d)]
    outs += [jnp.stack([t[i] for t in ctx_out], axis=1) for i in range(6)]
    return tuple(outs)
```

```python
import functools

import numpy as np
import jax
import jax.numpy as jnp
from jax import lax
from jax.experimental import pallas as pl
from jax.experimental.pallas import tpu as pltpu

F32 = jnp.float32
BF16 = jnp.bfloat16

D_MODEL = 2048
GRID_W = 64
NA_HEADS = 16
NA_HEAD_DIM = 64
NA_WIDTH = NA_HEADS * NA_HEAD_DIM
NA_KH = 8
NA_KW = 16
RW_HEADS = 8
RW_HEAD_DIM = 64
RW_WIDTH = RW_HEADS * RW_HEAD_DIM
RW_DECAY_LORA = 64
RW_ICLR_LORA = 64
RW_GATE_LORA = 128
RW_SHIFT_WIDTH = 3 * RW_WIDTH + 2 * RW_DECAY_LORA + 2 * RW_ICLR_LORA + RW_GATE_LORA
RW_GN_EPS = 64e-5
HG_HEADS = 4
HG_KEY_DIM = 128
HG_VAL_DIM = 128
HG_WIDTH = HG_HEADS * HG_VAL_DIM
HG_MIN_F = 1e-6
IN_WIDTH = 3 * NA_WIDTH + RW_SHIFT_WIDTH + 5 * HG_WIDTH
N_EXPERTS = 32
TOP_K = 4
D_EXPERT = 2048
SWIGLU_LIMIT = 7.0
SWIGLU_ALPHA = 1.702
RMS_EPS = 1e-6
NEG_INF = -1e30

LANES = 128
PCOL_Q = 0
PCOL_K = NA_WIDTH
PCOL_V = 2 * NA_WIDTH
PCOL_RW = 3 * NA_WIDTH
RW_PAD = -RW_SHIFT_WIDTH % HG_WIDTH
PCOL_HQ = PCOL_RW + RW_SHIFT_WIDTH + RW_PAD
PCOL_HFF = PCOL_HQ + HG_WIDTH
PCOL_HFB = PCOL_HFF + HG_WIDTH
PCOL_HI = PCOL_HFB + HG_WIDTH
PCOL_HG = PCOL_HI + HG_WIDTH
IN_WIDTH_PAD = PCOL_HG + HG_WIDTH
SCAN_CHUNK = 64
TOKEN_TILE = 256
VMEM_LIMIT = 56 * 1024 * 1024


def _cparams(n_axes, vmem=VMEM_LIMIT):
    return pltpu.CompilerParams(dimension_semantics=("arbitrary",) * n_axes, vmem_limit_bytes=vmem)


def _dot(a, b):
    return jnp.dot(a.astype(BF16), b.astype(BF16), preferred_element_type=F32)


def _dot_nt(a, b):
    return lax.dot_general(a.astype(BF16), b.astype(BF16), (((1,), (1,)), ((), ())), preferred_element_type=F32)


def _dot_tn(a, b):
    return lax.dot_general(a.astype(BF16), b.astype(BF16), (((0,), (0,)), ((), ())), preferred_element_type=F32)


def _split3(x):
    hi = x.astype(BF16)
    r1 = x - hi.astype(F32)
    mid = r1.astype(BF16)
    lo = (r1 - mid.astype(F32)).astype(BF16)
    return hi, mid, lo


def _dot01_left(m01, x):
    hi, mid, lo = _split3(x)
    d = lambda p: jnp.dot(m01, p, preferred_element_type=F32)
    return d(hi) + (d(mid) + d(lo))


def _dot01_right(x, m01):
    hi, mid, lo = _split3(x)
    d = lambda p: jnp.dot(p, m01, preferred_element_type=F32)
    return d(hi) + (d(mid) + d(lo))


def _sigmoid(x):
    return 1.0 / (1.0 + jnp.exp(-x))


def _softplus(x):
    return jnp.maximum(x, 0.0) + jnp.log(1.0 + jnp.exp(-jnp.abs(x)))


def _head_sum_matrix(width, head_dim):
    i = np.arange(width)
    return jnp.asarray((i[:, None] // head_dim) == (i[None, :] // head_dim), dtype=BF16)


def _order_masks(n, reverse):
    row = lax.broadcasted_iota(jnp.int32, (n, n), 0)
    col = lax.broadcasted_iota(jnp.int32, (n, n), 1)
    diff = col - row if reverse else row - col
    return diff >= 0, diff > 0


def _rwkv_prep_kernel(x_ref, w0_ref, w2_ref, a0_ref, a2_ref, g2_ref, kkw_ref, kaw_ref, hsum_ref,
                      kkn_ref, g_ref, lw_ref, kd_ref, av_ref):
    x = x_ref[...]
    k = x[:, RW_WIDTH:2 * RW_WIDTH]
    o = 3 * RW_WIDTH
    wd = (x[:, o:o + 64], x[:, o + 64:o + 128])
    ad = (x[:, o + 128:o + 192], x[:, o + 192:o + 256])
    gd = x[:, o + 256:o + 384]
    g_ref[...] = _dot(_sigmoid(gd), g2_ref[...])
    kk = k * kkw_ref[...]
    ssq = _dot01_right(kk * kk, hsum_ref[...])
    kkn_ref[...] = kk / jnp.maximum(jnp.sqrt(ssq), 1e-12)
    for d in range(2):
        wl = -_softplus(-(w0_ref[d] + _dot(jnp.tanh(wd[d]), w2_ref[d]))) - 0.5
        lw_ref[d] = -jnp.exp(wl)
        a = _sigmoid(a0_ref[d] + _dot(ad[d], a2_ref[d]))
        kd_ref[d] = k * (1.0 + (a - 1.0) * kaw_ref[...])
        av_ref[d] = a


def _rwkv_scan_kernel(*refs, n_chunks):
    ins = (refs[0:6], refs[6:12])
    s0_ref, yf_ref, yb_ref, sout_ref, s_scr = refs[12:]
    c = pl.program_id(1)
    n = SCAN_CHUNK

    @pl.when(c == 0)
    def _():
        s_scr[...] = s0_ref[...]

    masks = [_order_masks(n, False), _order_masks(n, True)]
    incl01 = [m[0].astype(BF16) for m in masks]
    ch = [(d, h) for d in range(2) for h in range(RW_HEADS)]
    cs = range(len(ch))
    m_incl = [masks[d][0] for d, _ in ch]
    m_strict = [masks[d][1] for d, _ in ch]
    col = lambda j: [ins[d][j][:, h * RW_HEAD_DIM:(h + 1) * RW_HEAD_DIM] for d, h in ch]
    r, v, kkn, lw, kd, av = (col(j) for j in range(6))
    cum = [_dot01_left(incl01[ch[i][0]], lw[i]) for i in cs]
    p_tot = [jnp.exp(jnp.sum(lw[i], axis=0, keepdims=True)) for i in cs]
    p_inv = [jnp.exp(-cum[i]) for i in cs]
    a_t = [-kkn[i] * jnp.exp(cum[i] - lw[i]) for i in cs]
    b_t = [kkn[i] * av[i] * p_inv[i] for i in cs]
    k_t = [kd[i] * p_inv[i] for i in cs]
    r_t = [r[i] * jnp.exp(cum[i]) for i in cs]
    a_ab = [jnp.where(m_strict[i], _dot_nt(a_t[i], b_t[i]), 0.0) for i in cs]
    a_ak = [jnp.where(m_strict[i], _dot_nt(a_t[i], k_t[i]), 0.0) for i in cs]
    a_rb = [jnp.where(m_incl[i], _dot_nt(r_t[i], b_t[i]), 0.0) for i in cs]
    a_rk = [jnp.where(m_incl[i], _dot_nt(r_t[i], k_t[i]), 0.0) for i in cs]
    t_m = list(a_ab)
    n_pow = list(a_ab)
    for _ in range(int(np.log2(n)) - 1):
        n_pow = [_dot(n_pow[i], n_pow[i]) for i in cs]
        t_m = [t_m[i] + n_pow[i] + _dot(t_m[i], n_pow[i]) for i in cs]
    x1 = [_dot(a_ak[i], v[i]) for i in cs]
    w1 = [x1[i] + _dot(t_m[i], x1[i]) for i in cs]
    w2 = [a_t[i] + _dot(t_m[i], a_t[i]) for i in cs]
    y0 = [_dot(a_rk[i], v[i]) + _dot(a_rb[i], w1[i]) for i in cs]
    q = [r_t[i] + _dot(a_rb[i], w2[i]) for i in cs]
    g_m = [_dot_tn(w2[i], b_t[i]) for i in cs]
    h_m = [_dot_tn(w1[i], b_t[i]) + _dot_tn(v[i], k_t[i]) for i in cs]
    s0 = [s_scr[d, h] for d, h in ch]
    ys = [y0[i] + _dot_nt(q[i], s0[i]) for i in cs]
    s_new = [(s0[i] + _dot(s0[i], g_m[i]) + h_m[i]) * p_tot[i] for i in cs]
    for i, (d, h) in enumerate(ch):
        s_scr[d, h] = s_new[i]
    yf_ref[...] = jnp.concatenate(ys[:RW_HEADS], axis=-1)
    yb_ref[...] = jnp.concatenate(ys[RW_HEADS:], axis=-1)

    @pl.when(c == n_chunks - 1)
    def _():
        sout_ref[...] = s_scr[...]


def _rwkv_post_kernel(yf_ref, yb_ref, x_ref, kd_ref, g_ref, rk_ref, lng_ref, lnb_ref, hsum_ref, o_ref):
    x = x_ref[...]
    r = x[:, 0:RW_WIDTH]
    v = x[:, 2 * RW_WIDTH:3 * RW_WIDTH]
    y = yf_ref[...] + yb_ref[...]
    hsum = hsum_ref[...]
    inv_n = 1.0 / RW_HEAD_DIM
    mu = _dot01_right(y, hsum) * inv_n
    yc = y - mu
    var = _dot01_right(yc * yc, hsum) * inv_n
    yn = yc * lax.rsqrt(var + RW_GN_EPS) * lng_ref[...] + lnb_ref[...]
    bonus = _dot01_right(r * (0.5 * (kd_ref[0] + kd_ref[1])) * rk_ref[...], hsum) * v
    o_ref[...] = (yn + bonus) * g_ref[...]


def _rwkv_mixer(xs, groups, lw):
    m = xs.shape[0]
    tm = TOKEN_TILE
    hsum = _head_sum_matrix(RW_WIDTH, RW_HEAD_DIM)
    row = lambda a: a.reshape(1, -1)
    full = lambda shape: pl.BlockSpec(shape, lambda i: (0,) * len(shape))
    tok = lambda w: pl.BlockSpec((tm, w), lambda i: (i, 0))
    tok2 = pl.BlockSpec((2, tm, RW_WIDTH), lambda i: (0, i, 0))
    d2 = jax.ShapeDtypeStruct((2, m, RW_WIDTH), F32)
    d1 = jax.ShapeDtypeStruct((m, RW_WIDTH), F32)
    kkn, g, lwd, kd, av = pl.pallas_call(
        _rwkv_prep_kernel,
        grid=(m // tm,),
        in_specs=[tok(RW_SHIFT_WIDTH), full((2, 1, RW_WIDTH)), full((2, RW_DECAY_LORA, RW_WIDTH)),
                  full((2, 1, RW_WIDTH)), full((2, RW_ICLR_LORA, RW_WIDTH)), full((RW_GATE_LORA, RW_WIDTH)),
                  full((1, RW_WIDTH)), full((1, RW_WIDTH)), full((RW_WIDTH, RW_WIDTH))],
        out_specs=[tok(RW_WIDTH), tok(RW_WIDTH), tok2, tok2, tok2],
        out_shape=[d1, d1, d2, d2, d2],
        compiler_params=_cparams(1),
        name="rwkv_prep",
    )(xs, lw['rw_w0'].reshape(2, 1, RW_WIDTH), lw['rw_w2'], lw['rw_a0'].reshape(2, 1, RW_WIDTH), lw['rw_a2'],
      lw['rw_g2'], row(lw['rw_k_k']), row(lw['rw_k_a']), hsum)

    n = SCAN_CHUNK
    yfs, ybs, states = [], [], []
    for row_off, n_seq, seq_len, s0 in groups:
        nc = seq_len // n
        base = row_off // n
        chunk = lambda b, c, d, nc=nc: b * nc + (nc - 1 - c if d else c)
        x_spec = lambda d, col, base=base, chunk=chunk: pl.BlockSpec(
            (n, RW_WIDTH), lambda b, c: (base + chunk(b, c, d), col))
        d_spec = lambda d, base=base, chunk=chunk: pl.BlockSpec(
            (None, n, RW_WIDTH), lambda b, c: (d, base + chunk(b, c, d), 0))
        y_spec = lambda d, chunk=chunk: pl.BlockSpec((n, RW_WIDTH), lambda b, c: (chunk(b, c, d), 0))
        s_spec = pl.BlockSpec((2, None, RW_HEADS, RW_HEAD_DIM, RW_HEAD_DIM), lambda b, c: (0, b, 0, 0, 0))
        per_dir = lambda d: [x_spec(d, 0), x_spec(d, 2), x_spec(d, 0), d_spec(d), d_spec(d), d_spec(d)]
        y_shape = jax.ShapeDtypeStruct((n_seq * seq_len, RW_WIDTH), F32)
        yf, yb, s_out = pl.pallas_call(
            functools.partial(_rwkv_scan_kernel, n_chunks=nc),
            grid=(n_seq, nc),
            in_specs=per_dir(0) + per_dir(1) + [s_spec],
            out_specs=[y_spec(0), y_spec(1), s_spec],
            out_shape=[y_shape, y_shape,
                       jax.ShapeDtypeStruct((2, n_seq, RW_HEADS, RW_HEAD_DIM, RW_HEAD_DIM), F32)],
            scratch_shapes=[pltpu.VMEM((2, RW_HEADS, RW_HEAD_DIM, RW_HEAD_DIM), F32)],
            compiler_params=_cparams(2),
            name="rwkv_scan",
        )(*([xs, xs, kkn, lwd, kd, av] * 2), s0)
        yfs.append(yf)
        ybs.append(yb)
        states.append(s_out)
    yf = jnp.concatenate(yfs, axis=0)
    yb = jnp.concatenate(ybs, axis=0)

    out = pl.pallas_call(
        _rwkv_post_kernel,
        grid=(m // tm,),
        in_specs=[tok(RW_WIDTH), tok(RW_WIDTH), tok(RW_SHIFT_WIDTH), tok2, tok(RW_WIDTH), full((1, RW_WIDTH)),
                  full((1, RW_WIDTH)), full((1, RW_WIDTH)), full((RW_WIDTH, RW_WIDTH))],
        out_specs=tok(RW_WIDTH),
        out_shape=d1,
        compiler_params=_cparams(1),
        name="rwkv_post",
    )(yf, yb, xs, kd, g, row(lw['rw_r_k']), row(lw['rw_ln_g']), row(lw['rw_ln_b']), hsum)
    return out, states


HG_LEVELS = (32, 16, 8, 4, 2, 1)


def _hgrn_consts():
    n = SCAN_CHUNK
    cums, masks = [], []
    for rev in (False, True):
        p = np.arange(n)[::-1] if rev else np.arange(n)
        pt, pj = p[:, None], p[None, :]
        rows = [pj <= pt, pj > pt]
        rows += [((pt // m) % 2 == 1) & (pj // m == pt // m) & (pj <= pt) for m in HG_LEVELS]
        rows += [((pt // m) % 2 == 0) & (pj // m == pt // m) & (pj > pt) for m in HG_LEVELS]
        cums.append(np.concatenate(rows, 0))
        mk = [pt == pj]
        mk += [(pt // (2 * m) == pj // (2 * m)) & ((pt // m) % 2 == 1) & ((pj // m) % 2 == 0) for m in HG_LEVELS]
        masks.append(np.stack(mk))
    return jnp.asarray(np.stack(cums), BF16), jnp.asarray(np.stack(masks), F32)


def _hgrn_scan_kernel(qf_ref, zf_ref, vf_ref, qb_ref, zb_ref, vb_ref, lb_ref, cm_ref, mask_ref, s0_ref,
                      of_ref, ob_ref, sout_ref, s_scr, *, n_chunks):
    c = pl.program_id(1)
    n = SCAN_CHUNK
    nl = len(HG_LEVELS)
    ins = ((qf_ref, zf_ref, vf_ref), (qb_ref, zb_ref, vb_ref))

    @pl.when(c == 0)
    def _():
        s_scr[...] = s0_ref[...]

    ones = jnp.ones((n, HG_VAL_DIM), BF16)
    tn01 = lambda p: lax.dot_general(p, ones, (((0,), (0,)), ((), ())), preferred_element_type=F32)
    ch = [(d, h) for d in range(2) for h in range(HG_HEADS)]
    cs = range(len(ch))
    col = lambda j: [ins[d][j][:, h * HG_KEY_DIM:(h + 1) * HG_KEY_DIM] for d, h in ch]
    hq, z, v = col(0), col(1), col(2)
    lb = [lb_ref[d, :, h * HG_KEY_DIM:(h + 1) * HG_KEY_DIM] for d, h in ch]
    q = [hq[i] * _sigmoid(hq[i]) for i in cs]
    lf = [jnp.log(jnp.maximum(lb[i] + (1.0 - lb[i]) * _sigmoid(z[i]), HG_MIN_F)) for i in cs]
    k = [(1.0 - lb[i]) * _sigmoid(-z[i]) for i in cs]
    parts = [_split3(lf[i]) for i in cs]
    ex = [jnp.exp(sum(jnp.dot(cm_ref[ch[i][0]], p, preferred_element_type=F32) for p in parts[i][::-1])) for i in cs]
    p_col = [jnp.exp(sum(tn01(p) for p in parts[i][::-1])) for i in cs]
    att = [mask_ref[ch[i][0], 0] * _dot_nt(q[i], k[i]) for i in cs]
    for lv in range(nl):
        lo_q, lo_k = (2 + lv) * n, (2 + nl + lv) * n
        lvl = [_dot_nt(q[i] * ex[i][lo_q:lo_q + n], k[i] * ex[i][lo_k:lo_k + n]) for i in cs]
        att = [att[i] + mask_ref[ch[i][0], lv + 1] * lvl[i] for i in cs]
    s0 = [s_scr[d, h] for d, h in ch]
    outs = [_dot(q[i] * ex[i][0:n], s0[i]) + _dot(att[i], v[i]) for i in cs]
    s_new = [s0[i] * p_col[i] + _dot_tn(k[i] * ex[i][n:2 * n], v[i]) for i in cs]
    for i, (d, h) in enumerate(ch):
        s_scr[d, h] = s_new[i]
    of_ref[...] = jnp.concatenate(outs[:HG_HEADS], axis=-1)
    ob_ref[...] = jnp.concatenate(outs[HG_HEADS:], axis=-1)

    @pl.when(c == n_chunks - 1)
    def _():
        sout_ref[...] = s_scr[...]


def _hgrn_post_kernel(of_ref, ob_ref, g_ref, gain_ref, hsum_ref, out_ref):
    o = of_ref[...] + ob_ref[...]
    ms = _dot01_right(o * o, hsum_ref[...]) * (1.0 / HG_VAL_DIM)
    on = o * lax.rsqrt(ms + RMS_EPS) * gain_ref[...]
    g = g_ref[...]
    out_ref[...] = on * (g * _sigmoid(g))


def _hgrn_mixer(proj, groups, lb, gain):
    m = proj.shape[0]
    n = SCAN_CHUNK
    cm, masks = _hgrn_consts()
    cq, cff, ci, cg = (col // HG_WIDTH for col in (PCOL_HQ, PCOL_HFF, PCOL_HI, PCOL_HG))
    ofs, obs, states = [], [], []
    for row_off, n_seq, seq_len, s0 in groups:
        nc = seq_len // n
        base = row_off // n
        chunk = lambda b, c, d, nc=nc: b * nc + (nc - 1 - c if d else c)
        x_spec = lambda d, col, base=base, chunk=chunk: pl.BlockSpec(
            (n, HG_WIDTH), lambda b, c: (base + chunk(b, c, d), col))
        o_spec = lambda d, chunk=chunk: pl.BlockSpec((n, HG_WIDTH), lambda b, c: (chunk(b, c, d), 0))
        s_spec = pl.BlockSpec((2, None, HG_HEADS, HG_KEY_DIM, HG_VAL_DIM), lambda b, c: (0, b, 0, 0, 0))
        whole = lambda a: pl.BlockSpec(a.shape, lambda b, c: (0,) * a.ndim)
        lb3 = lb.reshape(2, 1, HG_WIDTH)
        o_shape = jax.ShapeDtypeStruct((n_seq * seq_len, HG_WIDTH), F32)
        of, ob, s_out = pl.pallas_call(
            functools.partial(_hgrn_scan_kernel, n_chunks=nc),
            grid=(n_seq, nc),
            in_specs=[x_spec(0, cq), x_spec(0, cff), x_spec(0, ci), x_spec(1, cq), x_spec(1, cff + 1), x_spec(1, ci),
                      whole(lb3), whole(cm), whole(masks), s_spec],
            out_specs=[o_spec(0), o_spec(1), s_spec],
            out_shape=[o_shape, o_shape,
                       jax.ShapeDtypeStruct((2, n_seq, HG_HEADS, HG_KEY_DIM, HG_VAL_DIM), F32)],
            scratch_shapes=[pltpu.VMEM((2, HG_HEADS, HG_KEY_DIM, HG_VAL_DIM), F32)],
            compiler_params=_cparams(2),
            name="hgrn_scan",
        )(proj, proj, proj, proj, proj, proj, lb3, cm, masks, s0)
        ofs.append(of)
        obs.append(ob)
        states.append(s_out)
    of = jnp.concatenate(ofs, axis=0)
    ob = jnp.concatenate(obs, axis=0)

    tm = TOKEN_TILE
    tok = pl.BlockSpec((tm, HG_WIDTH), lambda i: (i, 0))
    out = pl.pallas_call(
        _hgrn_post_kernel,
        grid=(m // tm,),
        in_specs=[tok, tok,
                  pl.BlockSpec((tm, HG_WIDTH), lambda i: (i, cg)),
                  pl.BlockSpec((1, HG_WIDTH), lambda i: (0, 0)),
                  pl.BlockSpec((HG_WIDTH, HG_WIDTH), lambda i: (0, 0))],
        out_specs=pl.BlockSpec((tm, HG_WIDTH), lambda i: (i, 0)),
        out_shape=jax.ShapeDtypeStruct((m, HG_WIDTH), F32),
        compiler_params=_cparams(1),
        name="hgrn_post",
    )(of, ob, proj, gain.reshape(1, HG_WIDTH), _head_sum_matrix(HG_WIDTH, HG_VAL_DIM))
    return out, states


NA_HEADS_PER_STEP = LANES // NA_HEAD_DIM
NA_ROWS_PER_ITER = 4


def _ctx_attn_kernel(q_ref, k_ref, v_ref, o_ref, kc_ref, vc_ref):
    scale = NA_HEAD_DIM ** -0.5
    hs = range(NA_HEADS_PER_STEP)
    sl = [slice(h * NA_HEAD_DIM, (h + 1) * NA_HEAD_DIM) for h in hs]
    k = [k_ref[:, s] for s in sl]
    v = [v_ref[:, s] for s in sl]
    for h in hs:
        kc_ref[h] = k[h]
        vc_ref[h] = v[h]
    s = [_dot_nt(q_ref[:, sl[h]], k[h]) * scale for h in hs]
    p = [jnp.exp(s[h] - jnp.max(s[h], axis=-1, keepdims=True)) for h in hs]
    o = [_dot(p[h], v[h]) / jnp.sum(p[h], axis=-1, keepdims=True) for h in hs]
    o_ref[...] = jnp.concatenate(o, axis=-1)


def _ctx_attention(proj, n_seq, seq_len):
    nb = NA_WIDTH // LANES
    blk = lambda off: pl.BlockSpec((seq_len, LANES), lambda b, j: (b, off + j))
    c_spec = pl.BlockSpec((None, NA_HEADS_PER_STEP, seq_len, NA_HEAD_DIM), lambda b, j: (b, j, 0, 0))
    c_shape = jax.ShapeDtypeStruct((n_seq, NA_HEADS, seq_len, NA_HEAD_DIM), F32)
    return pl.pallas_call(
        _ctx_attn_kernel,
        grid=(n_seq, nb),
        in_specs=[blk(PCOL_Q // LANES), blk(PCOL_K // LANES), blk(PCOL_V // LANES)],
        out_specs=[pl.BlockSpec((seq_len, LANES), lambda b, j: (b, j)), c_spec, c_spec],
        out_shape=[jax.ShapeDtypeStruct((n_seq * seq_len, NA_WIDTH), F32), c_shape, c_shape],
        compiler_params=_cparams(2),
        name="ctx_attention",
    )(proj, proj, proj)


def _na_bias_table(rpb, rows):
    assert rows >= NA_KH
    o = np.arange(NA_KH)[:, None]
    i = np.arange(NA_KH)[None, :]
    dr = i - o + (NA_KH - 1)
    cq = np.arange(GRID_W)
    ck = np.arange(GRID_W)
    col_start = np.clip(cq - NA_KW // 2, 0, GRID_W - NA_KW)
    col_mask = (ck[None, :] >= col_start[:, None]) & (ck[None, :] < col_start[:, None] + NA_KW)
    dc = np.clip(ck[None, :] - cq[:, None], -(NA_KW - 1), NA_KW - 1) + (NA_KW - 1)
    onehot = jnp.asarray(dc.reshape(-1)[:, None] == np.arange(2 * NA_KW - 1)[None, :], F32)
    full = jnp.einsum('hrj,pj->hrp', rpb.astype(F32), onehot, precision=lax.Precision.HIGHEST)
    full = jnp.where(col_mask.reshape(-1)[None, None, :], full, NEG_INF).reshape(rpb.shape[0], -1, GRID_W, GRID_W)
    lo = lambda off: int(dr[off, 0])
    tab = jnp.stack([full[:, lo(off):lo(off) + NA_KH] for off in range(NA_KH)])
    return tab.transpose(0, 1, 3, 2, 4).reshape(NA_KH, rpb.shape[0], GRID_W, NA_KH * GRID_W)


def _na_kernel(q_ref, k_ref, v_ref, ck_ref, cv_ref, bias_ref, o_ref, *, rows):
    scale = NA_HEAD_DIM ** -0.5
    n_win = NA_KH * GRID_W

    def row_group(g, carry):
        ch = []
        for i in range(NA_ROWS_PER_ITER):
            r = g * NA_ROWS_PER_ITER + i
            r0 = jnp.clip(r - NA_KH // 2, 0, rows - NA_KH)
            q_at = pl.multiple_of(r * GRID_W, GRID_W)
            k_at = pl.multiple_of(r0 * GRID_W, GRID_W)
            for h in range(NA_HEADS_PER_STEP):
                ch.append((q_at, k_at, r - r0, h, slice(h * NA_HEAD_DIM, (h + 1) * NA_HEAD_DIM)))
        q = [q_ref[pl.ds(q_at, GRID_W), sl] for q_at, _, _, _, sl in ch]
        s_w = [_dot_nt(q[i], k_ref[pl.ds(k_at, n_win), sl]) * scale + bias_ref[off, h]
               for i, (_, k_at, off, h, sl) in enumerate(ch)]
        s_c = [_dot_nt(q[i], ck_ref[h]) * scale for i, (_, _, _, h, _) in enumerate(ch)]
        m = [jnp.maximum(jnp.max(a, axis=-1, keepdims=True), jnp.max(b, axis=-1, keepdims=True))
             for a, b in zip(s_w, s_c)]
        p_w = [jnp.exp(a - mm) for a, mm in zip(s_w, m)]
        p_c = [jnp.exp(b - mm) for b, mm in zip(s_c, m)]
        den = [jnp.sum(a, axis=-1, keepdims=True) + jnp.sum(b, axis=-1, keepdims=True) for a, b in zip(p_w, p_c)]
        o = [(_dot(p_w[i], v_ref[pl.ds(k_at, n_win), sl]) + _dot(p_c[i], cv_ref[h])) / den[i]
             for i, (_, k_at, _, h, sl) in enumerate(ch)]
        for i, (q_at, _, _, _, sl) in enumerate(ch):
            o_ref[pl.ds(q_at, GRID_W), sl] = o[i]
        return carry

    lax.fori_loop(0, rows // NA_ROWS_PER_ITER, row_group, 0)


def _na_attention(proj, row_off, n_seq, seq_len, cache_k, cache_v, layer, bias):
    assert row_off % seq_len == 0
    nb = NA_WIDTH // LANES
    rows = seq_len // GRID_W
    assert rows % NA_ROWS_PER_ITER == 0
    rb = row_off // seq_len
    past = cache_k.shape[3]
    blk = lambda off: pl.BlockSpec((seq_len, LANES), lambda b, j: (rb + b, off + j))
    c_spec = pl.BlockSpec((None, None, NA_HEADS_PER_STEP, past, NA_HEAD_DIM), lambda b, j: (b, layer, j, 0, 0))
    return pl.pallas_call(
        functools.partial(_na_kernel, rows=rows),
        grid=(n_seq, nb),
        in_specs=[blk(PCOL_Q // LANES), blk(PCOL_K // LANES), blk(PCOL_V // LANES), c_spec, c_spec,
                  pl.BlockSpec((NA_KH, NA_HEADS_PER_STEP, GRID_W, NA_KH * GRID_W), lambda b, j: (0, j, 0, 0))],
        out_specs=pl.BlockSpec((seq_len, LANES), lambda b, j: (b, j)),
        out_shape=jax.ShapeDtypeStruct((n_seq * seq_len, NA_WIDTH), F32),
        compiler_params=_cparams(2),
        name="na_attention",
    )(proj, proj, proj, cache_k, cache_v, bias)


ADA_TN = 1024
IN_TM = 1024
IN_TN = 1536
OUT_TM = 512


def _mod_row_fn(tm, n_ctx_rows, latent_len):
    def f(i):
        start = i * tm
        return jnp.where(start < n_ctx_rows, 0, 1 + (start - n_ctx_rows) // latent_len)
    return f


def _ada_kernel(c_ref, w_ref, b_ref, o_ref):
    x = c_ref[...]
    o_ref[...] = _dot(x * _sigmoid(x), w_ref[...]) + b_ref[...]


def _ada(cond, w_ada, b_ada, layer):
    n_rows, d = cond.shape
    n = w_ada.shape[2]
    return pl.pallas_call(
        _ada_kernel,
        grid=(n // ADA_TN,),
        in_specs=[pl.BlockSpec((n_rows, d), lambda j: (0, 0)),
                  pl.BlockSpec((None, d, ADA_TN), lambda j: (layer, 0, j)),
                  pl.BlockSpec((None, 1, ADA_TN), lambda j: (layer, 0, j))],
        out_specs=pl.BlockSpec((n_rows, ADA_TN), lambda j: (0, j)),
        out_shape=jax.ShapeDtypeStruct((n_rows, n), F32),
        compiler_params=_cparams(1),
        name="ada_mod",
    )(cond, w_ada, b_ada.reshape(b_ada.shape[0], 1, n))


def _rms_mod(x, g, sc, sh):
    y = x * lax.rsqrt(jnp.mean(x * x, axis=-1, keepdims=True) + RMS_EPS) * g
    return y * (1.0 + sc) + sh


def _in_proj_kernel(x_ref, g_ref, sh_ref, sc_ref, w_ref, o_ref, h_scr):
    @pl.when(pl.program_id(1) == 0)
    def _():
        h_scr[...] = _rms_mod(x_ref[...], g_ref[...], sc_ref[...], sh_ref[...]).astype(BF16)

    o_ref[...] = jnp.dot(h_scr[...], w_ref[...], preferred_element_type=F32)


def _in_proj(x, norm_g, mod, w, mod_row):
    m, d = x.shape
    n = w.shape[1]
    row = mod_row(IN_TM)
    return pl.pallas_call(
        _in_proj_kernel,
        grid=(m // IN_TM, n // IN_TN),
        in_specs=[pl.BlockSpec((IN_TM, d), lambda i, j: (i, 0)),
                  pl.BlockSpec((1, d), lambda i, j: (0, 0)),
                  pl.BlockSpec((None, 1, d), lambda i, j: (row(i), 0, 0)),
                  pl.BlockSpec((None, 1, d), lambda i, j: (row(i), 0, 1)),
                  pl.BlockSpec((d, IN_TN), lambda i, j: (0, j))],
        out_specs=pl.BlockSpec((IN_TM, IN_TN), lambda i, j: (i, j)),
        out_shape=jax.ShapeDtypeStruct((m, n), F32),
        scratch_shapes=[pltpu.VMEM((IN_TM, d), BF16)],
        compiler_params=_cparams(2),
        name="in_proj",
    )(x, norm_g.reshape(1, d), mod, mod, w)


def _out_proj_kernel(na_ref, rw_ref, hg_ref, x_ref, w_ref, g1_ref, n2_ref, sh2_ref, sc2_ref, rtw_ref, rtb_ref,
                     xo_ref, h2_ref, idx_ref, gate_ref):
    o1, o2 = NA_WIDTH, NA_WIDTH + RW_WIDTH
    mix = _dot(na_ref[...], w_ref[0:o1]) + _dot(rw_ref[...], w_ref[o1:o2]) + _dot(hg_ref[...], w_ref[o2:])
    x = x_ref[...] + g1_ref[...] * mix
    xo_ref[...] = x
    h = _rms_mod(x, n2_ref[...], sc2_ref[...], sh2_ref[...])
    h_hi = h.astype(BF16)
    h2_ref[...] = h_hi
    h_mid = (h - h_hi.astype(F32)).astype(BF16)
    d = lambda a, b: jnp.dot(a, b, preferred_element_type=F32)
    logits = d(h_hi, rtw_ref[0]) + (d(h_hi, rtw_ref[1]) + d(h_mid, rtw_ref[0])) + rtb_ref[...]
    lane = lax.broadcasted_iota(jnp.int32, logits.shape, 1).astype(F32)
    idx = jnp.zeros_like(logits)
    vals = []
    for k in range(TOP_K):
        mx = jnp.max(logits, axis=-1, keepdims=True)
        sel = jnp.min(jnp.where(logits == mx, lane, float(LANES)), axis=-1, keepdims=True)
        idx = jnp.where(lane == k, sel, idx)
        vals.append(mx)
        logits = jnp.where(lane == sel, 3.0 * NEG_INF, logits)
    es = [jnp.exp(v - vals[0]) for v in vals]
    den = es[0] + es[1] + es[2] + es[3]
    gate = jnp.zeros_like(idx)
    for k in range(TOP_K):
        gate = jnp.where(lane == k, es[k] / den, gate)
    idx_ref[...] = idx.astype(jnp.int32)
    gate_ref[...] = gate


def _out_proj(na_o, rw_o, hg_o, x, w_out, mod, norm2_g, router_w, router_b, mod_row):
    m, d = x.shape
    tm = OUT_TM
    row = mod_row(tm)
    n_exp = router_w.shape[1]
    rtw = jnp.pad(router_w.astype(F32), ((0, 0), (0, LANES - n_exp)))
    rtw_hi = rtw.astype(BF16)
    rtw = jnp.stack([rtw_hi, (rtw - rtw_hi.astype(F32)).astype(BF16)])
    rtb = jnp.pad(router_b, (0, LANES - n_exp), constant_values=NEG_INF).reshape(1, LANES)
    tok = lambda w: pl.BlockSpec((tm, w), lambda i: (i, 0))
    mod_blk = lambda part: pl.BlockSpec((None, 1, d), lambda i: (row(i), 0, part))
    return pl.pallas_call(
        _out_proj_kernel,
        grid=(m // tm,),
        in_specs=[tok(NA_WIDTH), tok(RW_WIDTH), tok(HG_WIDTH), tok(d),
                  pl.BlockSpec((d, d), lambda i: (0, 0)),
                  mod_blk(2), pl.BlockSpec((1, d), lambda i: (0, 0)), mod_blk(3), mod_blk(4),
                  pl.BlockSpec((2, d, LANES), lambda i: (0, 0, 0)), pl.BlockSpec((1, LANES), lambda i: (0, 0))],
        out_specs=[tok(d), tok(d), tok(LANES), tok(LANES)],
        out_shape=[jax.ShapeDtypeStruct((m, d), F32), jax.ShapeDtypeStruct((m, d), BF16),
                   jax.ShapeDtypeStruct((m, LANES), jnp.int32), jax.ShapeDtypeStruct((m, LANES), F32)],
        compiler_params=_cparams(1),
        name="out_proj_router",
    )(na_o, rw_o, hg_o, x, w_out, mod, norm2_g.reshape(1, d), mod, mod, rtw, rtb)


def _final_norm_kernel(x_ref, g_ref, o_ref):
    x = x_ref[...]
    o_ref[...] = x * lax.rsqrt(jnp.mean(x * x, axis=-1, keepdims=True) + RMS_EPS) * g_ref[...]


def _final_norm(x, g):
    m, d = x.shape
    tm = TOKEN_TILE
    return pl.pallas_call(
        _final_norm_kernel,
        grid=(m // tm,),
        in_specs=[pl.BlockSpec((tm, d), lambda i: (i, 0)), pl.BlockSpec((1, d), lambda i: (0, 0))],
        out_specs=pl.BlockSpec((tm, d), lambda i: (i, 0)),
        out_shape=jax.ShapeDtypeStruct((m, d), F32),
        compiler_params=_cparams(1),
        name="final_norm",
    )(x, g.reshape(1, d))


MOE_TM = 512
GU_TN = 1024
DOWN_TN = 1024


def _moe_dispatch(top_e, gates, n_exp):
    n_tok, top_k = top_e.shape
    n_assign = n_tok * top_k
    i32 = jnp.int32
    flat_e = top_e.reshape(-1).astype(i32)
    assign = jnp.arange(n_assign, dtype=i32)
    experts = jnp.arange(n_exp, dtype=i32)
    lookup = lambda table, e: jnp.sum(jnp.where(e[:, None] == experts[None, :], table[None, :], 0), axis=1)
    sorted_e, order, gate_sorted = lax.sort((flat_e, assign, gates.reshape(-1)), num_keys=1)
    counts = jnp.sum((flat_e[:, None] == experts[None, :]).astype(i32), axis=0)
    padded = (counts + MOE_TM - 1) // MOE_TM * MOE_TM
    pad_end = jnp.cumsum(padded)
    pad_start = pad_end - padded
    start = jnp.cumsum(counts) - counts
    shift = pad_start - start
    slot_sorted = assign + lookup(shift, sorted_e)
    _, slot_of_assign = lax.sort((order, slot_sorted), num_keys=1)
    n_blocks = -(-n_assign // MOE_TM) + n_exp
    blk = jnp.arange(n_blocks, dtype=i32)
    block_expert = jnp.sum((pad_end[None, :] <= (blk * MOE_TM)[:, None]).astype(i32), axis=1)
    block_expert = jnp.minimum(block_expert, n_exp - 1)
    n_active = (pad_end[-1] // MOE_TM).astype(i32)
    first = blk * MOE_TM - lookup(shift, block_expert)
    last = lookup(start + counts, block_expert)
    lane = jnp.arange(MOE_TM, dtype=i32)[None, :]
    pos = first[:, None] + lane
    valid = (pos < last[:, None]) & (blk[:, None] < n_active)
    pos = jnp.clip(pos, 0, n_assign - 1)
    slot_token = jnp.where(valid, (order // top_k)[pos], (blk[:, None] * MOE_TM + lane) % n_tok).reshape(-1)
    slot_gate = jnp.where(valid, gate_sorted[pos], 0.0).reshape(-1, 1)
    last_used = jnp.sum(jnp.where(blk == jnp.maximum(n_active - 1, 0), block_expert, 0))
    block_expert = jnp.where(blk < n_active, block_expert, last_used)
    return slot_token, slot_gate, slot_of_assign, block_expert, n_active.reshape(1)


def _expert_changed(be_ref, m):
    return (m == 0) | (be_ref[m] != be_ref[jnp.maximum(m - 1, 0)])


def _moe_gu_kernel(be_ref, na_ref, x_ref, wg_ref, wu_ref, bg_ref, bu_ref, o_ref, wg_s, wu_s):
    m = pl.program_id(1)

    @pl.when(_expert_changed(be_ref, m))
    def _():
        wg_s[...] = wg_ref[...].astype(BF16)
        wu_s[...] = wu_ref[...].astype(BF16)

    @pl.when(m < na_ref[0])
    def _():
        x = x_ref[...]
        gate = jnp.dot(x, wg_s[...], preferred_element_type=F32) + bg_ref[...]
        up = jnp.dot(x, wu_s[...], preferred_element_type=F32) + bu_ref[...]
        gate = jnp.minimum(gate, SWIGLU_LIMIT)
        up = jnp.clip(up, -SWIGLU_LIMIT, SWIGLU_LIMIT)
        o_ref[...] = ((up + 1.0) * gate * _sigmoid(SWIGLU_ALPHA * gate)).astype(BF16)

    @pl.when(m >= na_ref[0])
    def _():
        o_ref[...] = jnp.zeros_like(o_ref)


def _moe_down_kernel(be_ref, na_ref, a_ref, w_ref, b_ref, sg_ref, o_ref, w_s):
    m = pl.program_id(1)

    @pl.when(_expert_changed(be_ref, m))
    def _():
        w_s[...] = w_ref[...].astype(BF16)

    @pl.when(m < na_ref[0])
    def _():
        o_ref[...] = (jnp.dot(a_ref[...], w_s[...], preferred_element_type=F32) + b_ref[...]) * sg_ref[...]

    @pl.when(m >= na_ref[0])
    def _():
        o_ref[...] = jnp.zeros_like(o_ref)


def _moe_experts(xb, slot_gate, block_expert, n_active, w_gu, b_gu, w_down, b_down, layer):
    n_slots, d = xb.shape
    n_blocks = n_slots // MOE_TM
    n_exp, _, two_f = w_gu.shape[1:]
    f = two_f // 2
    nj = f // GU_TN
    act = pl.pallas_call(
        _moe_gu_kernel,
        grid_spec=pltpu.PrefetchScalarGridSpec(
            num_scalar_prefetch=2,
            grid=(nj, n_blocks),
            in_specs=[pl.BlockSpec((MOE_TM, d), lambda j, m, be, na: (m, 0)),
                      pl.BlockSpec((None, None, d, GU_TN), lambda j, m, be, na: (layer, be[m], 0, j)),
                      pl.BlockSpec((None, None, d, GU_TN), lambda j, m, be, na: (layer, be[m], 0, nj + j)),
                      pl.BlockSpec((None, None, 1, GU_TN), lambda j, m, be, na: (layer, be[m], 0, j)),
                      pl.BlockSpec((None, None, 1, GU_TN), lambda j, m, be, na: (layer, be[m], 0, nj + j))],
            out_specs=pl.BlockSpec((MOE_TM, GU_TN), lambda j, m, be, na: (m, j)),
            scratch_shapes=[pltpu.VMEM((d, GU_TN), BF16), pltpu.VMEM((d, GU_TN), BF16)]),
        out_shape=jax.ShapeDtypeStruct((n_slots, f), BF16),
        compiler_params=_cparams(2),
        name="moe_gate_up",
    )(block_expert, n_active, xb, w_gu, w_gu, b_gu.reshape(b_gu.shape[0], n_exp, 1, two_f),
      b_gu.reshape(b_gu.shape[0], n_exp, 1, two_f))
    nd = d // DOWN_TN
    return pl.pallas_call(
        _moe_down_kernel,
        grid_spec=pltpu.PrefetchScalarGridSpec(
            num_scalar_prefetch=2,
            grid=(nd, n_blocks),
            in_specs=[pl.BlockSpec((MOE_TM, f), lambda j, m, be, na: (m, 0)),
                      pl.BlockSpec((None, None, f, DOWN_TN), lambda j, m, be, na: (layer, be[m], 0, j)),
                      pl.BlockSpec((None, None, 1, DOWN_TN), lambda j, m, be, na: (layer, be[m], 0, j)),
                      pl.BlockSpec((MOE_TM, 1), lambda j, m, be, na: (m, 0))],
            out_specs=pl.BlockSpec((MOE_TM, DOWN_TN), lambda j, m, be, na: (m, j)),
            scratch_shapes=[pltpu.VMEM((f, DOWN_TN), BF16)]),
        out_shape=jax.ShapeDtypeStruct((n_slots, d), F32),
        compiler_params=_cparams(2),
        name="moe_down",
    )(block_expert, n_active, act, w_down, b_down.reshape(b_down.shape[0], n_exp, 1, d), slot_gate)


GATHER_ROWS = 256
COMBINE_TM = 64
DMA_PRIORITIES = 2


def _gather_rows_kernel(idx_ref, nxt_ref, src_ref, o_ref, buf, sems):
    i = pl.program_id(0)
    n = pl.num_programs(0)
    slot = i % 2

    def copy(ids, s, t):
        return pltpu.make_async_copy(src_ref.at[pl.ds(ids[0, t], 1)], buf.at[s, pl.ds(t, 1)], sems.at[s])

    def issue(ids, s):
        def body(g, carry):
            for p in range(DMA_PRIORITIES):
                copy(ids, s, g * DMA_PRIORITIES + p).start(priority=p)
            return carry
        lax.fori_loop(0, GATHER_ROWS // DMA_PRIORITIES, body, 0, unroll=4)

    @pl.when(i == 0)
    def _():
        issue(idx_ref, 0)

    @pl.when(i + 1 < n)
    def _():
        issue(nxt_ref, 1 - slot)

    pltpu.make_async_copy(src_ref.at[pl.ds(0, GATHER_ROWS)], buf.at[slot], sems.at[slot]).wait()
    o_ref[...] = buf[slot]


def _as_row_tiles(a):
    return a.reshape(a.shape[0], a.shape[1] // LANES, LANES)


def _gather_rows(src, idx):
    n = idx.shape[0]
    nb = n // GATHER_ROWS
    ids = idx.reshape(nb, 1, GATHER_ROWS)
    smem_blk = lambda f: pl.BlockSpec((None, 1, GATHER_ROWS), f, memory_space=pltpu.SMEM)
    blk = (GATHER_ROWS,) + src.shape[1:]
    return pl.pallas_call(
        _gather_rows_kernel,
        grid=(nb,),
        in_specs=[smem_blk(lambda i: (i, 0, 0)), smem_blk(lambda i: (jnp.minimum(i + 1, nb - 1), 0, 0)),
                  pl.BlockSpec(memory_space=pl.ANY)],
        out_specs=pl.BlockSpec(blk, lambda i: (i, 0, 0)),
        out_shape=jax.ShapeDtypeStruct((n,) + src.shape[1:], src.dtype),
        scratch_shapes=[pltpu.VMEM((2,) + blk, src.dtype), pltpu.SemaphoreType.DMA((2,))],
        compiler_params=_cparams(1),
        name="moe_dispatch_gather",
    )(ids, ids, src)


def _combine_kernel(idx_ref, nxt_ref, yb_ref, x_ref, g2_ref, o_ref, buf, acc, sems, *, top_k):
    i = pl.program_id(0)
    n = pl.num_programs(0)
    slot = i % 2

    def copy(ids, s, t, k):
        return pltpu.make_async_copy(yb_ref.at[pl.ds(ids[0, t * top_k + k], 1)], buf.at[s, k, pl.ds(t, 1)], sems.at[s])

    def issue(ids, s):
        def body(t, carry):
            for k in range(top_k):
                copy(ids, s, t, k).start(priority=k % DMA_PRIORITIES)
            return carry
        lax.fori_loop(0, COMBINE_TM, body, 0, unroll=2)

    @pl.when(i == 0)
    def _():
        issue(idx_ref, 0)

    @pl.when(i + 1 < n)
    def _():
        issue(nxt_ref, 1 - slot)

    for k in range(top_k):
        pltpu.make_async_copy(yb_ref.at[pl.ds(0, COMBINE_TM)], buf.at[slot, k], sems.at[slot]).wait()

    total = buf[slot, 0]
    for k in range(1, top_k):
        total = total + buf[slot, k]
    acc[...] = total
    for c in range(acc.shape[1]):
        cs = slice(c * LANES, (c + 1) * LANES)
        o_ref[:, cs] = x_ref[:, cs] + g2_ref[:, cs] * acc[:, c, :]


def _moe_combine(yb, slot_of_assign, x, mod, mod_row, top_k):
    m, d = x.shape
    tm = COMBINE_TM
    nt = m // tm
    row = mod_row(tm)
    ids = slot_of_assign.reshape(nt, 1, tm * top_k)
    smem_blk = lambda f: pl.BlockSpec((None, 1, tm * top_k), f, memory_space=pltpu.SMEM)
    return pl.pallas_call(
        functools.partial(_combine_kernel, top_k=top_k),
        grid=(nt,),
        in_specs=[smem_blk(lambda i: (i, 0, 0)), smem_blk(lambda i: (jnp.minimum(i + 1, nt - 1), 0, 0)),
                  pl.BlockSpec(memory_space=pl.ANY),
                  pl.BlockSpec((tm, d), lambda i: (i, 0)),
                  pl.BlockSpec((None, 1, d), lambda i: (row(i), 0, 5))],
        out_specs=pl.BlockSpec((tm, d), lambda i: (i, 0)),
        out_shape=jax.ShapeDtypeStruct((m, d), F32),
        scratch_shapes=[pltpu.VMEM((2, top_k, tm) + yb.shape[1:], F32), pltpu.VMEM((tm,) + yb.shape[1:], F32),
                        pltpu.SemaphoreType.DMA((2,))],
        compiler_params=_cparams(1),
        name="moe_combine",
    )(ids, ids, yb, x, mod)


def _moe(h2, idx, gate, x, mod, mod_row, w_gu, b_gu, w_down, b_down, layer):
    n_exp = w_gu.shape[1]
    slot_token, slot_gate, slot_of_assign, block_expert, n_active = _moe_dispatch(idx[:, :TOP_K], gate[:, :TOP_K], n_exp)
    xb = _gather_rows(_as_row_tiles(h2), slot_token).reshape(slot_token.shape[0], h2.shape[1])
    yb = _moe_experts(xb, slot_gate, block_expert, n_active, w_gu, b_gu, w_down, b_down, layer)
    return _moe_combine(_as_row_tiles(yb), slot_of_assign, x, mod, mod_row, TOP_K)


def _token_shift_mix(xr, mu, groups):
    parts = []
    for row_off, n_seq, seq_len in groups:
        x = xr[row_off:row_off + n_seq * seq_len].reshape(n_seq, seq_len, -1)
        xp = jnp.pad(x, ((0, 0), (1, 1), (0, 0)))
        sh = 0.5 * (xp[:, :-2] + xp[:, 2:])
        parts.append((x + (sh - x) * mu).reshape(n_seq * seq_len, -1))
    return jnp.concatenate(parts, axis=0)


def _pad_w_in(w):
    wb = w.astype(BF16)
    cut = 3 * NA_WIDTH + RW_SHIFT_WIDTH
    return jnp.concatenate([wb[:, :cut], jnp.zeros((w.shape[0], RW_PAD), BF16), wb[:, cut:]], axis=1)


def kernel(x_prompt, x_sample, c, cache_k, cache_v, state_rwkv_fwd, state_rwkv_bwd, state_hgrn_fwd, state_hgrn_bwd, c_ctx, norm1_g, norm2_g, w_ada, b_ada, w_in, w_out, na_rpb, rw_mu, rw_w0, rw_w2, rw_a0, rw_a2, rw_g2, rw_k_k, rw_k_a, rw_r_k, rw_ln_g, rw_ln_b, hg_lb_raw, hg_norm_g, router_w, router_b, exp_w_gu, exp_b_gu, exp_w_down, exp_b_down, final_g):
    n_p, t_p, d = x_prompt.shape
    n_s, t_s, _ = x_sample.shape
    depth = w_in.shape[0]
    mp, ms = n_p * t_p, n_s * t_s
    x = jnp.concatenate([x_prompt.reshape(mp, d), x_sample.reshape(ms, d)], axis=0)
    n_mod_rows = 8
    cond = jnp.concatenate([c_ctx[None], c, jnp.zeros((n_mod_rows - 1 - n_s, d), F32)], axis=0)
    mod_row = lambda tm: _mod_row_fn(tm, mp, t_s)
    p_lb = jax.nn.softmax(hg_lb_raw.astype(F32), axis=0)
    hg_lb = jnp.cumsum(p_lb, axis=0) - p_lb[:1]
    seqs = [(0, n_p, t_p), (mp, n_s, t_s)]
    ctx_out = []
    for l in range(depth):
        mod2d = _ada(cond, w_ada, b_ada, l)
        mod = mod2d.reshape(n_mod_rows, 1, 6 * d)
        proj = _in_proj(x, norm1_g[l], mod, _pad_w_in(w_in[l]), mod_row)
        na_p, k_ctx, v_ctx = _ctx_attention(proj, n_p, t_p)
        na_s = _na_attention(proj, mp, n_s, t_s, cache_k, cache_v, l, _na_bias_table(na_rpb[l], t_s // GRID_W))
        na_o = jnp.concatenate([na_p, na_s], axis=0)
        xr = _token_shift_mix(proj[:, PCOL_RW:PCOL_RW + RW_SHIFT_WIDTH], rw_mu[l], seqs)
        lw = dict(rw_w0=rw_w0[l], rw_w2=rw_w2[l], rw_a0=rw_a0[l], rw_a2=rw_a2[l], rw_g2=rw_g2[l], rw_k_k=rw_k_k[l],
                  rw_k_a=rw_k_a[l], rw_r_k=rw_r_k[l], rw_ln_g=rw_ln_g[l], rw_ln_b=rw_ln_b[l])
        rw_groups = [(0, n_p, t_p, jnp.zeros((2, n_p, RW_HEADS, RW_HEAD_DIM, RW_HEAD_DIM), F32)),
                     (mp, n_s, t_s, jnp.stack([state_rwkv_fwd[:, l], state_rwkv_bwd[:, l]]))]
        rw_o, rw_states = _rwkv_mixer(xr, rw_groups, lw)
        hg_groups = [(0, n_p, t_p, jnp.zeros((2, n_p, HG_HEADS, HG_KEY_DIM, HG_VAL_DIM), F32)),
                     (mp, n_s, t_s, jnp.stack([state_hgrn_fwd[:, l], state_hgrn_bwd[:, l]]))]
        hg_o, hg_states = _hgrn_mixer(proj, hg_groups, hg_lb[l], hg_norm_g[l])
        x, h2, idx, gate = _out_proj(na_o, rw_o, hg_o, x, w_out[l].astype(BF16), mod, norm2_g[l], router_w[l],
                                     router_b[l], mod_row)
        x = _moe(h2, idx, gate, x, mod, mod_row, exp_w_gu, exp_b_gu, exp_w_down, exp_b_down, l)
        ctx_out.append((k_ctx, v_ctx, rw_states[0][0], rw_states[0][1], hg_states[0][0], hg_states[0][1]))
    y = _final_norm(x, final_g)
    outs = [y[:mp].reshape(n_p, t_p, d), y[mp:].reshape(n_s, t_s, d)]
    outs += [jnp.stack([t[i] for t in ctx_out], axis=1) for i in range(6)]
    return tuple(outs)
```

```python
import functools

import numpy as np
import jax
import jax.numpy as jnp
from jax import lax
from jax.experimental import pallas as pl
from jax.experimental.pallas import tpu as pltpu

F32 = jnp.float32
BF16 = jnp.bfloat16

D_MODEL = 2048
GRID_W = 64
NA_HEADS = 16
NA_HEAD_DIM = 64
NA_WIDTH = NA_HEADS * NA_HEAD_DIM
NA_KH = 8
NA_KW = 16
RW_HEADS = 8
RW_HEAD_DIM = 64
RW_WIDTH = RW_HEADS * RW_HEAD_DIM
RW_DECAY_LORA = 64
RW_ICLR_LORA = 64
RW_GATE_LORA = 128
RW_SHIFT_WIDTH = 3 * RW_WIDTH + 2 * RW_DECAY_LORA + 2 * RW_ICLR_LORA + RW_GATE_LORA
RW_GN_EPS = 64e-5
HG_HEADS = 4
HG_KEY_DIM = 128
HG_VAL_DIM = 128
HG_WIDTH = HG_HEADS * HG_VAL_DIM
HG_MIN_F = 1e-6
IN_WIDTH = 3 * NA_WIDTH + RW_SHIFT_WIDTH + 5 * HG_WIDTH
N_EXPERTS = 32
TOP_K = 4
D_EXPERT = 2048
SWIGLU_LIMIT = 7.0
SWIGLU_ALPHA = 1.702
RMS_EPS = 1e-6
NEG_INF = -1e30

LANES = 128
PCOL_RW = 0
RW_PAD = -RW_SHIFT_WIDTH % HG_WIDTH
PCOL_Q = RW_SHIFT_WIDTH + RW_PAD
PCOL_K = PCOL_Q + NA_WIDTH
PCOL_V = PCOL_K + NA_WIDTH
PCOL_HQ = PCOL_V + NA_WIDTH
PCOL_HFF = PCOL_HQ + HG_WIDTH
PCOL_HFB = PCOL_HFF + HG_WIDTH
PCOL_HI = PCOL_HFB + HG_WIDTH
PCOL_HG = PCOL_HI + HG_WIDTH
IN_WIDTH_PAD = PCOL_HG + HG_WIDTH
SCAN_CHUNK = 64
TOKEN_TILE = 256
VMEM_LIMIT = 56 * 1024 * 1024


def _cparams(n_axes, vmem=VMEM_LIMIT):
    return pltpu.CompilerParams(dimension_semantics=("arbitrary",) * n_axes, vmem_limit_bytes=vmem)


def _dot(a, b):
    return jnp.dot(a.astype(BF16), b.astype(BF16), preferred_element_type=F32)


def _dot_nt(a, b):
    return lax.dot_general(a.astype(BF16), b.astype(BF16), (((1,), (1,)), ((), ())), preferred_element_type=F32)


def _dot_tn(a, b):
    return lax.dot_general(a.astype(BF16), b.astype(BF16), (((0,), (0,)), ((), ())), preferred_element_type=F32)


def _split3(x):
    hi = x.astype(BF16)
    r1 = x - hi.astype(F32)
    mid = r1.astype(BF16)
    lo = (r1 - mid.astype(F32)).astype(BF16)
    return hi, mid, lo


def _dot01_left(m01, x):
    hi, mid, lo = _split3(x)
    d = lambda p: jnp.dot(m01, p, preferred_element_type=F32)
    return d(hi) + (d(mid) + d(lo))


def _dot01_right(x, m01):
    hi, mid, lo = _split3(x)
    d = lambda p: jnp.dot(p, m01, preferred_element_type=F32)
    return d(hi) + (d(mid) + d(lo))


def _sigmoid(x):
    return 1.0 / (1.0 + jnp.exp(-x))


def _softplus(x):
    return jnp.maximum(x, 0.0) + jnp.log(1.0 + jnp.exp(-jnp.abs(x)))


def _head_sum_matrix(width, head_dim):
    i = np.arange(width)
    return jnp.asarray((i[:, None] // head_dim) == (i[None, :] // head_dim), dtype=BF16)


def _order_masks(n, reverse):
    row = lax.broadcasted_iota(jnp.int32, (n, n), 0)
    col = lax.broadcasted_iota(jnp.int32, (n, n), 1)
    diff = col - row if reverse else row - col
    return diff >= 0, diff > 0


def _rwkv_prep_kernel(x_ref, prev_ref, next_ref, mu_ref, w0_ref, w2_ref, a0_ref, a2_ref, g2_ref, kkw_ref, kaw_ref,
                      hsum_ref, r_ref, v_ref, kkn_ref, g_ref, lw_ref, kd_ref, av_ref, *, seq_groups):
    tm = x_ref.shape[0]
    t0 = pl.program_id(0) * tm
    (off0, len0), (off1, len1) = seq_groups
    in_first = t0 < off1
    pos = jnp.where(in_first, lax.rem(t0 - off0, len0), lax.rem(jnp.maximum(t0 - off1, 0), len1))
    seq_len = jnp.where(in_first, len0, len1)
    has_prev = (pos != 0).astype(F32)
    has_next = (pos + tm != seq_len).astype(F32)
    x = x_ref[...]
    row = lax.broadcasted_iota(jnp.int32, x.shape, 0)
    before = jnp.where(row == 0, prev_ref[7:8, :] * has_prev, pltpu.roll(x, 1, 0))
    after = jnp.where(row == tm - 1, next_ref[0:1, :] * has_next, pltpu.roll(x, tm - 1, 0))
    x = x + (0.5 * (before + after) - x) * mu_ref[...]
    r_ref[...] = x[:, 0:RW_WIDTH]
    v_ref[...] = x[:, 2 * RW_WIDTH:3 * RW_WIDTH]
    k = x[:, RW_WIDTH:2 * RW_WIDTH]
    o = 3 * RW_WIDTH
    wd = (x[:, o:o + 64], x[:, o + 64:o + 128])
    ad = (x[:, o + 128:o + 192], x[:, o + 192:o + 256])
    gd = x[:, o + 256:o + 384]
    g_ref[...] = _dot(_sigmoid(gd), g2_ref[...])
    kk = k * kkw_ref[...]
    ssq = _dot01_right(kk * kk, hsum_ref[...])
    kkn_ref[...] = kk / jnp.maximum(jnp.sqrt(ssq), 1e-12)
    for d in range(2):
        wl = -_softplus(-(w0_ref[d] + _dot(jnp.tanh(wd[d]), w2_ref[d]))) - 0.5
        lw_ref[d] = -jnp.exp(wl)
        a = _sigmoid(a0_ref[d] + _dot(ad[d], a2_ref[d]))
        kd_ref[d] = k * (1.0 + (a - 1.0) * kaw_ref[...])
        av_ref[d] = a


def _rwkv_scan_kernel(*refs, n_chunks):
    ins = (refs[0:6], refs[6:12])
    s0_ref, yf_ref, yb_ref, sout_ref, s_scr = refs[12:]
    c = pl.program_id(1)
    n = SCAN_CHUNK

    @pl.when(c == 0)
    def _():
        s_scr[...] = s0_ref[...]

    masks = [_order_masks(n, False), _order_masks(n, True)]
    incl01 = [m[0].astype(BF16) for m in masks]
    ch = [(d, h) for d in range(2) for h in range(RW_HEADS)]
    cs = range(len(ch))
    m_incl = [masks[d][0] for d, _ in ch]
    m_strict = [masks[d][1] for d, _ in ch]
    col = lambda j: [ins[d][j][:, h * RW_HEAD_DIM:(h + 1) * RW_HEAD_DIM] for d, h in ch]
    r, v, kkn, lw, kd, av = (col(j) for j in range(6))
    cum = [_dot01_left(incl01[ch[i][0]], lw[i]) for i in cs]
    p_tot = [jnp.exp(jnp.sum(lw[i], axis=0, keepdims=True)) for i in cs]
    p_inv = [jnp.exp(-cum[i]) for i in cs]
    a_t = [-kkn[i] * jnp.exp(cum[i] - lw[i]) for i in cs]
    b_t = [kkn[i] * av[i] * p_inv[i] for i in cs]
    k_t = [kd[i] * p_inv[i] for i in cs]
    r_t = [r[i] * jnp.exp(cum[i]) for i in cs]
    a_ab = [jnp.where(m_strict[i], _dot_nt(a_t[i], b_t[i]), 0.0) for i in cs]
    a_ak = [jnp.where(m_strict[i], _dot_nt(a_t[i], k_t[i]), 0.0) for i in cs]
    a_rb = [jnp.where(m_incl[i], _dot_nt(r_t[i], b_t[i]), 0.0) for i in cs]
    a_rk = [jnp.where(m_incl[i], _dot_nt(r_t[i], k_t[i]), 0.0) for i in cs]
    t_m = list(a_ab)
    n_pow = list(a_ab)
    for _ in range(int(np.log2(n)) - 1):
        n_pow = [_dot(n_pow[i], n_pow[i]) for i in cs]
        t_m = [t_m[i] + n_pow[i] + _dot(t_m[i], n_pow[i]) for i in cs]
    x1 = [_dot(a_ak[i], v[i]) for i in cs]
    w1 = [x1[i] + _dot(t_m[i], x1[i]) for i in cs]
    w2 = [a_t[i] + _dot(t_m[i], a_t[i]) for i in cs]
    y0 = [_dot(a_rk[i], v[i]) + _dot(a_rb[i], w1[i]) for i in cs]
    q = [r_t[i] + _dot(a_rb[i], w2[i]) for i in cs]
    g_m = [_dot_tn(w2[i], b_t[i]) for i in cs]
    h_m = [_dot_tn(w1[i], b_t[i]) + _dot_tn(v[i], k_t[i]) for i in cs]
    s0 = [s_scr[d, h] for d, h in ch]
    ys = [y0[i] + _dot_nt(q[i], s0[i]) for i in cs]
    s_new = [(s0[i] + _dot(s0[i], g_m[i]) + h_m[i]) * p_tot[i] for i in cs]
    for i, (d, h) in enumerate(ch):
        s_scr[d, h] = s_new[i]
    yf_ref[...] = jnp.concatenate(ys[:RW_HEADS], axis=-1)
    yb_ref[...] = jnp.concatenate(ys[RW_HEADS:], axis=-1)

    @pl.when(c == n_chunks - 1)
    def _():
        sout_ref[...] = s_scr[...]


def _rwkv_post_kernel(yf_ref, yb_ref, r_ref, v_ref, kd_ref, g_ref, rk_ref, lng_ref, lnb_ref, hsum_ref, o_ref):
    r = r_ref[...]
    v = v_ref[...]
    y = yf_ref[...] + yb_ref[...]
    hsum = hsum_ref[...]
    inv_n = 1.0 / RW_HEAD_DIM
    mu = _dot01_right(y, hsum) * inv_n
    yc = y - mu
    var = _dot01_right(yc * yc, hsum) * inv_n
    yn = yc * lax.rsqrt(var + RW_GN_EPS) * lng_ref[...] + lnb_ref[...]
    bonus = _dot01_right(r * (0.5 * (kd_ref[0] + kd_ref[1])) * rk_ref[...], hsum) * v
    o_ref[...] = (yn + bonus) * g_ref[...]


def _rwkv_mixer(proj, groups, lw):
    m = proj.shape[0]
    tm = TOKEN_TILE
    sub = 8
    assert PCOL_RW == 0 and len(groups) == 2 and all(g[2] % tm == 0 for g in groups)
    hsum = _head_sum_matrix(RW_WIDTH, RW_HEAD_DIM)
    row = lambda a: a.reshape(1, -1)
    full = lambda shape: pl.BlockSpec(shape, lambda i: (0,) * len(shape))
    tok = lambda w: pl.BlockSpec((tm, w), lambda i: (i, 0))
    tok2 = pl.BlockSpec((2, tm, RW_WIDTH), lambda i: (0, i, 0))
    d2 = jax.ShapeDtypeStruct((2, m, RW_WIDTH), F32)
    d1 = jax.ShapeDtypeStruct((m, RW_WIDTH), F32)
    per = tm // sub
    prev_blk = pl.BlockSpec((sub, RW_SHIFT_WIDTH), lambda i: (jnp.maximum(i * per - 1, 0), 0))
    next_blk = pl.BlockSpec((sub, RW_SHIFT_WIDTH), lambda i: (jnp.minimum((i + 1) * per, m // sub - 1), 0))
    xs_r, xs_v, kkn, g, lwd, kd, av = pl.pallas_call(
        functools.partial(_rwkv_prep_kernel, seq_groups=tuple((g[0], g[2]) for g in groups)),
        grid=(m // tm,),
        in_specs=[tok(RW_SHIFT_WIDTH), prev_blk, next_blk, full((1, RW_SHIFT_WIDTH)),
                  full((2, 1, RW_WIDTH)), full((2, RW_DECAY_LORA, RW_WIDTH)),
                  full((2, 1, RW_WIDTH)), full((2, RW_ICLR_LORA, RW_WIDTH)), full((RW_GATE_LORA, RW_WIDTH)),
                  full((1, RW_WIDTH)), full((1, RW_WIDTH)), full((RW_WIDTH, RW_WIDTH))],
        out_specs=[tok(RW_WIDTH), tok(RW_WIDTH), tok(RW_WIDTH), tok(RW_WIDTH), tok2, tok2, tok2],
        out_shape=[d1, d1, d1, d1, d2, d2, d2],
        compiler_params=_cparams(1),
        name="rwkv_prep",
    )(proj, proj, proj, row(lw['rw_mu']), lw['rw_w0'].reshape(2, 1, RW_WIDTH), lw['rw_w2'],
      lw['rw_a0'].reshape(2, 1, RW_WIDTH), lw['rw_a2'], lw['rw_g2'], row(lw['rw_k_k']), row(lw['rw_k_a']), hsum)

    n = SCAN_CHUNK
    yfs, ybs, states = [], [], []
    for row_off, n_seq, seq_len, s0 in groups:
        nc = seq_len // n
        base = row_off // n
        chunk = lambda b, c, d, nc=nc: b * nc + (nc - 1 - c if d else c)
        x_spec = lambda d, col, base=base, chunk=chunk: pl.BlockSpec(
            (n, RW_WIDTH), lambda b, c: (base + chunk(b, c, d), col))
        d_spec = lambda d, base=base, chunk=chunk: pl.BlockSpec(
            (None, n, RW_WIDTH), lambda b, c: (d, base + chunk(b, c, d), 0))
        y_spec = lambda d, chunk=chunk: pl.BlockSpec((n, RW_WIDTH), lambda b, c: (chunk(b, c, d), 0))
        s_spec = pl.BlockSpec((2, None, RW_HEADS, RW_HEAD_DIM, RW_HEAD_DIM), lambda b, c: (0, b, 0, 0, 0))
        per_dir = lambda d: [x_spec(d, 0), x_spec(d, 0), x_spec(d, 0), d_spec(d), d_spec(d), d_spec(d)]
        y_shape = jax.ShapeDtypeStruct((n_seq * seq_len, RW_WIDTH), F32)
        yf, yb, s_out = pl.pallas_call(
            functools.partial(_rwkv_scan_kernel, n_chunks=nc),
            grid=(n_seq, nc),
            in_specs=per_dir(0) + per_dir(1) + [s_spec],
            out_specs=[y_spec(0), y_spec(1), s_spec],
            out_shape=[y_shape, y_shape,
                       jax.ShapeDtypeStruct((2, n_seq, RW_HEADS, RW_HEAD_DIM, RW_HEAD_DIM), F32)],
            scratch_shapes=[pltpu.VMEM((2, RW_HEADS, RW_HEAD_DIM, RW_HEAD_DIM), F32)],
            compiler_params=_cparams(2),
            name="rwkv_scan",
        )(*([xs_r, xs_v, kkn, lwd, kd, av] * 2), s0)
        yfs.append(yf)
        ybs.append(yb)
        states.append(s_out)
    yf = jnp.concatenate(yfs, axis=0)
    yb = jnp.concatenate(ybs, axis=0)

    out = pl.pallas_call(
        _rwkv_post_kernel,
        grid=(m // tm,),
        in_specs=[tok(RW_WIDTH), tok(RW_WIDTH), tok(RW_WIDTH), tok(RW_WIDTH), tok2, tok(RW_WIDTH),
                  full((1, RW_WIDTH)), full((1, RW_WIDTH)), full((1, RW_WIDTH)), full((RW_WIDTH, RW_WIDTH))],
        out_specs=tok(RW_WIDTH),
        out_shape=d1,
        compiler_params=_cparams(1),
        name="rwkv_post",
    )(yf, yb, xs_r, xs_v, kd, g, row(lw['rw_r_k']), row(lw['rw_ln_g']), row(lw['rw_ln_b']), hsum)
    return out, states


HG_LEVELS = (32, 16, 8, 4, 2, 1)


def _hgrn_consts():
    n = SCAN_CHUNK
    cums, masks = [], []
    for rev in (False, True):
        p = np.arange(n)[::-1] if rev else np.arange(n)
        pt, pj = p[:, None], p[None, :]
        rows = [pj <= pt, pj > pt]
        rows += [((pt // m) % 2 == 1) & (pj // m == pt // m) & (pj <= pt) for m in HG_LEVELS]
        rows += [((pt // m) % 2 == 0) & (pj // m == pt // m) & (pj > pt) for m in HG_LEVELS]
        cums.append(np.concatenate(rows, 0))
        mk = [pt == pj]
        mk += [(pt // (2 * m) == pj // (2 * m)) & ((pt // m) % 2 == 1) & ((pj // m) % 2 == 0) for m in HG_LEVELS]
        masks.append(np.stack(mk))
    return jnp.asarray(np.stack(cums), BF16), jnp.asarray(np.stack(masks), F32)


def _hgrn_scan_kernel(qf_ref, zf_ref, vf_ref, qb_ref, zb_ref, vb_ref, lb_ref, cm_ref, mask_ref, s0_ref,
                      of_ref, ob_ref, sout_ref, s_scr, *, n_chunks):
    c = pl.program_id(1)
    n = SCAN_CHUNK
    nl = len(HG_LEVELS)
    ins = ((qf_ref, zf_ref, vf_ref), (qb_ref, zb_ref, vb_ref))

    @pl.when(c == 0)
    def _():
        s_scr[...] = s0_ref[...]

    ones = jnp.ones((n, HG_VAL_DIM), BF16)
    tn01 = lambda p: lax.dot_general(p, ones, (((0,), (0,)), ((), ())), preferred_element_type=F32)
    ch = [(d, h) for d in range(2) for h in range(HG_HEADS)]
    cs = range(len(ch))
    col = lambda j: [ins[d][j][:, h * HG_KEY_DIM:(h + 1) * HG_KEY_DIM] for d, h in ch]
    hq, z, v = col(0), col(1), col(2)
    lb = [lb_ref[d, :, h * HG_KEY_DIM:(h + 1) * HG_KEY_DIM] for d, h in ch]
    q = [hq[i] * _sigmoid(hq[i]) for i in cs]
    lf = [jnp.log(jnp.maximum(lb[i] + (1.0 - lb[i]) * _sigmoid(z[i]), HG_MIN_F)) for i in cs]
    k = [(1.0 - lb[i]) * _sigmoid(-z[i]) for i in cs]
    parts = [_split3(lf[i]) for i in cs]
    ex = [jnp.exp(sum(jnp.dot(cm_ref[ch[i][0]], p, preferred_element_type=F32) for p in parts[i][::-1])) for i in cs]
    p_col = [jnp.exp(sum(tn01(p) for p in parts[i][::-1])) for i in cs]
    att = [mask_ref[ch[i][0], 0] * _dot_nt(q[i], k[i]) for i in cs]
    for lv in range(nl):
        lo_q, lo_k = (2 + lv) * n, (2 + nl + lv) * n
        lvl = [_dot_nt(q[i] * ex[i][lo_q:lo_q + n], k[i] * ex[i][lo_k:lo_k + n]) for i in cs]
        att = [att[i] + mask_ref[ch[i][0], lv + 1] * lvl[i] for i in cs]
    s0 = [s_scr[d, h] for d, h in ch]
    outs = [_dot(q[i] * ex[i][0:n], s0[i]) + _dot(att[i], v[i]) for i in cs]
    s_new = [s0[i] * p_col[i] + _dot_tn(k[i] * ex[i][n:2 * n], v[i]) for i in cs]
    for i, (d, h) in enumerate(ch):
        s_scr[d, h] = s_new[i]
    of_ref[...] = jnp.concatenate(outs[:HG_HEADS], axis=-1)
    ob_ref[...] = jnp.concatenate(outs[HG_HEADS:], axis=-1)

    @pl.when(c == n_chunks - 1)
    def _():
        sout_ref[...] = s_scr[...]


def _hgrn_post_kernel(of_ref, ob_ref, g_ref, gain_ref, hsum_ref, out_ref):
    o = of_ref[...] + ob_ref[...]
    ms = _dot01_right(o * o, hsum_ref[...]) * (1.0 / HG_VAL_DIM)
    on = o * lax.rsqrt(ms + RMS_EPS) * gain_ref[...]
    g = g_ref[...]
    out_ref[...] = on * (g * _sigmoid(g))


def _hgrn_mixer(proj, groups, lb, gain):
    m = proj.shape[0]
    n = SCAN_CHUNK
    cm, masks = _hgrn_consts()
    cq, cff, ci, cg = (col // HG_WIDTH for col in (PCOL_HQ, PCOL_HFF, PCOL_HI, PCOL_HG))
    ofs, obs, states = [], [], []
    for row_off, n_seq, seq_len, s0 in groups:
        nc = seq_len // n
        base = row_off // n
        chunk = lambda b, c, d, nc=nc: b * nc + (nc - 1 - c if d else c)
        x_spec = lambda d, col, base=base, chunk=chunk: pl.BlockSpec(
            (n, HG_WIDTH), lambda b, c: (base + chunk(b, c, d), col))
        o_spec = lambda d, chunk=chunk: pl.BlockSpec((n, HG_WIDTH), lambda b, c: (chunk(b, c, d), 0))
        s_spec = pl.BlockSpec((2, None, HG_HEADS, HG_KEY_DIM, HG_VAL_DIM), lambda b, c: (0, b, 0, 0, 0))
        whole = lambda a: pl.BlockSpec(a.shape, lambda b, c: (0,) * a.ndim)
        lb3 = lb.reshape(2, 1, HG_WIDTH)
        o_shape = jax.ShapeDtypeStruct((n_seq * seq_len, HG_WIDTH), F32)
        of, ob, s_out = pl.pallas_call(
            functools.partial(_hgrn_scan_kernel, n_chunks=nc),
            grid=(n_seq, nc),
            in_specs=[x_spec(0, cq), x_spec(0, cff), x_spec(0, ci), x_spec(1, cq), x_spec(1, cff + 1), x_spec(1, ci),
                      whole(lb3), whole(cm), whole(masks), s_spec],
            out_specs=[o_spec(0), o_spec(1), s_spec],
            out_shape=[o_shape, o_shape,
                       jax.ShapeDtypeStruct((2, n_seq, HG_HEADS, HG_KEY_DIM, HG_VAL_DIM), F32)],
            scratch_shapes=[pltpu.VMEM((2, HG_HEADS, HG_KEY_DIM, HG_VAL_DIM), F32)],
            compiler_params=_cparams(2),
            name="hgrn_scan",
        )(proj, proj, proj, proj, proj, proj, lb3, cm, masks, s0)
        ofs.append(of)
        obs.append(ob)
        states.append(s_out)
    of = jnp.concatenate(ofs, axis=0)
    ob = jnp.concatenate(obs, axis=0)

    tm = TOKEN_TILE
    tok = pl.BlockSpec((tm, HG_WIDTH), lambda i: (i, 0))
    out = pl.pallas_call(
        _hgrn_post_kernel,
        grid=(m // tm,),
        in_specs=[tok, tok,
                  pl.BlockSpec((tm, HG_WIDTH), lambda i: (i, cg)),
                  pl.BlockSpec((1, HG_WIDTH), lambda i: (0, 0)),
                  pl.BlockSpec((HG_WIDTH, HG_WIDTH), lambda i: (0, 0))],
        out_specs=pl.BlockSpec((tm, HG_WIDTH), lambda i: (i, 0)),
        out_shape=jax.ShapeDtypeStruct((m, HG_WIDTH), F32),
        compiler_params=_cparams(1),
        name="hgrn_post",
    )(of, ob, proj, gain.reshape(1, HG_WIDTH), _head_sum_matrix(HG_WIDTH, HG_VAL_DIM))
    return out, states


NA_HEADS_PER_STEP = LANES // NA_HEAD_DIM
NA_ROWS_PER_ITER = 4


def _ctx_attn_kernel(q_ref, k_ref, v_ref, o_ref, kc_ref, vc_ref):
    scale = NA_HEAD_DIM ** -0.5
    hs = range(NA_HEADS_PER_STEP)
    sl = [slice(h * NA_HEAD_DIM, (h + 1) * NA_HEAD_DIM) for h in hs]
    k = [k_ref[:, s] for s in sl]
    v = [v_ref[:, s] for s in sl]
    for h in hs:
        kc_ref[h] = k[h]
        vc_ref[h] = v[h]
    s = [_dot_nt(q_ref[:, sl[h]], k[h]) * scale for h in hs]
    p = [jnp.exp(s[h] - jnp.max(s[h], axis=-1, keepdims=True)) for h in hs]
    o = [_dot(p[h], v[h]) / jnp.sum(p[h], axis=-1, keepdims=True) for h in hs]
    o_ref[...] = jnp.concatenate(o, axis=-1)


def _ctx_attention(proj, n_seq, seq_len):
    nb = NA_WIDTH // LANES
    blk = lambda off: pl.BlockSpec((seq_len, LANES), lambda b, j: (b, off + j))
    c_spec = pl.BlockSpec((None, NA_HEADS_PER_STEP, seq_len, NA_HEAD_DIM), lambda b, j: (b, j, 0, 0))
    c_shape = jax.ShapeDtypeStruct((n_seq, NA_HEADS, seq_len, NA_HEAD_DIM), F32)
    return pl.pallas_call(
        _ctx_attn_kernel,
        grid=(n_seq, nb),
        in_specs=[blk(PCOL_Q // LANES), blk(PCOL_K // LANES), blk(PCOL_V // LANES)],
        out_specs=[pl.BlockSpec((seq_len, LANES), lambda b, j: (b, j)), c_spec, c_spec],
        out_shape=[jax.ShapeDtypeStruct((n_seq * seq_len, NA_WIDTH), F32), c_shape, c_shape],
        compiler_params=_cparams(2),
        name="ctx_attention",
    )(proj, proj, proj)


def _na_bias_table(rpb, rows):
    assert rows >= NA_KH
    o = np.arange(NA_KH)[:, None]
    i = np.arange(NA_KH)[None, :]
    dr = i - o + (NA_KH - 1)
    cq = np.arange(GRID_W)
    ck = np.arange(GRID_W)
    col_start = np.clip(cq - NA_KW // 2, 0, GRID_W - NA_KW)
    col_mask = (ck[None, :] >= col_start[:, None]) & (ck[None, :] < col_start[:, None] + NA_KW)
    dc = np.clip(ck[None, :] - cq[:, None], -(NA_KW - 1), NA_KW - 1) + (NA_KW - 1)
    onehot = jnp.asarray(dc.reshape(-1)[:, None] == np.arange(2 * NA_KW - 1)[None, :], F32)
    full = jnp.einsum('hrj,pj->hrp', rpb.astype(F32), onehot, precision=lax.Precision.HIGHEST)
    full = jnp.where(col_mask.reshape(-1)[None, None, :], full, NEG_INF).reshape(rpb.shape[0], -1, GRID_W, GRID_W)
    lo = lambda off: int(dr[off, 0])
    tab = jnp.stack([full[:, lo(off):lo(off) + NA_KH] for off in range(NA_KH)])
    return tab.transpose(0, 1, 3, 2, 4).reshape(NA_KH, rpb.shape[0], GRID_W, NA_KH * GRID_W)


def _na_kernel(q_ref, k_ref, v_ref, ck_ref, cv_ref, bias_ref, o_ref, *, rows):
    scale = NA_HEAD_DIM ** -0.5
    n_win = NA_KH * GRID_W

    def row_group(g, carry):
        ch = []
        for i in range(NA_ROWS_PER_ITER):
            r = g * NA_ROWS_PER_ITER + i
            r0 = jnp.clip(r - NA_KH // 2, 0, rows - NA_KH)
            q_at = pl.multiple_of(r * GRID_W, GRID_W)
            k_at = pl.multiple_of(r0 * GRID_W, GRID_W)
            for h in range(NA_HEADS_PER_STEP):
                ch.append((q_at, k_at, r - r0, h, slice(h * NA_HEAD_DIM, (h + 1) * NA_HEAD_DIM)))
        q = [q_ref[pl.ds(q_at, GRID_W), sl] for q_at, _, _, _, sl in ch]
        s_w = [_dot_nt(q[i], k_ref[pl.ds(k_at, n_win), sl]) * scale + bias_ref[off, h]
               for i, (_, k_at, off, h, sl) in enumerate(ch)]
        s_c = [_dot_nt(q[i], ck_ref[h]) * scale for i, (_, _, _, h, _) in enumerate(ch)]
        m = [jnp.maximum(jnp.max(a, axis=-1, keepdims=True), jnp.max(b, axis=-1, keepdims=True))
             for a, b in zip(s_w, s_c)]
        p_w = [jnp.exp(a - mm) for a, mm in zip(s_w, m)]
        p_c = [jnp.exp(b - mm) for b, mm in zip(s_c, m)]
        den = [jnp.sum(a, axis=-1, keepdims=True) + jnp.sum(b, axis=-1, keepdims=True) for a, b in zip(p_w, p_c)]
        o = [(_dot(p_w[i], v_ref[pl.ds(k_at, n_win), sl]) + _dot(p_c[i], cv_ref[h])) / den[i]
             for i, (_, k_at, _, h, sl) in enumerate(ch)]
        for i, (q_at, _, _, _, sl) in enumerate(ch):
            o_ref[pl.ds(q_at, GRID_W), sl] = o[i]
        return carry

    lax.fori_loop(0, rows // NA_ROWS_PER_ITER, row_group, 0)


def _na_attention(proj, row_off, n_seq, seq_len, cache_k, cache_v, layer, bias):
    assert row_off % seq_len == 0
    nb = NA_WIDTH // LANES
    rows = seq_len // GRID_W
    assert rows % NA_ROWS_PER_ITER == 0
    rb = row_off // seq_len
    past = cache_k.shape[3]
    blk = lambda off: pl.BlockSpec((seq_len, LANES), lambda b, j: (rb + b, off + j))
    c_spec = pl.BlockSpec((None, None, NA_HEADS_PER_STEP, past, NA_HEAD_DIM), lambda b, j: (b, layer, j, 0, 0))
    return pl.pallas_call(
        functools.partial(_na_kernel, rows=rows),
        grid=(n_seq, nb),
        in_specs=[blk(PCOL_Q // LANES), blk(PCOL_K // LANES), blk(PCOL_V // LANES), c_spec, c_spec,
                  pl.BlockSpec((NA_KH, NA_HEADS_PER_STEP, GRID_W, NA_KH * GRID_W), lambda b, j: (0, j, 0, 0))],
        out_specs=pl.BlockSpec((seq_len, LANES), lambda b, j: (b, j)),
        out_shape=jax.ShapeDtypeStruct((n_seq * seq_len, NA_WIDTH), F32),
        compiler_params=_cparams(2),
        name="na_attention",
    )(proj, proj, proj, cache_k, cache_v, bias)


ADA_TN = 1024
IN_TM = 1024
IN_TN = 1536
OUT_TM = 512


def _mod_row_fn(tm, n_ctx_rows, latent_len):
    def f(i):
        start = i * tm
        return jnp.where(start < n_ctx_rows, 0, 1 + (start - n_ctx_rows) // latent_len)
    return f


def _ada_kernel(c_ref, w_ref, b_ref, o_ref):
    x = c_ref[...]
    o_ref[...] = _dot(x * _sigmoid(x), w_ref[...]) + b_ref[...]


def _ada(cond, w_ada, b_ada, layer):
    n_rows, d = cond.shape
    n = w_ada.shape[2]
    return pl.pallas_call(
        _ada_kernel,
        grid=(n // ADA_TN,),
        in_specs=[pl.BlockSpec((n_rows, d), lambda j: (0, 0)),
                  pl.BlockSpec((None, d, ADA_TN), lambda j: (layer, 0, j)),
                  pl.BlockSpec((None, 1, ADA_TN), lambda j: (layer, 0, j))],
        out_specs=pl.BlockSpec((n_rows, ADA_TN), lambda j: (0, j)),
        out_shape=jax.ShapeDtypeStruct((n_rows, n), F32),
        compiler_params=_cparams(1),
        name="ada_mod",
    )(cond, w_ada, b_ada.reshape(b_ada.shape[0], 1, n))


def _rms_mod(x, g, sc, sh):
    y = x * lax.rsqrt(jnp.mean(x * x, axis=-1, keepdims=True) + RMS_EPS) * g
    return y * (1.0 + sc) + sh


def _in_proj_kernel(x_ref, g_ref, sh_ref, sc_ref, w_ref, o_ref, h_scr):
    @pl.when(pl.program_id(1) == 0)
    def _():
        h_scr[...] = _rms_mod(x_ref[...], g_ref[...], sc_ref[...], sh_ref[...]).astype(BF16)

    o_ref[...] = jnp.dot(h_scr[...], w_ref[...], preferred_element_type=F32)


def _in_proj(x, norm_g, mod, w, mod_row):
    m, d = x.shape
    n = w.shape[1]
    row = mod_row(IN_TM)
    return pl.pallas_call(
        _in_proj_kernel,
        grid=(m // IN_TM, n // IN_TN),
        in_specs=[pl.BlockSpec((IN_TM, d), lambda i, j: (i, 0)),
                  pl.BlockSpec((1, d), lambda i, j: (0, 0)),
                  pl.BlockSpec((None, 1, d), lambda i, j: (row(i), 0, 0)),
                  pl.BlockSpec((None, 1, d), lambda i, j: (row(i), 0, 1)),
                  pl.BlockSpec((d, IN_TN), lambda i, j: (0, j))],
        out_specs=pl.BlockSpec((IN_TM, IN_TN), lambda i, j: (i, j)),
        out_shape=jax.ShapeDtypeStruct((m, n), F32),
        scratch_shapes=[pltpu.VMEM((IN_TM, d), BF16)],
        compiler_params=_cparams(2),
        name="in_proj",
    )(x, norm_g.reshape(1, d), mod, mod, w)


def _out_proj_kernel(na_ref, rw_ref, hg_ref, x_ref, w_ref, g1_ref, n2_ref, sh2_ref, sc2_ref, rtw_ref, rtb_ref,
                     xo_ref, h2_ref, idx_ref, gate_ref):
    o1, o2 = NA_WIDTH, NA_WIDTH + RW_WIDTH
    mix = _dot(na_ref[...], w_ref[0:o1]) + _dot(rw_ref[...], w_ref[o1:o2]) + _dot(hg_ref[...], w_ref[o2:])
    x = x_ref[...] + g1_ref[...] * mix
    xo_ref[...] = x
    h = _rms_mod(x, n2_ref[...], sc2_ref[...], sh2_ref[...])
    h_hi = h.astype(BF16)
    h2_ref[...] = h_hi
    h_mid = (h - h_hi.astype(F32)).astype(BF16)
    d = lambda a, b: jnp.dot(a, b, preferred_element_type=F32)
    logits = d(h_hi, rtw_ref[0]) + (d(h_hi, rtw_ref[1]) + d(h_mid, rtw_ref[0])) + rtb_ref[...]
    lane = lax.broadcasted_iota(jnp.int32, logits.shape, 1).astype(F32)
    idx = jnp.zeros_like(logits)
    vals = []
    for k in range(TOP_K):
        mx = jnp.max(logits, axis=-1, keepdims=True)
        sel = jnp.min(jnp.where(logits == mx, lane, float(LANES)), axis=-1, keepdims=True)
        idx = jnp.where(lane == k, sel, idx)
        vals.append(mx)
        logits = jnp.where(lane == sel, 3.0 * NEG_INF, logits)
    es = [jnp.exp(v - vals[0]) for v in vals]
    den = es[0] + es[1] + es[2] + es[3]
    gate = jnp.zeros_like(idx)
    for k in range(TOP_K):
        gate = jnp.where(lane == k, es[k] / den, gate)
    idx_ref[...] = idx.astype(jnp.int32)
    gate_ref[...] = gate


def _out_proj(na_o, rw_o, hg_o, x, w_out, mod, norm2_g, router_w, router_b, mod_row):
    m, d = x.shape
    tm = OUT_TM
    row = mod_row(tm)
    n_exp = router_w.shape[1]
    rtw = jnp.pad(router_w.astype(F32), ((0, 0), (0, LANES - n_exp)))
    rtw_hi = rtw.astype(BF16)
    rtw = jnp.stack([rtw_hi, (rtw - rtw_hi.astype(F32)).astype(BF16)])
    rtb = jnp.pad(router_b, (0, LANES - n_exp), constant_values=NEG_INF).reshape(1, LANES)
    tok = lambda w: pl.BlockSpec((tm, w), lambda i: (i, 0))
    mod_blk = lambda part: pl.BlockSpec((None, 1, d), lambda i: (row(i), 0, part))
    return pl.pallas_call(
        _out_proj_kernel,
        grid=(m // tm,),
        in_specs=[tok(NA_WIDTH), tok(RW_WIDTH), tok(HG_WIDTH), tok(d),
                  pl.BlockSpec((d, d), lambda i: (0, 0)),
                  mod_blk(2), pl.BlockSpec((1, d), lambda i: (0, 0)), mod_blk(3), mod_blk(4),
                  pl.BlockSpec((2, d, LANES), lambda i: (0, 0, 0)), pl.BlockSpec((1, LANES), lambda i: (0, 0))],
        out_specs=[tok(d), tok(d), tok(LANES), tok(LANES)],
        out_shape=[jax.ShapeDtypeStruct((m, d), F32), jax.ShapeDtypeStruct((m, d), BF16),
                   jax.ShapeDtypeStruct((m, LANES), jnp.int32), jax.ShapeDtypeStruct((m, LANES), F32)],
        compiler_params=_cparams(1),
        name="out_proj_router",
    )(na_o, rw_o, hg_o, x, w_out, mod, norm2_g.reshape(1, d), mod, mod, rtw, rtb)


def _final_norm_kernel(x_ref, g_ref, o_ref):
    x = x_ref[...]
    o_ref[...] = x * lax.rsqrt(jnp.mean(x * x, axis=-1, keepdims=True) + RMS_EPS) * g_ref[...]


def _final_norm(x, g):
    m, d = x.shape
    tm = TOKEN_TILE
    return pl.pallas_call(
        _final_norm_kernel,
        grid=(m // tm,),
        in_specs=[pl.BlockSpec((tm, d), lambda i: (i, 0)), pl.BlockSpec((1, d), lambda i: (0, 0))],
        out_specs=pl.BlockSpec((tm, d), lambda i: (i, 0)),
        out_shape=jax.ShapeDtypeStruct((m, d), F32),
        compiler_params=_cparams(1),
        name="final_norm",
    )(x, g.reshape(1, d))


MOE_TM = 512
GU_TN = 1024
DOWN_TN = 1024


def _moe_dispatch(top_e, gates, n_exp):
    n_tok, top_k = top_e.shape
    n_assign = n_tok * top_k
    i32 = jnp.int32
    flat_e = top_e.reshape(-1).astype(i32)
    assign = jnp.arange(n_assign, dtype=i32)
    experts = jnp.arange(n_exp, dtype=i32)
    lookup = lambda table, e: jnp.sum(jnp.where(e[:, None] == experts[None, :], table[None, :], 0), axis=1)
    sorted_e, order, gate_sorted = lax.sort((flat_e, assign, gates.reshape(-1)), num_keys=1)
    counts = jnp.sum((flat_e[:, None] == experts[None, :]).astype(i32), axis=0)
    padded = (counts + MOE_TM - 1) // MOE_TM * MOE_TM
    pad_end = jnp.cumsum(padded)
    pad_start = pad_end - padded
    start = jnp.cumsum(counts) - counts
    shift = pad_start - start
    slot_sorted = assign + lookup(shift, sorted_e)
    _, slot_of_assign = lax.sort((order, slot_sorted), num_keys=1)
    n_blocks = -(-n_assign // MOE_TM) + n_exp
    blk = jnp.arange(n_blocks, dtype=i32)
    block_expert = jnp.sum((pad_end[None, :] <= (blk * MOE_TM)[:, None]).astype(i32), axis=1)
    block_expert = jnp.minimum(block_expert, n_exp - 1)
    n_active = (pad_end[-1] // MOE_TM).astype(i32)
    first = blk * MOE_TM - lookup(shift, block_expert)
    last = lookup(start + counts, block_expert)
    lane = jnp.arange(MOE_TM, dtype=i32)[None, :]
    pos = first[:, None] + lane
    valid = (pos < last[:, None]) & (blk[:, None] < n_active)
    pos = jnp.clip(pos, 0, n_assign - 1)
    slot_token = jnp.where(valid, (order // top_k)[pos], (blk[:, None] * MOE_TM + lane) % n_tok).reshape(-1)
    slot_gate = jnp.where(valid, gate_sorted[pos], 0.0).reshape(-1, 1)
    last_used = jnp.sum(jnp.where(blk == jnp.maximum(n_active - 1, 0), block_expert, 0))
    block_expert = jnp.where(blk < n_active, block_expert, last_used)
    return slot_token, slot_gate, slot_of_assign, block_expert, n_active.reshape(1)


def _expert_changed(be_ref, m):
    return (m == 0) | (be_ref[m] != be_ref[jnp.maximum(m - 1, 0)])


def _moe_gu_kernel(be_ref, na_ref, x_ref, wg_ref, wu_ref, bg_ref, bu_ref, o_ref, wg_s, wu_s):
    m = pl.program_id(1)

    @pl.when(_expert_changed(be_ref, m))
    def _():
        wg_s[...] = wg_ref[...].astype(BF16)
        wu_s[...] = wu_ref[...].astype(BF16)

    @pl.when(m < na_ref[0])
    def _():
        x = x_ref[...]
        gate = jnp.dot(x, wg_s[...], preferred_element_type=F32) + bg_ref[...]
        up = jnp.dot(x, wu_s[...], preferred_element_type=F32) + bu_ref[...]
        gate = jnp.minimum(gate, SWIGLU_LIMIT)
        up = jnp.clip(up, -SWIGLU_LIMIT, SWIGLU_LIMIT)
        o_ref[...] = ((up + 1.0) * gate * _sigmoid(SWIGLU_ALPHA * gate)).astype(BF16)

    @pl.when(m >= na_ref[0])
    def _():
        o_ref[...] = jnp.zeros_like(o_ref)


def _moe_down_kernel(be_ref, na_ref, a_ref, w_ref, b_ref, sg_ref, o_ref, w_s):
    m = pl.program_id(1)

    @pl.when(_expert_changed(be_ref, m))
    def _():
        w_s[...] = w_ref[...].astype(BF16)

    @pl.when(m < na_ref[0])
    def _():
        o_ref[...] = (jnp.dot(a_ref[...], w_s[...], preferred_element_type=F32) + b_ref[...]) * sg_ref[...]

    @pl.when(m >= na_ref[0])
    def _():
        o_ref[...] = jnp.zeros_like(o_ref)


def _moe_experts(xb, slot_gate, block_expert, n_active, w_gu, b_gu, w_down, b_down, layer):
    n_slots, d = xb.shape
    n_blocks = n_slots // MOE_TM
    n_exp, _, two_f = w_gu.shape[1:]
    f = two_f // 2
    nj = f // GU_TN
    act = pl.pallas_call(
        _moe_gu_kernel,
        grid_spec=pltpu.PrefetchScalarGridSpec(
            num_scalar_prefetch=2,
            grid=(nj, n_blocks),
            in_specs=[pl.BlockSpec((MOE_TM, d), lambda j, m, be, na: (m, 0)),
                      pl.BlockSpec((None, None, d, GU_TN), lambda j, m, be, na: (layer, be[m], 0, j)),
                      pl.BlockSpec((None, None, d, GU_TN), lambda j, m, be, na: (layer, be[m], 0, nj + j)),
                      pl.BlockSpec((None, None, 1, GU_TN), lambda j, m, be, na: (layer, be[m], 0, j)),
                      pl.BlockSpec((None, None, 1, GU_TN), lambda j, m, be, na: (layer, be[m], 0, nj + j))],
            out_specs=pl.BlockSpec((MOE_TM, GU_TN), lambda j, m, be, na: (m, j)),
            scratch_shapes=[pltpu.VMEM((d, GU_TN), BF16), pltpu.VMEM((d, GU_TN), BF16)]),
        out_shape=jax.ShapeDtypeStruct((n_slots, f), BF16),
        compiler_params=_cparams(2),
        name="moe_gate_up",
    )(block_expert, n_active, xb, w_gu, w_gu, b_gu.reshape(b_gu.shape[0], n_exp, 1, two_f),
      b_gu.reshape(b_gu.shape[0], n_exp, 1, two_f))
    nd = d // DOWN_TN
    return pl.pallas_call(
        _moe_down_kernel,
        grid_spec=pltpu.PrefetchScalarGridSpec(
            num_scalar_prefetch=2,
            grid=(nd, n_blocks),
            in_specs=[pl.BlockSpec((MOE_TM, f), lambda j, m, be, na: (m, 0)),
                      pl.BlockSpec((None, None, f, DOWN_TN), lambda j, m, be, na: (layer, be[m], 0, j)),
                      pl.BlockSpec((None, None, 1, DOWN_TN), lambda j, m, be, na: (layer, be[m], 0, j)),
                      pl.BlockSpec((MOE_TM, 1), lambda j, m, be, na: (m, 0))],
            out_specs=pl.BlockSpec((MOE_TM, DOWN_TN), lambda j, m, be, na: (m, j)),
            scratch_shapes=[pltpu.VMEM((f, DOWN_TN), BF16)]),
        out_shape=jax.ShapeDtypeStruct((n_slots, d), F32),
        compiler_params=_cparams(2),
        name="moe_down",
    )(block_expert, n_active, act, w_down, b_down.reshape(b_down.shape[0], n_exp, 1, d), slot_gate)


GATHER_ROWS = 256
COMBINE_TM = 64
DMA_PRIORITIES = 2


def _gather_rows_kernel(idx_ref, nxt_ref, src_ref, o_ref, buf, sems):
    i = pl.program_id(0)
    n = pl.num_programs(0)
    slot = i % 2

    def copy(ids, s, t):
        return pltpu.make_async_copy(src_ref.at[pl.ds(ids[0, t], 1)], buf.at[s, pl.ds(t, 1)], sems.at[s])

    def issue(ids, s):
        def body(g, carry):
            for p in range(DMA_PRIORITIES):
                copy(ids, s, g * DMA_PRIORITIES + p).start(priority=p)
            return carry
        lax.fori_loop(0, GATHER_ROWS // DMA_PRIORITIES, body, 0, unroll=4)

    @pl.when(i == 0)
    def _():
        issue(idx_ref, 0)

    @pl.when(i + 1 < n)
    def _():
        issue(nxt_ref, 1 - slot)

    pltpu.make_async_copy(src_ref.at[pl.ds(0, GATHER_ROWS)], buf.at[slot], sems.at[slot]).wait()
    o_ref[...] = buf[slot]


def _as_row_tiles(a):
    return a.reshape(a.shape[0], a.shape[1] // LANES, LANES)


def _gather_rows(src, idx):
    n = idx.shape[0]
    nb = n // GATHER_ROWS
    ids = idx.reshape(nb, 1, GATHER_ROWS)
    smem_blk = lambda f: pl.BlockSpec((None, 1, GATHER_ROWS), f, memory_space=pltpu.SMEM)
    blk = (GATHER_ROWS,) + src.shape[1:]
    return pl.pallas_call(
        _gather_rows_kernel,
        grid=(nb,),
        in_specs=[smem_blk(lambda i: (i, 0, 0)), smem_blk(lambda i: (jnp.minimum(i + 1, nb - 1), 0, 0)),
                  pl.BlockSpec(memory_space=pl.ANY)],
        out_specs=pl.BlockSpec(blk, lambda i: (i, 0, 0)),
        out_shape=jax.ShapeDtypeStruct((n,) + src.shape[1:], src.dtype),
        scratch_shapes=[pltpu.VMEM((2,) + blk, src.dtype), pltpu.SemaphoreType.DMA((2,))],
        compiler_params=_cparams(1),
        name="moe_dispatch_gather",
    )(ids, ids, src)


def _combine_kernel(idx_ref, nxt_ref, yb_ref, x_ref, g2_ref, o_ref, buf, acc, sems, *, top_k):
    i = pl.program_id(0)
    n = pl.num_programs(0)
    slot = i % 2

    def copy(ids, s, t, k):
        return pltpu.make_async_copy(yb_ref.at[pl.ds(ids[0, t * top_k + k], 1)], buf.at[s, k, pl.ds(t, 1)], sems.at[s])

    def issue(ids, s):
        def body(t, carry):
            for k in range(top_k):
                copy(ids, s, t, k).start(priority=k % DMA_PRIORITIES)
            return carry
        lax.fori_loop(0, COMBINE_TM, body, 0, unroll=2)

    @pl.when(i == 0)
    def _():
        issue(idx_ref, 0)

    @pl.when(i + 1 < n)
    def _():
        issue(nxt_ref, 1 - slot)

    for k in range(top_k):
        pltpu.make_async_copy(yb_ref.at[pl.ds(0, COMBINE_TM)], buf.at[slot, k], sems.at[slot]).wait()

    total = buf[slot, 0]
    for k in range(1, top_k):
        total = total + buf[slot, k]
    acc[...] = total
    for c in range(acc.shape[1]):
        cs = slice(c * LANES, (c + 1) * LANES)
        o_ref[:, cs] = x_ref[:, cs] + g2_ref[:, cs] * acc[:, c, :]


def _moe_combine(yb, slot_of_assign, x, mod, mod_row, top_k):
    m, d = x.shape
    tm = COMBINE_TM
    nt = m // tm
    row = mod_row(tm)
    ids = slot_of_assign.reshape(nt, 1, tm * top_k)
    smem_blk = lambda f: pl.BlockSpec((None, 1, tm * top_k), f, memory_space=pltpu.SMEM)
    return pl.pallas_call(
        functools.partial(_combine_kernel, top_k=top_k),
        grid=(nt,),
        in_specs=[smem_blk(lambda i: (i, 0, 0)), smem_blk(lambda i: (jnp.minimum(i + 1, nt - 1), 0, 0)),
                  pl.BlockSpec(memory_space=pl.ANY),
                  pl.BlockSpec((tm, d), lambda i: (i, 0)),
                  pl.BlockSpec((None, 1, d), lambda i: (row(i), 0, 5))],
        out_specs=pl.BlockSpec((tm, d), lambda i: (i, 0)),
        out_shape=jax.ShapeDtypeStruct((m, d), F32),
        scratch_shapes=[pltpu.VMEM((2, top_k, tm) + yb.shape[1:], F32), pltpu.VMEM((tm,) + yb.shape[1:], F32),
                        pltpu.SemaphoreType.DMA((2,))],
        compiler_params=_cparams(1),
        name="moe_combine",
    )(ids, ids, yb, x, mod)


def _moe(h2, idx, gate, x, mod, mod_row, w_gu, b_gu, w_down, b_down, layer):
    n_exp = w_gu.shape[1]
    slot_token, slot_gate, slot_of_assign, block_expert, n_active = _moe_dispatch(idx[:, :TOP_K], gate[:, :TOP_K], n_exp)
    xb = _gather_rows(_as_row_tiles(h2), slot_token).reshape(slot_token.shape[0], h2.shape[1])
    yb = _moe_experts(xb, slot_gate, block_expert, n_active, w_gu, b_gu, w_down, b_down, layer)
    return _moe_combine(_as_row_tiles(yb), slot_of_assign, x, mod, mod_row, TOP_K)


def _pad_w_in(w):
    wb = w.astype(BF16)
    cut = 3 * NA_WIDTH + RW_SHIFT_WIDTH
    na = 3 * NA_WIDTH
    return jnp.concatenate([wb[:, na:cut], jnp.zeros((w.shape[0], RW_PAD), BF16), wb[:, :na], wb[:, cut:]], axis=1)


def kernel(x_prompt, x_sample, c, cache_k, cache_v, state_rwkv_fwd, state_rwkv_bwd, state_hgrn_fwd, state_hgrn_bwd, c_ctx, norm1_g, norm2_g, w_ada, b_ada, w_in, w_out, na_rpb, rw_mu, rw_w0, rw_w2, rw_a0, rw_a2, rw_g2, rw_k_k, rw_k_a, rw_r_k, rw_ln_g, rw_ln_b, hg_lb_raw, hg_norm_g, router_w, router_b, exp_w_gu, exp_b_gu, exp_w_down, exp_b_down, final_g):
    n_p, t_p, d = x_prompt.shape
    n_s, t_s, _ = x_sample.shape
    depth = w_in.shape[0]
    mp, ms = n_p * t_p, n_s * t_s
    x = jnp.concatenate([x_prompt.reshape(mp, d), x_sample.reshape(ms, d)], axis=0)
    n_mod_rows = 8
    cond = jnp.concatenate([c_ctx[None], c, jnp.zeros((n_mod_rows - 1 - n_s, d), F32)], axis=0)
    mod_row = lambda tm: _mod_row_fn(tm, mp, t_s)
    p_lb = jax.nn.softmax(hg_lb_raw.astype(F32), axis=0)
    hg_lb = jnp.cumsum(p_lb, axis=0) - p_lb[:1]
    ctx_out = []
    for l in range(depth):
        mod2d = _ada(cond, w_ada, b_ada, l)
        mod = mod2d.reshape(n_mod_rows, 1, 6 * d)
        proj = _in_proj(x, norm1_g[l], mod, _pad_w_in(w_in[l]), mod_row)
        na_p, k_ctx, v_ctx = _ctx_attention(proj, n_p, t_p)
        na_s = _na_attention(proj, mp, n_s, t_s, cache_k, cache_v, l, _na_bias_table(na_rpb[l], t_s // GRID_W))
        na_o = jnp.concatenate([na_p, na_s], axis=0)
        lw = dict(rw_mu=rw_mu[l], rw_w0=rw_w0[l], rw_w2=rw_w2[l], rw_a0=rw_a0[l], rw_a2=rw_a2[l], rw_g2=rw_g2[l],
                  rw_k_k=rw_k_k[l], rw_k_a=rw_k_a[l], rw_r_k=rw_r_k[l], rw_ln_g=rw_ln_g[l], rw_ln_b=rw_ln_b[l])
        rw_groups = [(0, n_p, t_p, jnp.zeros((2, n_p, RW_HEADS, RW_HEAD_DIM, RW_HEAD_DIM), F32)),
                     (mp, n_s, t_s, jnp.stack([state_rwkv_fwd[:, l], state_rwkv_bwd[:, l]]))]
        rw_o, rw_states = _rwkv_mixer(proj, rw_groups, lw)
        hg_groups = [(0, n_p, t_p, jnp.zeros((2, n_p, HG_HEADS, HG_KEY_DIM, HG_VAL_DIM), F32)),
                     (mp, n_s, t_s, jnp.stack([state_hgrn_fwd[:, l], state_hgrn_bwd[:, l]]))]
        hg_o, hg_states = _hgrn_mixer(proj, hg_groups, hg_lb[l], hg_norm_g[l])
        x, h2, idx, gate = _out_proj(na_o, rw_o, hg_o, x, w_out[l].astype(BF16), mod, norm2_g[l], router_w[l],
                                     router_b[l], mod_row)
        x = _moe(h2, idx, gate, x, mod, mod_row, exp_w_gu, exp_b_gu, exp_w_down, exp_b_down, l)
        ctx_out.append((k_ctx, v_ctx, rw_states[0][0], rw_states[0][1], hg_states[0][0], hg_states[0][1]))
    y = _final_norm(x, final_g)
    outs = [y[:mp].reshape(n_p, t_p, d), y[mp:].reshape(n_s, t_s, d)]
    outs += [jnp.stack([t[i] for t in ctx_out], axis=1) for i in range(6)]
    return tuple(outs)
```

```python
import functools

import numpy as np
import jax
import jax.numpy as jnp
from jax import lax
from jax.experimental import pallas as pl
from jax.experimental.pallas import tpu as pltpu

F32 = jnp.float32
BF16 = jnp.bfloat16

D_MODEL = 2048
GRID_W = 64
NA_HEADS = 16
NA_HEAD_DIM = 64
NA_WIDTH = NA_HEADS * NA_HEAD_DIM
NA_KH = 8
NA_KW = 16
RW_HEADS = 8
RW_HEAD_DIM = 64
RW_WIDTH = RW_HEADS * RW_HEAD_DIM
RW_DECAY_LORA = 64
RW_ICLR_LORA = 64
RW_GATE_LORA = 128
RW_SHIFT_WIDTH = 3 * RW_WIDTH + 2 * RW_DECAY_LORA + 2 * RW_ICLR_LORA + RW_GATE_LORA
RW_GN_EPS = 64e-5
HG_HEADS = 4
HG_KEY_DIM = 128
HG_VAL_DIM = 128
HG_WIDTH = HG_HEADS * HG_VAL_DIM
HG_MIN_F = 1e-6
IN_WIDTH = 3 * NA_WIDTH + RW_SHIFT_WIDTH + 5 * HG_WIDTH
N_EXPERTS = 32
TOP_K = 4
D_EXPERT = 2048
SWIGLU_LIMIT = 7.0
SWIGLU_ALPHA = 1.702
RMS_EPS = 1e-6
NEG_INF = -1e30

LANES = 128
PCOL_RW = 0
RW_PAD = -RW_SHIFT_WIDTH % HG_WIDTH
PCOL_Q = RW_SHIFT_WIDTH + RW_PAD
PCOL_K = PCOL_Q + NA_WIDTH
PCOL_V = PCOL_K + NA_WIDTH
PCOL_HQ = PCOL_V + NA_WIDTH
PCOL_HFF = PCOL_HQ + HG_WIDTH
PCOL_HFB = PCOL_HFF + HG_WIDTH
PCOL_HI = PCOL_HFB + HG_WIDTH
PCOL_HG = PCOL_HI + HG_WIDTH
IN_WIDTH_PAD = PCOL_HG + HG_WIDTH
SCAN_CHUNK = 64
TOKEN_TILE = 256
VMEM_LIMIT = 56 * 1024 * 1024


def _cparams(n_axes, vmem=VMEM_LIMIT):
    return pltpu.CompilerParams(dimension_semantics=("arbitrary",) * n_axes, vmem_limit_bytes=vmem)


def _dot(a, b):
    return jnp.dot(a.astype(BF16), b.astype(BF16), preferred_element_type=F32)


def _dot_nt(a, b):
    return lax.dot_general(a.astype(BF16), b.astype(BF16), (((1,), (1,)), ((), ())), preferred_element_type=F32)


def _dot_tn(a, b):
    return lax.dot_general(a.astype(BF16), b.astype(BF16), (((0,), (0,)), ((), ())), preferred_element_type=F32)


def _split3(x):
    hi = x.astype(BF16)
    r1 = x - hi.astype(F32)
    mid = r1.astype(BF16)
    lo = (r1 - mid.astype(F32)).astype(BF16)
    return hi, mid, lo


def _dot01_left(m01, x):
    hi, mid, lo = _split3(x)
    d = lambda p: jnp.dot(m01, p, preferred_element_type=F32)
    return d(hi) + (d(mid) + d(lo))


def _dot01_right(x, m01):
    hi, mid, lo = _split3(x)
    d = lambda p: jnp.dot(p, m01, preferred_element_type=F32)
    return d(hi) + (d(mid) + d(lo))


def _sigmoid(x):
    return 1.0 / (1.0 + jnp.exp(-x))


def _softplus(x):
    return jnp.maximum(x, 0.0) + jnp.log(1.0 + jnp.exp(-jnp.abs(x)))


def _head_sum_matrix(width, head_dim):
    i = np.arange(width)
    return jnp.asarray((i[:, None] // head_dim) == (i[None, :] // head_dim), dtype=BF16)


def _order_masks(n, reverse):
    row = lax.broadcasted_iota(jnp.int32, (n, n), 0)
    col = lax.broadcasted_iota(jnp.int32, (n, n), 1)
    diff = col - row if reverse else row - col
    return diff >= 0, diff > 0


def _rwkv_prep_kernel(x_ref, prev_ref, next_ref, mu_ref, w0_ref, w2_ref, a0_ref, a2_ref, g2_ref, kkw_ref, kaw_ref,
                      hsum_ref, r_ref, v_ref, kkn_ref, g_ref, lw_ref, kd_ref, av_ref, *, seq_groups):
    tm = x_ref.shape[0]
    t0 = pl.program_id(0) * tm
    (off0, len0), (off1, len1) = seq_groups
    in_first = t0 < off1
    pos = jnp.where(in_first, lax.rem(t0 - off0, len0), lax.rem(jnp.maximum(t0 - off1, 0), len1))
    seq_len = jnp.where(in_first, len0, len1)
    has_prev = (pos != 0).astype(F32)
    has_next = (pos + tm != seq_len).astype(F32)
    x = x_ref[...]
    row = lax.broadcasted_iota(jnp.int32, x.shape, 0)
    before = jnp.where(row == 0, prev_ref[7:8, :] * has_prev, pltpu.roll(x, 1, 0))
    after = jnp.where(row == tm - 1, next_ref[0:1, :] * has_next, pltpu.roll(x, tm - 1, 0))
    x = x + (0.5 * (before + after) - x) * mu_ref[...]
    r_ref[...] = x[:, 0:RW_WIDTH]
    v_ref[...] = x[:, 2 * RW_WIDTH:3 * RW_WIDTH]
    k = x[:, RW_WIDTH:2 * RW_WIDTH]
    o = 3 * RW_WIDTH
    wd = (x[:, o:o + 64], x[:, o + 64:o + 128])
    ad = (x[:, o + 128:o + 192], x[:, o + 192:o + 256])
    gd = x[:, o + 256:o + 384]
    g_ref[...] = _dot(_sigmoid(gd), g2_ref[...])
    kk = k * kkw_ref[...]
    ssq = _dot01_right(kk * kk, hsum_ref[...])
    kkn_ref[...] = kk / jnp.maximum(jnp.sqrt(ssq), 1e-12)
    for d in range(2):
        wl = -_softplus(-(w0_ref[d] + _dot(jnp.tanh(wd[d]), w2_ref[d]))) - 0.5
        lw_ref[d] = -jnp.exp(wl)
        a = _sigmoid(a0_ref[d] + _dot(ad[d], a2_ref[d]))
        kd_ref[d] = k * (1.0 + (a - 1.0) * kaw_ref[...])
        av_ref[d] = a


def _rwkv_scan_kernel(*refs, n_chunks):
    ins = (refs[0:6], refs[6:12])
    s0_ref, yf_ref, yb_ref, sout_ref, s_scr = refs[12:]
    c = pl.program_id(1)
    n = SCAN_CHUNK

    @pl.when(c == 0)
    def _():
        s_scr[...] = s0_ref[...]

    masks = [_order_masks(n, False), _order_masks(n, True)]
    incl01 = [m[0].astype(BF16) for m in masks]
    ch = [(d, h) for d in range(2) for h in range(RW_HEADS)]
    cs = range(len(ch))
    m_incl = [masks[d][0] for d, _ in ch]
    m_strict = [masks[d][1] for d, _ in ch]
    col = lambda j: [ins[d][j][:, h * RW_HEAD_DIM:(h + 1) * RW_HEAD_DIM] for d, h in ch]
    r, v, kkn, lw, kd, av = (col(j) for j in range(6))
    cum = [_dot01_left(incl01[ch[i][0]], lw[i]) for i in cs]
    p_tot = [jnp.exp(jnp.sum(lw[i], axis=0, keepdims=True)) for i in cs]
    p_inv = [jnp.exp(-cum[i]) for i in cs]
    a_t = [-kkn[i] * jnp.exp(cum[i] - lw[i]) for i in cs]
    b_t = [kkn[i] * av[i] * p_inv[i] for i in cs]
    k_t = [kd[i] * p_inv[i] for i in cs]
    r_t = [r[i] * jnp.exp(cum[i]) for i in cs]
    a_ab = [jnp.where(m_strict[i], _dot_nt(a_t[i], b_t[i]), 0.0) for i in cs]
    a_ak = [jnp.where(m_strict[i], _dot_nt(a_t[i], k_t[i]), 0.0) for i in cs]
    a_rb = [jnp.where(m_incl[i], _dot_nt(r_t[i], b_t[i]), 0.0) for i in cs]
    a_rk = [jnp.where(m_incl[i], _dot_nt(r_t[i], k_t[i]), 0.0) for i in cs]
    t_m = list(a_ab)
    n_pow = list(a_ab)
    for _ in range(int(np.log2(n)) - 1):
        n_pow = [_dot(n_pow[i], n_pow[i]) for i in cs]
        t_m = [t_m[i] + n_pow[i] + _dot(t_m[i], n_pow[i]) for i in cs]
    x1 = [_dot(a_ak[i], v[i]) for i in cs]
    w1 = [x1[i] + _dot(t_m[i], x1[i]) for i in cs]
    w2 = [a_t[i] + _dot(t_m[i], a_t[i]) for i in cs]
    y0 = [_dot(a_rk[i], v[i]) + _dot(a_rb[i], w1[i]) for i in cs]
    q = [r_t[i] + _dot(a_rb[i], w2[i]) for i in cs]
    g_m = [_dot_tn(w2[i], b_t[i]) for i in cs]
    h_m = [_dot_tn(w1[i], b_t[i]) + _dot_tn(v[i], k_t[i]) for i in cs]
    s0 = [s_scr[d, h] for d, h in ch]
    ys = [y0[i] + _dot_nt(q[i], s0[i]) for i in cs]
    s_new = [(s0[i] + _dot(s0[i], g_m[i]) + h_m[i]) * p_tot[i] for i in cs]
    for i, (d, h) in enumerate(ch):
        s_scr[d, h] = s_new[i]
    yf_ref[...] = jnp.concatenate(ys[:RW_HEADS], axis=-1)
    yb_ref[...] = jnp.concatenate(ys[RW_HEADS:], axis=-1)

    @pl.when(c == n_chunks - 1)
    def _():
        sout_ref[...] = s_scr[...]


def _rwkv_post_kernel(yf_ref, yb_ref, r_ref, v_ref, kd_ref, g_ref, rk_ref, lng_ref, lnb_ref, hsum_ref, o_ref):
    r = r_ref[...]
    v = v_ref[...]
    y = yf_ref[...] + yb_ref[...]
    hsum = hsum_ref[...]
    inv_n = 1.0 / RW_HEAD_DIM
    mu = _dot01_right(y, hsum) * inv_n
    yc = y - mu
    var = _dot01_right(yc * yc, hsum) * inv_n
    yn = yc * lax.rsqrt(var + RW_GN_EPS) * lng_ref[...] + lnb_ref[...]
    bonus = _dot01_right(r * (0.5 * (kd_ref[0] + kd_ref[1])) * rk_ref[...], hsum) * v
    o_ref[...] = (yn + bonus) * g_ref[...]


def _rwkv_mixer(proj, groups, lw):
    m = proj.shape[0]
    tm = TOKEN_TILE
    sub = 8
    assert PCOL_RW == 0 and len(groups) == 2 and all(g[2] % tm == 0 for g in groups)
    hsum = _head_sum_matrix(RW_WIDTH, RW_HEAD_DIM)
    row = lambda a: a.reshape(1, -1)
    full = lambda shape: pl.BlockSpec(shape, lambda i: (0,) * len(shape))
    tok = lambda w: pl.BlockSpec((tm, w), lambda i: (i, 0))
    tok2 = pl.BlockSpec((2, tm, RW_WIDTH), lambda i: (0, i, 0))
    d2 = jax.ShapeDtypeStruct((2, m, RW_WIDTH), F32)
    d1 = jax.ShapeDtypeStruct((m, RW_WIDTH), F32)
    per = tm // sub
    prev_blk = pl.BlockSpec((sub, RW_SHIFT_WIDTH), lambda i: (jnp.maximum(i * per - 1, 0), 0))
    next_blk = pl.BlockSpec((sub, RW_SHIFT_WIDTH), lambda i: (jnp.minimum((i + 1) * per, m // sub - 1), 0))
    xs_r, xs_v, kkn, g, lwd, kd, av = pl.pallas_call(
        functools.partial(_rwkv_prep_kernel, seq_groups=tuple((g[0], g[2]) for g in groups)),
        grid=(m // tm,),
        in_specs=[tok(RW_SHIFT_WIDTH), prev_blk, next_blk, full((1, RW_SHIFT_WIDTH)),
                  full((2, 1, RW_WIDTH)), full((2, RW_DECAY_LORA, RW_WIDTH)),
                  full((2, 1, RW_WIDTH)), full((2, RW_ICLR_LORA, RW_WIDTH)), full((RW_GATE_LORA, RW_WIDTH)),
                  full((1, RW_WIDTH)), full((1, RW_WIDTH)), full((RW_WIDTH, RW_WIDTH))],
        out_specs=[tok(RW_WIDTH), tok(RW_WIDTH), tok(RW_WIDTH), tok(RW_WIDTH), tok2, tok2, tok2],
        out_shape=[d1, d1, d1, d1, d2, d2, d2],
        compiler_params=_cparams(1),
        name="rwkv_prep",
    )(proj, proj, proj, row(lw['rw_mu']), lw['rw_w0'].reshape(2, 1, RW_WIDTH), lw['rw_w2'],
      lw['rw_a0'].reshape(2, 1, RW_WIDTH), lw['rw_a2'], lw['rw_g2'], row(lw['rw_k_k']), row(lw['rw_k_a']), hsum)

    n = SCAN_CHUNK
    yfs, ybs, states = [], [], []
    for row_off, n_seq, seq_len, s0 in groups:
        nc = seq_len // n
        base = row_off // n
        chunk = lambda b, c, d, nc=nc: b * nc + (nc - 1 - c if d else c)
        x_spec = lambda d, col, base=base, chunk=chunk: pl.BlockSpec(
            (n, RW_WIDTH), lambda b, c: (base + chunk(b, c, d), col))
        d_spec = lambda d, base=base, chunk=chunk: pl.BlockSpec(
            (None, n, RW_WIDTH), lambda b, c: (d, base + chunk(b, c, d), 0))
        y_spec = lambda d, chunk=chunk: pl.BlockSpec((n, RW_WIDTH), lambda b, c: (chunk(b, c, d), 0))
        s_spec = pl.BlockSpec((2, None, RW_HEADS, RW_HEAD_DIM, RW_HEAD_DIM), lambda b, c: (0, b, 0, 0, 0))
        per_dir = lambda d: [x_spec(d, 0), x_spec(d, 0), x_spec(d, 0), d_spec(d), d_spec(d), d_spec(d)]
        y_shape = jax.ShapeDtypeStruct((n_seq * seq_len, RW_WIDTH), F32)
        yf, yb, s_out = pl.pallas_call(
            functools.partial(_rwkv_scan_kernel, n_chunks=nc),
            grid=(n_seq, nc),
            in_specs=per_dir(0) + per_dir(1) + [s_spec],
            out_specs=[y_spec(0), y_spec(1), s_spec],
            out_shape=[y_shape, y_shape,
                       jax.ShapeDtypeStruct((2, n_seq, RW_HEADS, RW_HEAD_DIM, RW_HEAD_DIM), F32)],
            scratch_shapes=[pltpu.VMEM((2, RW_HEADS, RW_HEAD_DIM, RW_HEAD_DIM), F32)],
            compiler_params=_cparams(2),
            name="rwkv_scan",
        )(*([xs_r, xs_v, kkn, lwd, kd, av] * 2), s0)
        yfs.append(yf)
        ybs.append(yb)
        states.append(s_out)
    yf = jnp.concatenate(yfs, axis=0)
    yb = jnp.concatenate(ybs, axis=0)

    out = pl.pallas_call(
        _rwkv_post_kernel,
        grid=(m // tm,),
        in_specs=[tok(RW_WIDTH), tok(RW_WIDTH), tok(RW_WIDTH), tok(RW_WIDTH), tok2, tok(RW_WIDTH),
                  full((1, RW_WIDTH)), full((1, RW_WIDTH)), full((1, RW_WIDTH)), full((RW_WIDTH, RW_WIDTH))],
        out_specs=tok(RW_WIDTH),
        out_shape=d1,
        compiler_params=_cparams(1),
        name="rwkv_post",
    )(yf, yb, xs_r, xs_v, kd, g, row(lw['rw_r_k']), row(lw['rw_ln_g']), row(lw['rw_ln_b']), hsum)
    return out, states


HG_LEVELS = (32, 16, 8, 4, 2, 1)


def _hgrn_consts():
    n = SCAN_CHUNK
    cums, masks = [], []
    for rev in (False, True):
        p = np.arange(n)[::-1] if rev else np.arange(n)
        pt, pj = p[:, None], p[None, :]
        rows = [pj <= pt, pj > pt]
        rows += [((pt // m) % 2 == 1) & (pj // m == pt // m) & (pj <= pt) for m in HG_LEVELS]
        rows += [((pt // m) % 2 == 0) & (pj // m == pt // m) & (pj > pt) for m in HG_LEVELS]
        cums.append(np.concatenate(rows, 0))
        mk = [pt == pj]
        mk += [(pt // (2 * m) == pj // (2 * m)) & ((pt // m) % 2 == 1) & ((pj // m) % 2 == 0) for m in HG_LEVELS]
        masks.append(np.stack(mk))
    return jnp.asarray(np.stack(cums), BF16), jnp.asarray(np.stack(masks), F32)


def _hgrn_scan_kernel(qf_ref, zf_ref, vf_ref, qb_ref, zb_ref, vb_ref, lb_ref, cm_ref, mask_ref, s0_ref,
                      of_ref, ob_ref, sout_ref, s_scr, *, n_chunks):
    c = pl.program_id(1)
    n = SCAN_CHUNK
    nl = len(HG_LEVELS)
    ins = ((qf_ref, zf_ref, vf_ref), (qb_ref, zb_ref, vb_ref))

    @pl.when(c == 0)
    def _():
        s_scr[...] = s0_ref[...]

    ones = jnp.ones((n, HG_VAL_DIM), BF16)
    tn01 = lambda p: lax.dot_general(p, ones, (((0,), (0,)), ((), ())), preferred_element_type=F32)
    ch = [(d, h) for d in range(2) for h in range(HG_HEADS)]
    cs = range(len(ch))
    col = lambda j: [ins[d][j][:, h * HG_KEY_DIM:(h + 1) * HG_KEY_DIM] for d, h in ch]
    hq, z, v = col(0), col(1), col(2)
    lb = [lb_ref[d, :, h * HG_KEY_DIM:(h + 1) * HG_KEY_DIM] for d, h in ch]
    q = [hq[i] * _sigmoid(hq[i]) for i in cs]
    lf = [jnp.log(jnp.maximum(lb[i] + (1.0 - lb[i]) * _sigmoid(z[i]), HG_MIN_F)) for i in cs]
    k = [(1.0 - lb[i]) * _sigmoid(-z[i]) for i in cs]
    parts = [_split3(lf[i]) for i in cs]
    ex = [jnp.exp(sum(jnp.dot(cm_ref[ch[i][0]], p, preferred_element_type=F32) for p in parts[i][::-1])) for i in cs]
    p_col = [jnp.exp(sum(tn01(p) for p in parts[i][::-1])) for i in cs]
    att = [mask_ref[ch[i][0], 0] * _dot_nt(q[i], k[i]) for i in cs]
    for lv in range(nl):
        lo_q, lo_k = (2 + lv) * n, (2 + nl + lv) * n
        lvl = [_dot_nt(q[i] * ex[i][lo_q:lo_q + n], k[i] * ex[i][lo_k:lo_k + n]) for i in cs]
        att = [att[i] + mask_ref[ch[i][0], lv + 1] * lvl[i] for i in cs]
    s0 = [s_scr[d, h] for d, h in ch]
    outs = [_dot(q[i] * ex[i][0:n], s0[i]) + _dot(att[i], v[i]) for i in cs]
    s_new = [s0[i] * p_col[i] + _dot_tn(k[i] * ex[i][n:2 * n], v[i]) for i in cs]
    for i, (d, h) in enumerate(ch):
        s_scr[d, h] = s_new[i]
    of_ref[...] = jnp.concatenate(outs[:HG_HEADS], axis=-1)
    ob_ref[...] = jnp.concatenate(outs[HG_HEADS:], axis=-1)

    @pl.when(c == n_chunks - 1)
    def _():
        sout_ref[...] = s_scr[...]


def _hgrn_post_kernel(of_ref, ob_ref, g_ref, gain_ref, hsum_ref, out_ref):
    o = of_ref[...] + ob_ref[...]
    ms = _dot01_right(o * o, hsum_ref[...]) * (1.0 / HG_VAL_DIM)
    on = o * lax.rsqrt(ms + RMS_EPS) * gain_ref[...]
    g = g_ref[...]
    out_ref[...] = on * (g * _sigmoid(g))


def _hgrn_mixer(proj, groups, lb, gain):
    m = proj.shape[0]
    n = SCAN_CHUNK
    cm, masks = _hgrn_consts()
    cq, cff, ci, cg = (col // HG_WIDTH for col in (PCOL_HQ, PCOL_HFF, PCOL_HI, PCOL_HG))
    ofs, obs, states = [], [], []
    for row_off, n_seq, seq_len, s0 in groups:
        nc = seq_len // n
        base = row_off // n
        chunk = lambda b, c, d, nc=nc: b * nc + (nc - 1 - c if d else c)
        x_spec = lambda d, col, base=base, chunk=chunk: pl.BlockSpec(
            (n, HG_WIDTH), lambda b, c: (base + chunk(b, c, d), col))
        o_spec = lambda d, chunk=chunk: pl.BlockSpec((n, HG_WIDTH), lambda b, c: (chunk(b, c, d), 0))
        s_spec = pl.BlockSpec((2, None, HG_HEADS, HG_KEY_DIM, HG_VAL_DIM), lambda b, c: (0, b, 0, 0, 0))
        whole = lambda a: pl.BlockSpec(a.shape, lambda b, c: (0,) * a.ndim)
        lb3 = lb.reshape(2, 1, HG_WIDTH)
        o_shape = jax.ShapeDtypeStruct((n_seq * seq_len, HG_WIDTH), F32)
        of, ob, s_out = pl.pallas_call(
            functools.partial(_hgrn_scan_kernel, n_chunks=nc),
            grid=(n_seq, nc),
            in_specs=[x_spec(0, cq), x_spec(0, cff), x_spec(0, ci), x_spec(1, cq), x_spec(1, cff + 1), x_spec(1, ci),
                      whole(lb3), whole(cm), whole(masks), s_spec],
            out_specs=[o_spec(0), o_spec(1), s_spec],
            out_shape=[o_shape, o_shape,
                       jax.ShapeDtypeStruct((2, n_seq, HG_HEADS, HG_KEY_DIM, HG_VAL_DIM), F32)],
            scratch_shapes=[pltpu.VMEM((2, HG_HEADS, HG_KEY_DIM, HG_VAL_DIM), F32)],
            compiler_params=_cparams(2),
            name="hgrn_scan",
        )(proj, proj, proj, proj, proj, proj, lb3, cm, masks, s0)
        ofs.append(of)
        obs.append(ob)
        states.append(s_out)
    of = jnp.concatenate(ofs, axis=0)
    ob = jnp.concatenate(obs, axis=0)

    tm = TOKEN_TILE
    tok = pl.BlockSpec((tm, HG_WIDTH), lambda i: (i, 0))
    out = pl.pallas_call(
        _hgrn_post_kernel,
        grid=(m // tm,),
        in_specs=[tok, tok,
                  pl.BlockSpec((tm, HG_WIDTH), lambda i: (i, cg)),
                  pl.BlockSpec((1, HG_WIDTH), lambda i: (0, 0)),
                  pl.BlockSpec((HG_WIDTH, HG_WIDTH), lambda i: (0, 0))],
        out_specs=pl.BlockSpec((tm, HG_WIDTH), lambda i: (i, 0)),
        out_shape=jax.ShapeDtypeStruct((m, HG_WIDTH), F32),
        compiler_params=_cparams(1),
        name="hgrn_post",
    )(of, ob, proj, gain.reshape(1, HG_WIDTH), _head_sum_matrix(HG_WIDTH, HG_VAL_DIM))
    return out, states


NA_HEADS_PER_STEP = LANES // NA_HEAD_DIM
NA_ROWS_PER_ITER = 8


def _ctx_attn_kernel(q_ref, k_ref, v_ref, o_ref, kc_ref, vc_ref):
    scale = NA_HEAD_DIM ** -0.5
    hs = range(NA_HEADS_PER_STEP)
    sl = [slice(h * NA_HEAD_DIM, (h + 1) * NA_HEAD_DIM) for h in hs]
    k = [k_ref[:, s] for s in sl]
    v = [v_ref[:, s] for s in sl]
    for h in hs:
        kc_ref[h] = k[h]
        vc_ref[h] = v[h]
    s = [_dot_nt(q_ref[:, sl[h]], k[h]) * scale for h in hs]
    p = [jnp.exp(s[h] - jnp.max(s[h], axis=-1, keepdims=True)) for h in hs]
    o = [_dot(p[h], v[h]) / jnp.sum(p[h], axis=-1, keepdims=True) for h in hs]
    o_ref[...] = jnp.concatenate(o, axis=-1)


def _ctx_attention(proj, n_seq, seq_len):
    nb = NA_WIDTH // LANES
    blk = lambda off: pl.BlockSpec((seq_len, LANES), lambda b, j: (b, off + j))
    c_spec = pl.BlockSpec((None, NA_HEADS_PER_STEP, seq_len, NA_HEAD_DIM), lambda b, j: (b, j, 0, 0))
    c_shape = jax.ShapeDtypeStruct((n_seq, NA_HEADS, seq_len, NA_HEAD_DIM), F32)
    return pl.pallas_call(
        _ctx_attn_kernel,
        grid=(n_seq, nb),
        in_specs=[blk(PCOL_Q // LANES), blk(PCOL_K // LANES), blk(PCOL_V // LANES)],
        out_specs=[pl.BlockSpec((seq_len, LANES), lambda b, j: (b, j)), c_spec, c_spec],
        out_shape=[jax.ShapeDtypeStruct((n_seq * seq_len, NA_WIDTH), F32), c_shape, c_shape],
        compiler_params=_cparams(2),
        name="ctx_attention",
    )(proj, proj, proj)


def _na_bias_table(rpb, rows):
    assert rows >= NA_KH
    o = np.arange(NA_KH)[:, None]
    i = np.arange(NA_KH)[None, :]
    dr = i - o + (NA_KH - 1)
    cq = np.arange(GRID_W)
    ck = np.arange(GRID_W)
    col_start = np.clip(cq - NA_KW // 2, 0, GRID_W - NA_KW)
    col_mask = (ck[None, :] >= col_start[:, None]) & (ck[None, :] < col_start[:, None] + NA_KW)
    dc = np.clip(ck[None, :] - cq[:, None], -(NA_KW - 1), NA_KW - 1) + (NA_KW - 1)
    onehot = jnp.asarray(dc.reshape(-1)[:, None] == np.arange(2 * NA_KW - 1)[None, :], F32)
    full = jnp.einsum('hrj,pj->hrp', rpb.astype(F32), onehot, precision=lax.Precision.HIGHEST)
    full = jnp.where(col_mask.reshape(-1)[None, None, :], full, NEG_INF).reshape(rpb.shape[0], -1, GRID_W, GRID_W)
    lo = lambda off: int(dr[off, 0])
    tab = jnp.stack([full[:, lo(off):lo(off) + NA_KH] for off in range(NA_KH)])
    return tab.transpose(0, 1, 3, 2, 4).reshape(NA_KH, rpb.shape[0], GRID_W, NA_KH * GRID_W)


def _na_kernel(q_ref, k_ref, v_ref, ck_ref, cv_ref, bias_ref, o_ref, *, rows):
    scale = NA_HEAD_DIM ** -0.5
    n_win = NA_KH * GRID_W

    def row_group(g, carry):
        ch = []
        for i in range(NA_ROWS_PER_ITER):
            r = g * NA_ROWS_PER_ITER + i
            r0 = jnp.clip(r - NA_KH // 2, 0, rows - NA_KH)
            q_at = pl.multiple_of(r * GRID_W, GRID_W)
            k_at = pl.multiple_of(r0 * GRID_W, GRID_W)
            for h in range(NA_HEADS_PER_STEP):
                ch.append((q_at, k_at, r - r0, h, slice(h * NA_HEAD_DIM, (h + 1) * NA_HEAD_DIM)))
        q = [q_ref[pl.ds(q_at, GRID_W), sl] for q_at, _, _, _, sl in ch]
        s_w = [_dot_nt(q[i], k_ref[pl.ds(k_at, n_win), sl]) * scale + bias_ref[off, h]
               for i, (_, k_at, off, h, sl) in enumerate(ch)]
        s_c = [_dot_nt(q[i], ck_ref[h]) * scale for i, (_, _, _, h, _) in enumerate(ch)]
        m = [jnp.maximum(jnp.max(a, axis=-1, keepdims=True), jnp.max(b, axis=-1, keepdims=True))
             for a, b in zip(s_w, s_c)]
        p_w = [jnp.exp(a - mm) for a, mm in zip(s_w, m)]
        p_c = [jnp.exp(b - mm) for b, mm in zip(s_c, m)]
        den = [jnp.sum(a, axis=-1, keepdims=True) + jnp.sum(b, axis=-1, keepdims=True) for a, b in zip(p_w, p_c)]
        o = [(_dot(p_w[i], v_ref[pl.ds(k_at, n_win), sl]) + _dot(p_c[i], cv_ref[h])) / den[i]
             for i, (_, k_at, _, h, sl) in enumerate(ch)]
        for i, (q_at, _, _, _, sl) in enumerate(ch):
            o_ref[pl.ds(q_at, GRID_W), sl] = o[i]
        return carry

    lax.fori_loop(0, rows // NA_ROWS_PER_ITER, row_group, 0)


def _na_attention(proj, row_off, n_seq, seq_len, cache_k, cache_v, layer, bias):
    assert row_off % seq_len == 0
    nb = NA_WIDTH // LANES
    rows = seq_len // GRID_W
    assert rows % NA_ROWS_PER_ITER == 0
    rb = row_off // seq_len
    past = cache_k.shape[3]
    blk = lambda off: pl.BlockSpec((seq_len, LANES), lambda b, j: (rb + b, off + j))
    c_spec = pl.BlockSpec((None, None, NA_HEADS_PER_STEP, past, NA_HEAD_DIM), lambda b, j: (b, layer, j, 0, 0))
    return pl.pallas_call(
        functools.partial(_na_kernel, rows=rows),
        grid=(n_seq, nb),
        in_specs=[blk(PCOL_Q // LANES), blk(PCOL_K // LANES), blk(PCOL_V // LANES), c_spec, c_spec,
                  pl.BlockSpec((NA_KH, NA_HEADS_PER_STEP, GRID_W, NA_KH * GRID_W), lambda b, j: (0, j, 0, 0))],
        out_specs=pl.BlockSpec((seq_len, LANES), lambda b, j: (b, j)),
        out_shape=jax.ShapeDtypeStruct((n_seq * seq_len, NA_WIDTH), F32),
        compiler_params=_cparams(2),
        name="na_attention",
    )(proj, proj, proj, cache_k, cache_v, bias)


ADA_TN = 1024
IN_TM = 1024
IN_TN = 1536
OUT_TM = 512


def _mod_row_fn(tm, n_ctx_rows, latent_len):
    def f(i):
        start = i * tm
        return jnp.where(start < n_ctx_rows, 0, 1 + (start - n_ctx_rows) // latent_len)
    return f


def _ada_kernel(c_ref, w_ref, b_ref, o_ref):
    x = c_ref[...]
    o_ref[...] = _dot(x * _sigmoid(x), w_ref[...]) + b_ref[...]


def _ada(cond, w_ada, b_ada, layer):
    n_rows, d = cond.shape
    n = w_ada.shape[2]
    return pl.pallas_call(
        _ada_kernel,
        grid=(n // ADA_TN,),
        in_specs=[pl.BlockSpec((n_rows, d), lambda j: (0, 0)),
                  pl.BlockSpec((None, d, ADA_TN), lambda j: (layer, 0, j)),
                  pl.BlockSpec((None, 1, ADA_TN), lambda j: (layer, 0, j))],
        out_specs=pl.BlockSpec((n_rows, ADA_TN), lambda j: (0, j)),
        out_shape=jax.ShapeDtypeStruct((n_rows, n), F32),
        compiler_params=_cparams(1),
        name="ada_mod",
    )(cond, w_ada, b_ada.reshape(b_ada.shape[0], 1, n))


def _rms_mod(x, g, sc, sh):
    y = x * lax.rsqrt(jnp.mean(x * x, axis=-1, keepdims=True) + RMS_EPS) * g
    return y * (1.0 + sc) + sh


def _in_proj_kernel(x_ref, g_ref, sh_ref, sc_ref, w_ref, o_ref, h_scr):
    @pl.when(pl.program_id(1) == 0)
    def _():
        h_scr[...] = _rms_mod(x_ref[...], g_ref[...], sc_ref[...], sh_ref[...]).astype(BF16)

    o_ref[...] = jnp.dot(h_scr[...], w_ref[...], preferred_element_type=F32)


def _in_proj(x, norm_g, mod, w, mod_row):
    m, d = x.shape
    n = w.shape[1]
    row = mod_row(IN_TM)
    return pl.pallas_call(
        _in_proj_kernel,
        grid=(m // IN_TM, n // IN_TN),
        in_specs=[pl.BlockSpec((IN_TM, d), lambda i, j: (i, 0)),
                  pl.BlockSpec((1, d), lambda i, j: (0, 0)),
                  pl.BlockSpec((None, 1, d), lambda i, j: (row(i), 0, 0)),
                  pl.BlockSpec((None, 1, d), lambda i, j: (row(i), 0, 1)),
                  pl.BlockSpec((d, IN_TN), lambda i, j: (0, j))],
        out_specs=pl.BlockSpec((IN_TM, IN_TN), lambda i, j: (i, j)),
        out_shape=jax.ShapeDtypeStruct((m, n), F32),
        scratch_shapes=[pltpu.VMEM((IN_TM, d), BF16)],
        compiler_params=_cparams(2),
        name="in_proj",
    )(x, norm_g.reshape(1, d), mod, mod, w)


def _out_proj_kernel(na_ref, rw_ref, hg_ref, x_ref, w_ref, g1_ref, n2_ref, sh2_ref, sc2_ref, rtw_ref, rtb_ref,
                     xo_ref, h2_ref, idx_ref, gate_ref):
    o1, o2 = NA_WIDTH, NA_WIDTH + RW_WIDTH
    mix = _dot(na_ref[...], w_ref[0:o1]) + _dot(rw_ref[...], w_ref[o1:o2]) + _dot(hg_ref[...], w_ref[o2:])
    x = x_ref[...] + g1_ref[...] * mix
    xo_ref[...] = x
    h = _rms_mod(x, n2_ref[...], sc2_ref[...], sh2_ref[...])
    h_hi = h.astype(BF16)
    h2_ref[...] = h_hi
    h_mid = (h - h_hi.astype(F32)).astype(BF16)
    d = lambda a, b: jnp.dot(a, b, preferred_element_type=F32)
    logits = d(h_hi, rtw_ref[0]) + (d(h_hi, rtw_ref[1]) + d(h_mid, rtw_ref[0])) + rtb_ref[...]
    lane = lax.broadcasted_iota(jnp.int32, logits.shape, 1).astype(F32)
    idx = jnp.zeros_like(logits)
    vals = []
    for k in range(TOP_K):
        mx = jnp.max(logits, axis=-1, keepdims=True)
        sel = jnp.min(jnp.where(logits == mx, lane, float(LANES)), axis=-1, keepdims=True)
        idx = jnp.where(lane == k, sel, idx)
        vals.append(mx)
        logits = jnp.where(lane == sel, 3.0 * NEG_INF, logits)
    es = [jnp.exp(v - vals[0]) for v in vals]
    den = es[0] + es[1] + es[2] + es[3]
    gate = jnp.zeros_like(idx)
    for k in range(TOP_K):
        gate = jnp.where(lane == k, es[k] / den, gate)
    idx_ref[...] = idx.astype(jnp.int32)
    gate_ref[...] = gate


def _out_proj(na_o, rw_o, hg_o, x, w_out, mod, norm2_g, router_w, router_b, mod_row):
    m, d = x.shape
    tm = OUT_TM
    row = mod_row(tm)
    n_exp = router_w.shape[1]
    rtw = jnp.pad(router_w.astype(F32), ((0, 0), (0, LANES - n_exp)))
    rtw_hi = rtw.astype(BF16)
    rtw = jnp.stack([rtw_hi, (rtw - rtw_hi.astype(F32)).astype(BF16)])
    rtb = jnp.pad(router_b, (0, LANES - n_exp), constant_values=NEG_INF).reshape(1, LANES)
    tok = lambda w: pl.BlockSpec((tm, w), lambda i: (i, 0))
    mod_blk = lambda part: pl.BlockSpec((None, 1, d), lambda i: (row(i), 0, part))
    return pl.pallas_call(
        _out_proj_kernel,
        grid=(m // tm,),
        in_specs=[tok(NA_WIDTH), tok(RW_WIDTH), tok(HG_WIDTH), tok(d),
                  pl.BlockSpec((d, d), lambda i: (0, 0)),
                  mod_blk(2), pl.BlockSpec((1, d), lambda i: (0, 0)), mod_blk(3), mod_blk(4),
                  pl.BlockSpec((2, d, LANES), lambda i: (0, 0, 0)), pl.BlockSpec((1, LANES), lambda i: (0, 0))],
        out_specs=[tok(d), tok(d), tok(LANES), tok(LANES)],
        out_shape=[jax.ShapeDtypeStruct((m, d), F32), jax.ShapeDtypeStruct((m, d), BF16),
                   jax.ShapeDtypeStruct((m, LANES), jnp.int32), jax.ShapeDtypeStruct((m, LANES), F32)],
        compiler_params=_cparams(1),
        name="out_proj_router",
    )(na_o, rw_o, hg_o, x, w_out, mod, norm2_g.reshape(1, d), mod, mod, rtw, rtb)


def _final_norm_kernel(x_ref, g_ref, o_ref):
    x = x_ref[...]
    o_ref[...] = x * lax.rsqrt(jnp.mean(x * x, axis=-1, keepdims=True) + RMS_EPS) * g_ref[...]


def _final_norm(x, g):
    m, d = x.shape
    tm = TOKEN_TILE
    return pl.pallas_call(
        _final_norm_kernel,
        grid=(m // tm,),
        in_specs=[pl.BlockSpec((tm, d), lambda i: (i, 0)), pl.BlockSpec((1, d), lambda i: (0, 0))],
        out_specs=pl.BlockSpec((tm, d), lambda i: (i, 0)),
        out_shape=jax.ShapeDtypeStruct((m, d), F32),
        compiler_params=_cparams(1),
        name="final_norm",
    )(x, g.reshape(1, d))


MOE_TM = 512
GU_TN = 1024
DOWN_TN = 2048


def _moe_dispatch(top_e, gates, n_exp):
    n_tok, top_k = top_e.shape
    n_assign = n_tok * top_k
    i32 = jnp.int32
    flat_e = top_e.reshape(-1).astype(i32)
    assign = jnp.arange(n_assign, dtype=i32)
    experts = jnp.arange(n_exp, dtype=i32)
    lookup = lambda table, e: jnp.sum(jnp.where(e[:, None] == experts[None, :], table[None, :], 0), axis=1)
    sorted_e, order, gate_sorted = lax.sort((flat_e, assign, gates.reshape(-1)), num_keys=1)
    counts = jnp.sum((flat_e[:, None] == experts[None, :]).astype(i32), axis=0)
    padded = (counts + MOE_TM - 1) // MOE_TM * MOE_TM
    pad_end = jnp.cumsum(padded)
    pad_start = pad_end - padded
    start = jnp.cumsum(counts) - counts
    shift = pad_start - start
    slot_sorted = assign + lookup(shift, sorted_e)
    _, slot_of_assign = lax.sort((order, slot_sorted), num_keys=1)
    n_blocks = -(-n_assign // MOE_TM) + n_exp
    blk = jnp.arange(n_blocks, dtype=i32)
    block_expert = jnp.sum((pad_end[None, :] <= (blk * MOE_TM)[:, None]).astype(i32), axis=1)
    block_expert = jnp.minimum(block_expert, n_exp - 1)
    n_active = (pad_end[-1] // MOE_TM).astype(i32)
    first = blk * MOE_TM - lookup(shift, block_expert)
    last = lookup(start + counts, block_expert)
    lane = jnp.arange(MOE_TM, dtype=i32)[None, :]
    pos = first[:, None] + lane
    valid = (pos < last[:, None]) & (blk[:, None] < n_active)
    pos = jnp.clip(pos, 0, n_assign - 1)
    slot_token = jnp.where(valid, (order // top_k)[pos], (blk[:, None] * MOE_TM + lane) % n_tok).reshape(-1)
    slot_gate = jnp.where(valid, gate_sorted[pos], 0.0).reshape(-1, 1)
    last_used = jnp.sum(jnp.where(blk == jnp.maximum(n_active - 1, 0), block_expert, 0))
    block_expert = jnp.where(blk < n_active, block_expert, last_used)
    return slot_token, slot_gate, slot_of_assign, block_expert, n_active.reshape(1)


def _expert_changed(be_ref, m):
    return (m == 0) | (be_ref[m] != be_ref[jnp.maximum(m - 1, 0)])


def _moe_gu_kernel(be_ref, na_ref, x_ref, wg_ref, wu_ref, bg_ref, bu_ref, o_ref, wg_s, wu_s):
    m = pl.program_id(1)

    @pl.when(_expert_changed(be_ref, m))
    def _():
        wg_s[...] = wg_ref[...].astype(BF16)
        wu_s[...] = wu_ref[...].astype(BF16)

    @pl.when(m < na_ref[0])
    def _():
        x = x_ref[...]
        gate = jnp.dot(x, wg_s[...], preferred_element_type=F32) + bg_ref[...]
        up = jnp.dot(x, wu_s[...], preferred_element_type=F32) + bu_ref[...]
        gate = jnp.minimum(gate, SWIGLU_LIMIT)
        up = jnp.clip(up, -SWIGLU_LIMIT, SWIGLU_LIMIT)
        o_ref[...] = ((up + 1.0) * gate * _sigmoid(SWIGLU_ALPHA * gate)).astype(BF16)

    @pl.when(m >= na_ref[0])
    def _():
        o_ref[...] = jnp.zeros_like(o_ref)


def _moe_down_kernel(be_ref, na_ref, a_ref, w_ref, b_ref, sg_ref, o_ref, w_s):
    m = pl.program_id(1)

    @pl.when(_expert_changed(be_ref, m))
    def _():
        w_s[...] = w_ref[...].astype(BF16)

    @pl.when(m < na_ref[0])
    def _():
        o_ref[...] = (jnp.dot(a_ref[...], w_s[...], preferred_element_type=F32) + b_ref[...]) * sg_ref[...]

    @pl.when(m >= na_ref[0])
    def _():
        o_ref[...] = jnp.zeros_like(o_ref)


def _moe_experts(xb, slot_gate, block_expert, n_active, w_gu, b_gu, w_down, b_down, layer):
    n_slots, d = xb.shape
    n_blocks = n_slots // MOE_TM
    n_exp, _, two_f = w_gu.shape[1:]
    f = two_f // 2
    nj = f // GU_TN
    act = pl.pallas_call(
        _moe_gu_kernel,
        grid_spec=pltpu.PrefetchScalarGridSpec(
            num_scalar_prefetch=2,
            grid=(nj, n_blocks),
            in_specs=[pl.BlockSpec((MOE_TM, d), lambda j, m, be, na: (m, 0)),
                      pl.BlockSpec((None, None, d, GU_TN), lambda j, m, be, na: (layer, be[m], 0, j)),
                      pl.BlockSpec((None, None, d, GU_TN), lambda j, m, be, na: (layer, be[m], 0, nj + j)),
                      pl.BlockSpec((None, None, 1, GU_TN), lambda j, m, be, na: (layer, be[m], 0, j)),
                      pl.BlockSpec((None, None, 1, GU_TN), lambda j, m, be, na: (layer, be[m], 0, nj + j))],
            out_specs=pl.BlockSpec((MOE_TM, GU_TN), lambda j, m, be, na: (m, j)),
            scratch_shapes=[pltpu.VMEM((d, GU_TN), BF16), pltpu.VMEM((d, GU_TN), BF16)]),
        out_shape=jax.ShapeDtypeStruct((n_slots, f), BF16),
        compiler_params=_cparams(2),
        name="moe_gate_up",
    )(block_expert, n_active, xb, w_gu, w_gu, b_gu.reshape(b_gu.shape[0], n_exp, 1, two_f),
      b_gu.reshape(b_gu.shape[0], n_exp, 1, two_f))
    nd = d // DOWN_TN
    return pl.pallas_call(
        _moe_down_kernel,
        grid_spec=pltpu.PrefetchScalarGridSpec(
            num_scalar_prefetch=2,
            grid=(nd, n_blocks),
            in_specs=[pl.BlockSpec((MOE_TM, f), lambda j, m, be, na: (m, 0)),
                      pl.BlockSpec((None, None, f, DOWN_TN), lambda j, m, be, na: (layer, be[m], 0, j)),
                      pl.BlockSpec((None, None, 1, DOWN_TN), lambda j, m, be, na: (layer, be[m], 0, j)),
                      pl.BlockSpec((MOE_TM, 1), lambda j, m, be, na: (m, 0))],
            out_specs=pl.BlockSpec((MOE_TM, DOWN_TN), lambda j, m, be, na: (m, j)),
            scratch_shapes=[pltpu.VMEM((f, DOWN_TN), BF16)]),
        out_shape=jax.ShapeDtypeStruct((n_slots, d), F32),
        compiler_params=_cparams(2),
        name="moe_down",
    )(block_expert, n_active, act, w_down, b_down.reshape(b_down.shape[0], n_exp, 1, d), slot_gate)


GATHER_ROWS = 512
COMBINE_TM = 128
DMA_PRIORITIES = 2


def _gather_rows_kernel(idx_ref, nxt_ref, src_ref, o_ref, buf, sems):
    i = pl.program_id(0)
    n = pl.num_programs(0)
    slot = i % 2

    def copy(ids, s, t):
        return pltpu.make_async_copy(src_ref.at[pl.ds(ids[0, t], 1)], buf.at[s, pl.ds(t, 1)], sems.at[s])

    def issue(ids, s):
        def body(g, carry):
            for p in range(DMA_PRIORITIES):
                copy(ids, s, g * DMA_PRIORITIES + p).start(priority=p)
            return carry
        lax.fori_loop(0, GATHER_ROWS // DMA_PRIORITIES, body, 0, unroll=4)

    @pl.when(i == 0)
    def _():
        issue(idx_ref, 0)

    @pl.when(i + 1 < n)
    def _():
        issue(nxt_ref, 1 - slot)

    pltpu.make_async_copy(src_ref.at[pl.ds(0, GATHER_ROWS)], buf.at[slot], sems.at[slot]).wait()
    o_ref[...] = buf[slot]


def _as_row_tiles(a):
    return a.reshape(a.shape[0], a.shape[1] // LANES, LANES)


def _gather_rows(src, idx):
    n = idx.shape[0]
    nb = n // GATHER_ROWS
    ids = idx.reshape(nb, 1, GATHER_ROWS)
    smem_blk = lambda f: pl.BlockSpec((None, 1, GATHER_ROWS), f, memory_space=pltpu.SMEM)
    blk = (GATHER_ROWS,) + src.shape[1:]
    return pl.pallas_call(
        _gather_rows_kernel,
        grid=(nb,),
        in_specs=[smem_blk(lambda i: (i, 0, 0)), smem_blk(lambda i: (jnp.minimum(i + 1, nb - 1), 0, 0)),
                  pl.BlockSpec(memory_space=pl.ANY)],
        out_specs=pl.BlockSpec(blk, lambda i: (i, 0, 0)),
        out_shape=jax.ShapeDtypeStruct((n,) + src.shape[1:], src.dtype),
        scratch_shapes=[pltpu.VMEM((2,) + blk, src.dtype), pltpu.SemaphoreType.DMA((2,))],
        compiler_params=_cparams(1),
        name="moe_dispatch_gather",
    )(ids, ids, src)


def _combine_kernel(idx_ref, nxt_ref, yb_ref, x_ref, g2_ref, o_ref, buf, acc, sems, *, top_k):
    i = pl.program_id(0)
    n = pl.num_programs(0)
    slot = i % 2

    def copy(ids, s, t, k):
        return pltpu.make_async_copy(yb_ref.at[pl.ds(ids[0, t * top_k + k], 1)], buf.at[s, k, pl.ds(t, 1)], sems.at[s])

    def issue(ids, s):
        def body(t, carry):
            for k in range(top_k):
                copy(ids, s, t, k).start(priority=k % DMA_PRIORITIES)
            return carry
        lax.fori_loop(0, COMBINE_TM, body, 0, unroll=2)

    @pl.when(i == 0)
    def _():
        issue(idx_ref, 0)

    @pl.when(i + 1 < n)
    def _():
        issue(nxt_ref, 1 - slot)

    for k in range(top_k):
        pltpu.make_async_copy(yb_ref.at[pl.ds(0, COMBINE_TM)], buf.at[slot, k], sems.at[slot]).wait()

    total = buf[slot, 0]
    for k in range(1, top_k):
        total = total + buf[slot, k]
    acc[...] = total
    for c in range(acc.shape[1]):
        cs = slice(c * LANES, (c + 1) * LANES)
        o_ref[:, cs] = x_ref[:, cs] + g2_ref[:, cs] * acc[:, c, :]


def _moe_combine(yb, slot_of_assign, x, mod, mod_row, top_k):
    m, d = x.shape
    tm = COMBINE_TM
    nt = m // tm
    row = mod_row(tm)
    ids = slot_of_assign.reshape(nt, 1, tm * top_k)
    smem_blk = lambda f: pl.BlockSpec((None, 1, tm * top_k), f, memory_space=pltpu.SMEM)
    return pl.pallas_call(
        functools.partial(_combine_kernel, top_k=top_k),
        grid=(nt,),
        in_specs=[smem_blk(lambda i: (i, 0, 0)), smem_blk(lambda i: (jnp.minimum(i + 1, nt - 1), 0, 0)),
                  pl.BlockSpec(memory_space=pl.ANY),
                  pl.BlockSpec((tm, d), lambda i: (i, 0)),
                  pl.BlockSpec((None, 1, d), lambda i: (row(i), 0, 5))],
        out_specs=pl.BlockSpec((tm, d), lambda i: (i, 0)),
        out_shape=jax.ShapeDtypeStruct((m, d), F32),
        scratch_shapes=[pltpu.VMEM((2, top_k, tm) + yb.shape[1:], F32), pltpu.VMEM((tm,) + yb.shape[1:], F32),
                        pltpu.SemaphoreType.DMA((2,))],
        compiler_params=_cparams(1),
        name="moe_combine",
    )(ids, ids, yb, x, mod)


def _moe(h2, idx, gate, x, mod, mod_row, w_gu, b_gu, w_down, b_down, layer):
    n_exp = w_gu.shape[1]
    slot_token, slot_gate, slot_of_assign, block_expert, n_active = _moe_dispatch(idx[:, :TOP_K], gate[:, :TOP_K], n_exp)
    xb = _gather_rows(_as_row_tiles(h2), slot_token).reshape(slot_token.shape[0], h2.shape[1])
    yb = _moe_experts(xb, slot_gate, block_expert, n_active, w_gu, b_gu, w_down, b_down, layer)
    return _moe_combine(_as_row_tiles(yb), slot_of_assign, x, mod, mod_row, TOP_K)


def _pad_w_in(w):
    wb = w.astype(BF16)
    cut = 3 * NA_WIDTH + RW_SHIFT_WIDTH
    na = 3 * NA_WIDTH
    return jnp.concatenate([wb[:, na:cut], jnp.zeros((w.shape[0], RW_PAD), BF16), wb[:, :na], wb[:, cut:]], axis=1)


def kernel(x_prompt, x_sample, c, cache_k, cache_v, state_rwkv_fwd, state_rwkv_bwd, state_hgrn_fwd, state_hgrn_bwd, c_ctx, norm1_g, norm2_g, w_ada, b_ada, w_in, w_out, na_rpb, rw_mu, rw_w0, rw_w2, rw_a0, rw_a2, rw_g2, rw_k_k, rw_k_a, rw_r_k, rw_ln_g, rw_ln_b, hg_lb_raw, hg_norm_g, router_w, router_b, exp_w_gu, exp_b_gu, exp_w_down, exp_b_down, final_g):
    n_p, t_p, d = x_prompt.shape
    n_s, t_s, _ = x_sample.shape
    depth = w_in.shape[0]
    mp, ms = n_p * t_p, n_s * t_s
    x = jnp.concatenate([x_prompt.reshape(mp, d), x_sample.reshape(ms, d)], axis=0)
    n_mod_rows = 8
    cond = jnp.concatenate([c_ctx[None], c, jnp.zeros((n_mod_rows - 1 - n_s, d), F32)], axis=0)
    mod_row = lambda tm: _mod_row_fn(tm, mp, t_s)
    p_lb = jax.nn.softmax(hg_lb_raw.astype(F32), axis=0)
    hg_lb = jnp.cumsum(p_lb, axis=0) - p_lb[:1]
    ctx_out = []
    for l in range(depth):
        mod2d = _ada(cond, w_ada, b_ada, l)
        mod = mod2d.reshape(n_mod_rows, 1, 6 * d)
        proj = _in_proj(x, norm1_g[l], mod, _pad_w_in(w_in[l]), mod_row)
        na_p, k_ctx, v_ctx = _ctx_attention(proj, n_p, t_p)
        na_s = _na_attention(proj, mp, n_s, t_s, cache_k, cache_v, l, _na_bias_table(na_rpb[l], t_s // GRID_W))
        na_o = jnp.concatenate([na_p, na_s], axis=0)
        lw = dict(rw_mu=rw_mu[l], rw_w0=rw_w0[l], rw_w2=rw_w2[l], rw_a0=rw_a0[l], rw_a2=rw_a2[l], rw_g2=rw_g2[l],
                  rw_k_k=rw_k_k[l], rw_k_a=rw_k_a[l], rw_r_k=rw_r_k[l], rw_ln_g=rw_ln_g[l], rw_ln_b=rw_ln_b[l])
        rw_groups = [(0, n_p, t_p, jnp.zeros((2, n_p, RW_HEADS, RW_HEAD_DIM, RW_HEAD_DIM), F32)),
                     (mp, n_s, t_s, jnp.stack([state_rwkv_fwd[:, l], state_rwkv_bwd[:, l]]))]
        rw_o, rw_states = _rwkv_mixer(proj, rw_groups, lw)
        hg_groups = [(0, n_p, t_p, jnp.zeros((2, n_p, HG_HEADS, HG_KEY_DIM, HG_VAL_DIM), F32)),
                     (mp, n_s, t_s, jnp.stack([state_hgrn_fwd[:, l], state_hgrn_bwd[:, l]]))]
        hg_o, hg_states = _hgrn_mixer(proj, hg_groups, hg_lb[l], hg_norm_g[l])
        x, h2, idx, gate = _out_proj(na_o, rw_o, hg_o, x, w_out[l].astype(BF16), mod, norm2_g[l], router_w[l],
                                     router_b[l], mod_row)
        x = _moe(h2, idx, gate, x, mod, mod_row, exp_w_gu, exp_b_gu, exp_w_down, exp_b_down, l)
        ctx_out.append((k_ctx, v_ctx, rw_states[0][0], rw_states[0][1], hg_states[0][0], hg_states[0][1]))
    y = _final_norm(x, final_g)
    outs = [y[:mp].reshape(n_p, t_p, d), y[mp:].reshape(n_s, t_s, d)]
    outs += [jnp.stack([t[i] for t in ctx_out], axis=1) for i in range(6)]
    return tuple(outs)
```

```python
import functools

import numpy as np
import jax
import jax.numpy as jnp
from jax import lax
from jax.experimental import pallas as pl
from jax.experimental.pallas import tpu as pltpu

F32 = jnp.float32
BF16 = jnp.bfloat16

D_MODEL = 2048
GRID_W = 64
NA_HEADS = 16
NA_HEAD_DIM = 64
NA_WIDTH = NA_HEADS * NA_HEAD_DIM
NA_KH = 8
NA_KW = 16
RW_HEADS = 8
RW_HEAD_DIM = 64
RW_WIDTH = RW_HEADS * RW_HEAD_DIM
RW_DECAY_LORA = 64
RW_ICLR_LORA = 64
RW_GATE_LORA = 128
RW_SHIFT_WIDTH = 3 * RW_WIDTH + 2 * RW_DECAY_LORA + 2 * RW_ICLR_LORA + RW_GATE_LORA
RW_GN_EPS = 64e-5
HG_HEADS = 4
HG_KEY_DIM = 128
HG_VAL_DIM = 128
HG_WIDTH = HG_HEADS * HG_VAL_DIM
HG_MIN_F = 1e-6
IN_WIDTH = 3 * NA_WIDTH + RW_SHIFT_WIDTH + 5 * HG_WIDTH
N_EXPERTS = 32
TOP_K = 4
D_EXPERT = 2048
SWIGLU_LIMIT = 7.0
SWIGLU_ALPHA = 1.702
RMS_EPS = 1e-6
NEG_INF = -1e30

LANES = 128
PCOL_RW = 0
RW_PAD = -RW_SHIFT_WIDTH % HG_WIDTH
PCOL_Q = RW_SHIFT_WIDTH + RW_PAD
PCOL_K = PCOL_Q + NA_WIDTH
PCOL_V = PCOL_K + NA_WIDTH
PCOL_HQ = PCOL_V + NA_WIDTH
PCOL_HFF = PCOL_HQ + HG_WIDTH
PCOL_HFB = PCOL_HFF + HG_WIDTH
PCOL_HI = PCOL_HFB + HG_WIDTH
PCOL_HG = PCOL_HI + HG_WIDTH
IN_WIDTH_PAD = PCOL_HG + HG_WIDTH
SCAN_CHUNK = 64
TOKEN_TILE = 256
VMEM_LIMIT = 56 * 1024 * 1024


def _cparams(n_axes, vmem=VMEM_LIMIT):
    return pltpu.CompilerParams(dimension_semantics=("arbitrary",) * n_axes, vmem_limit_bytes=vmem)


def _dot(a, b):
    return jnp.dot(a.astype(BF16), b.astype(BF16), preferred_element_type=F32)


def _dot_nt(a, b):
    return lax.dot_general(a.astype(BF16), b.astype(BF16), (((1,), (1,)), ((), ())), preferred_element_type=F32)


def _dot_tn(a, b):
    return lax.dot_general(a.astype(BF16), b.astype(BF16), (((0,), (0,)), ((), ())), preferred_element_type=F32)


def _split3(x):
    hi = x.astype(BF16)
    r1 = x - hi.astype(F32)
    mid = r1.astype(BF16)
    lo = (r1 - mid.astype(F32)).astype(BF16)
    return hi, mid, lo


def _dot01_left(m01, x):
    hi, mid, lo = _split3(x)
    d = lambda p: jnp.dot(m01, p, preferred_element_type=F32)
    return d(hi) + (d(mid) + d(lo))


def _dot01_right(x, m01):
    hi, mid, lo = _split3(x)
    d = lambda p: jnp.dot(p, m01, preferred_element_type=F32)
    return d(hi) + (d(mid) + d(lo))


def _sigmoid(x):
    return 1.0 / (1.0 + jnp.exp(-x))


def _softplus(x):
    return jnp.maximum(x, 0.0) + jnp.log(1.0 + jnp.exp(-jnp.abs(x)))


def _head_sum_matrix(width, head_dim):
    i = np.arange(width)
    return jnp.asarray((i[:, None] // head_dim) == (i[None, :] // head_dim), dtype=BF16)


def _order_masks(n, reverse):
    row = lax.broadcasted_iota(jnp.int32, (n, n), 0)
    col = lax.broadcasted_iota(jnp.int32, (n, n), 1)
    diff = col - row if reverse else row - col
    return diff >= 0, diff > 0


def _rwkv_prep_kernel(x_ref, prev_ref, next_ref, mu_ref, w0_ref, w2_ref, a0_ref, a2_ref, g2_ref, kkw_ref, kaw_ref,
                      hsum_ref, r_ref, v_ref, kkn_ref, g_ref, lw_ref, kd_ref, av_ref, *, seq_groups):
    tm = x_ref.shape[0]
    t0 = pl.program_id(0) * tm
    (off0, len0), (off1, len1) = seq_groups
    in_first = t0 < off1
    pos = jnp.where(in_first, lax.rem(t0 - off0, len0), lax.rem(jnp.maximum(t0 - off1, 0), len1))
    seq_len = jnp.where(in_first, len0, len1)
    has_prev = (pos != 0).astype(F32)
    has_next = (pos + tm != seq_len).astype(F32)
    x = x_ref[...]
    row = lax.broadcasted_iota(jnp.int32, x.shape, 0)
    before = jnp.where(row == 0, prev_ref[7:8, :] * has_prev, pltpu.roll(x, 1, 0))
    after = jnp.where(row == tm - 1, next_ref[0:1, :] * has_next, pltpu.roll(x, tm - 1, 0))
    x = x + (0.5 * (before + after) - x) * mu_ref[...]
    r_ref[...] = x[:, 0:RW_WIDTH]
    v_ref[...] = x[:, 2 * RW_WIDTH:3 * RW_WIDTH]
    k = x[:, RW_WIDTH:2 * RW_WIDTH]
    o = 3 * RW_WIDTH
    wd = (x[:, o:o + 64], x[:, o + 64:o + 128])
    ad = (x[:, o + 128:o + 192], x[:, o + 192:o + 256])
    gd = x[:, o + 256:o + 384]
    g_ref[...] = _dot(_sigmoid(gd), g2_ref[...])
    kk = k * kkw_ref[...]
    ssq = _dot01_right(kk * kk, hsum_ref[...])
    kkn_ref[...] = kk / jnp.maximum(jnp.sqrt(ssq), 1e-12)
    for d in range(2):
        wl = -_softplus(-(w0_ref[d] + _dot(jnp.tanh(wd[d]), w2_ref[d]))) - 0.5
        lw_ref[d] = -jnp.exp(wl)
        a = _sigmoid(a0_ref[d] + _dot(ad[d], a2_ref[d]))
        kd_ref[d] = k * (1.0 + (a - 1.0) * kaw_ref[...])
        av_ref[d] = a


def _rwkv_scan_kernel(*refs, n_chunks):
    ins = (refs[0:6], refs[6:12])
    s0_ref, yf_ref, yb_ref, sout_ref, s_scr = refs[12:]
    c = pl.program_id(1)
    n = SCAN_CHUNK

    @pl.when(c == 0)
    def _():
        s_scr[...] = s0_ref[...]

    masks = [_order_masks(n, False), _order_masks(n, True)]
    incl01 = [m[0].astype(BF16) for m in masks]
    ch = [(d, h) for d in range(2) for h in range(RW_HEADS)]
    cs = range(len(ch))
    m_incl = [masks[d][0] for d, _ in ch]
    m_strict = [masks[d][1] for d, _ in ch]
    col = lambda j: [ins[d][j][:, h * RW_HEAD_DIM:(h + 1) * RW_HEAD_DIM] for d, h in ch]
    r, v, kkn, lw, kd, av = (col(j) for j in range(6))
    cum = [_dot01_left(incl01[ch[i][0]], lw[i]) for i in cs]
    p_tot = [jnp.exp(jnp.sum(lw[i], axis=0, keepdims=True)) for i in cs]
    p_inv = [jnp.exp(-cum[i]) for i in cs]
    a_t = [-kkn[i] * jnp.exp(cum[i] - lw[i]) for i in cs]
    b_t = [kkn[i] * av[i] * p_inv[i] for i in cs]
    k_t = [kd[i] * p_inv[i] for i in cs]
    r_t = [r[i] * jnp.exp(cum[i]) for i in cs]
    a_ab = [jnp.where(m_strict[i], _dot_nt(a_t[i], b_t[i]), 0.0) for i in cs]
    a_ak = [jnp.where(m_strict[i], _dot_nt(a_t[i], k_t[i]), 0.0) for i in cs]
    a_rb = [jnp.where(m_incl[i], _dot_nt(r_t[i], b_t[i]), 0.0) for i in cs]
    a_rk = [jnp.where(m_incl[i], _dot_nt(r_t[i], k_t[i]), 0.0) for i in cs]
    t_m = list(a_ab)
    n_pow = list(a_ab)
    for _ in range(int(np.log2(n)) - 1):
        n_pow = [_dot(n_pow[i], n_pow[i]) for i in cs]
        t_m = [t_m[i] + n_pow[i] + _dot(t_m[i], n_pow[i]) for i in cs]
    x1 = [_dot(a_ak[i], v[i]) for i in cs]
    w1 = [x1[i] + _dot(t_m[i], x1[i]) for i in cs]
    w2 = [a_t[i] + _dot(t_m[i], a_t[i]) for i in cs]
    y0 = [_dot(a_rk[i], v[i]) + _dot(a_rb[i], w1[i]) for i in cs]
    q = [r_t[i] + _dot(a_rb[i], w2[i]) for i in cs]
    g_m = [_dot_tn(w2[i], b_t[i]) for i in cs]
    h_m = [_dot_tn(w1[i], b_t[i]) + _dot_tn(v[i], k_t[i]) for i in cs]
    s0 = [s_scr[d, h] for d, h in ch]
    ys = [y0[i] + _dot_nt(q[i], s0[i]) for i in cs]
    s_new = [(s0[i] + _dot(s0[i], g_m[i]) + h_m[i]) * p_tot[i] for i in cs]
    for i, (d, h) in enumerate(ch):
        s_scr[d, h] = s_new[i]
    yf_ref[...] = jnp.concatenate(ys[:RW_HEADS], axis=-1)
    yb_ref[...] = jnp.concatenate(ys[RW_HEADS:], axis=-1)

    @pl.when(c == n_chunks - 1)
    def _():
        sout_ref[...] = s_scr[...]


def _rwkv_post_kernel(yf_ref, yb_ref, r_ref, v_ref, kd_ref, g_ref, rk_ref, lng_ref, lnb_ref, hsum_ref, o_ref):
    r = r_ref[...]
    v = v_ref[...]
    y = yf_ref[...] + yb_ref[...]
    hsum = hsum_ref[...]
    inv_n = 1.0 / RW_HEAD_DIM
    mu = _dot01_right(y, hsum) * inv_n
    yc = y - mu
    var = _dot01_right(yc * yc, hsum) * inv_n
    yn = yc * lax.rsqrt(var + RW_GN_EPS) * lng_ref[...] + lnb_ref[...]
    bonus = _dot01_right(r * (0.5 * (kd_ref[0] + kd_ref[1])) * rk_ref[...], hsum) * v
    o_ref[...] = (yn + bonus) * g_ref[...]


def _rwkv_mixer(proj, groups, lw):
    m = proj.shape[0]
    tm = TOKEN_TILE
    sub = 8
    assert PCOL_RW == 0 and len(groups) == 2 and all(g[2] % tm == 0 for g in groups)
    hsum = _head_sum_matrix(RW_WIDTH, RW_HEAD_DIM)
    row = lambda a: a.reshape(1, -1)
    full = lambda shape: pl.BlockSpec(shape, lambda i: (0,) * len(shape))
    tok = lambda w: pl.BlockSpec((tm, w), lambda i: (i, 0))
    tok2 = pl.BlockSpec((2, tm, RW_WIDTH), lambda i: (0, i, 0))
    d2 = jax.ShapeDtypeStruct((2, m, RW_WIDTH), F32)
    d1 = jax.ShapeDtypeStruct((m, RW_WIDTH), F32)
    per = tm // sub
    prev_blk = pl.BlockSpec((sub, RW_SHIFT_WIDTH), lambda i: (jnp.maximum(i * per - 1, 0), 0))
    next_blk = pl.BlockSpec((sub, RW_SHIFT_WIDTH), lambda i: (jnp.minimum((i + 1) * per, m // sub - 1), 0))
    xs_r, xs_v, kkn, g, lwd, kd, av = pl.pallas_call(
        functools.partial(_rwkv_prep_kernel, seq_groups=tuple((g[0], g[2]) for g in groups)),
        grid=(m // tm,),
        in_specs=[tok(RW_SHIFT_WIDTH), prev_blk, next_blk, full((1, RW_SHIFT_WIDTH)),
                  full((2, 1, RW_WIDTH)), full((2, RW_DECAY_LORA, RW_WIDTH)),
                  full((2, 1, RW_WIDTH)), full((2, RW_ICLR_LORA, RW_WIDTH)), full((RW_GATE_LORA, RW_WIDTH)),
                  full((1, RW_WIDTH)), full((1, RW_WIDTH)), full((RW_WIDTH, RW_WIDTH))],
        out_specs=[tok(RW_WIDTH), tok(RW_WIDTH), tok(RW_WIDTH), tok(RW_WIDTH), tok2, tok2, tok2],
        out_shape=[d1, d1, d1, d1, d2, d2, d2],
        compiler_params=_cparams(1),
        name="rwkv_prep",
    )(proj, proj, proj, row(lw['rw_mu']), lw['rw_w0'].reshape(2, 1, RW_WIDTH), lw['rw_w2'],
      lw['rw_a0'].reshape(2, 1, RW_WIDTH), lw['rw_a2'], lw['rw_g2'], row(lw['rw_k_k']), row(lw['rw_k_a']), hsum)

    n = SCAN_CHUNK
    yfs, ybs, states = [], [], []
    for row_off, n_seq, seq_len, s0 in groups:
        nc = seq_len // n
        base = row_off // n
        chunk = lambda b, c, d, nc=nc: b * nc + (nc - 1 - c if d else c)
        x_spec = lambda d, col, base=base, chunk=chunk: pl.BlockSpec(
            (n, RW_WIDTH), lambda b, c: (base + chunk(b, c, d), col))
        d_spec = lambda d, base=base, chunk=chunk: pl.BlockSpec(
            (None, n, RW_WIDTH), lambda b, c: (d, base + chunk(b, c, d), 0))
        y_spec = lambda d, chunk=chunk: pl.BlockSpec((n, RW_WIDTH), lambda b, c: (chunk(b, c, d), 0))
        s_spec = pl.BlockSpec((2, None, RW_HEADS, RW_HEAD_DIM, RW_HEAD_DIM), lambda b, c: (0, b, 0, 0, 0))
        per_dir = lambda d: [x_spec(d, 0), x_spec(d, 0), x_spec(d, 0), d_spec(d), d_spec(d), d_spec(d)]
        y_shape = jax.ShapeDtypeStruct((n_seq * seq_len, RW_WIDTH), F32)
        yf, yb, s_out = pl.pallas_call(
            functools.partial(_rwkv_scan_kernel, n_chunks=nc),
            grid=(n_seq, nc),
            in_specs=per_dir(0) + per_dir(1) + [s_spec],
            out_specs=[y_spec(0), y_spec(1), s_spec],
            out_shape=[y_shape, y_shape,
                       jax.ShapeDtypeStruct((2, n_seq, RW_HEADS, RW_HEAD_DIM, RW_HEAD_DIM), F32)],
            scratch_shapes=[pltpu.VMEM((2, RW_HEADS, RW_HEAD_DIM, RW_HEAD_DIM), F32)],
            compiler_params=_cparams(2),
            name="rwkv_scan",
        )(*([xs_r, xs_v, kkn, lwd, kd, av] * 2), s0)
        yfs.append(yf)
        ybs.append(yb)
        states.append(s_out)
    yf = jnp.concatenate(yfs, axis=0)
    yb = jnp.concatenate(ybs, axis=0)

    out = pl.pallas_call(
        _rwkv_post_kernel,
        grid=(m // tm,),
        in_specs=[tok(RW_WIDTH), tok(RW_WIDTH), tok(RW_WIDTH), tok(RW_WIDTH), tok2, tok(RW_WIDTH),
                  full((1, RW_WIDTH)), full((1, RW_WIDTH)), full((1, RW_WIDTH)), full((RW_WIDTH, RW_WIDTH))],
        out_specs=tok(RW_WIDTH),
        out_shape=d1,
        compiler_params=_cparams(1),
        name="rwkv_post",
    )(yf, yb, xs_r, xs_v, kd, g, row(lw['rw_r_k']), row(lw['rw_ln_g']), row(lw['rw_ln_b']), hsum)
    return out, states


HG_LEVELS = (32, 16, 8, 4, 2, 1)


def _hgrn_consts():
    n = SCAN_CHUNK
    cums, masks = [], []
    for rev in (False, True):
        p = np.arange(n)[::-1] if rev else np.arange(n)
        pt, pj = p[:, None], p[None, :]
        rows = [pj <= pt, pj > pt]
        rows += [((pt // m) % 2 == 1) & (pj // m == pt // m) & (pj <= pt) for m in HG_LEVELS]
        rows += [((pt // m) % 2 == 0) & (pj // m == pt // m) & (pj > pt) for m in HG_LEVELS]
        cums.append(np.concatenate(rows, 0))
        mk = [pt == pj]
        mk += [(pt // (2 * m) == pj // (2 * m)) & ((pt // m) % 2 == 1) & ((pj // m) % 2 == 0) for m in HG_LEVELS]
        masks.append(np.stack(mk))
    return jnp.asarray(np.stack(cums), BF16), jnp.asarray(np.stack(masks), F32)


def _hgrn_scan_kernel(qf_ref, zf_ref, vf_ref, qb_ref, zb_ref, vb_ref, lb_ref, cm_ref, mask_ref, s0_ref,
                      of_ref, ob_ref, sout_ref, s_scr, *, n_chunks):
    c = pl.program_id(1)
    n = SCAN_CHUNK
    nl = len(HG_LEVELS)
    ins = ((qf_ref, zf_ref, vf_ref), (qb_ref, zb_ref, vb_ref))

    @pl.when(c == 0)
    def _():
        s_scr[...] = s0_ref[...]

    ones = jnp.ones((n, HG_VAL_DIM), BF16)
    tn01 = lambda p: lax.dot_general(p, ones, (((0,), (0,)), ((), ())), preferred_element_type=F32)
    ch = [(d, h) for d in range(2) for h in range(HG_HEADS)]
    cs = range(len(ch))
    col = lambda j: [ins[d][j][:, h * HG_KEY_DIM:(h + 1) * HG_KEY_DIM] for d, h in ch]
    hq, z, v = col(0), col(1), col(2)
    lb = [lb_ref[d, :, h * HG_KEY_DIM:(h + 1) * HG_KEY_DIM] for d, h in ch]
    q = [hq[i] * _sigmoid(hq[i]) for i in cs]
    lf = [jnp.log(jnp.maximum(lb[i] + (1.0 - lb[i]) * _sigmoid(z[i]), HG_MIN_F)) for i in cs]
    k = [(1.0 - lb[i]) * _sigmoid(-z[i]) for i in cs]
    parts = [_split3(lf[i]) for i in cs]
    ex = [jnp.exp(sum(jnp.dot(cm_ref[ch[i][0]], p, preferred_element_type=F32) for p in parts[i][::-1])) for i in cs]
    p_col = [jnp.exp(sum(tn01(p) for p in parts[i][::-1])) for i in cs]
    att = [mask_ref[ch[i][0], 0] * _dot_nt(q[i], k[i]) for i in cs]
    for lv in range(nl):
        lo_q, lo_k = (2 + lv) * n, (2 + nl + lv) * n
        lvl = [_dot_nt(q[i] * ex[i][lo_q:lo_q + n], k[i] * ex[i][lo_k:lo_k + n]) for i in cs]
        att = [att[i] + mask_ref[ch[i][0], lv + 1] * lvl[i] for i in cs]
    s0 = [s_scr[d, h] for d, h in ch]
    outs = [_dot(q[i] * ex[i][0:n], s0[i]) + _dot(att[i], v[i]) for i in cs]
    s_new = [s0[i] * p_col[i] + _dot_tn(k[i] * ex[i][n:2 * n], v[i]) for i in cs]
    for i, (d, h) in enumerate(ch):
        s_scr[d, h] = s_new[i]
    of_ref[...] = jnp.concatenate(outs[:HG_HEADS], axis=-1)
    ob_ref[...] = jnp.concatenate(outs[HG_HEADS:], axis=-1)

    @pl.when(c == n_chunks - 1)
    def _():
        sout_ref[...] = s_scr[...]


def _hgrn_post_kernel(of_ref, ob_ref, g_ref, gain_ref, hsum_ref, out_ref):
    o = of_ref[...] + ob_ref[...]
    ms = _dot01_right(o * o, hsum_ref[...]) * (1.0 / HG_VAL_DIM)
    on = o * lax.rsqrt(ms + RMS_EPS) * gain_ref[...]
    g = g_ref[...]
    out_ref[...] = on * (g * _sigmoid(g))


def _hgrn_mixer(proj, groups, lb, gain):
    m = proj.shape[0]
    n = SCAN_CHUNK
    cm, masks = _hgrn_consts()
    cq, cff, ci, cg = (col // HG_WIDTH for col in (PCOL_HQ, PCOL_HFF, PCOL_HI, PCOL_HG))
    ofs, obs, states = [], [], []
    for row_off, n_seq, seq_len, s0 in groups:
        nc = seq_len // n
        base = row_off // n
        chunk = lambda b, c, d, nc=nc: b * nc + (nc - 1 - c if d else c)
        x_spec = lambda d, col, base=base, chunk=chunk: pl.BlockSpec(
            (n, HG_WIDTH), lambda b, c: (base + chunk(b, c, d), col))
        o_spec = lambda d, chunk=chunk: pl.BlockSpec((n, HG_WIDTH), lambda b, c: (chunk(b, c, d), 0))
        s_spec = pl.BlockSpec((2, None, HG_HEADS, HG_KEY_DIM, HG_VAL_DIM), lambda b, c: (0, b, 0, 0, 0))
        whole = lambda a: pl.BlockSpec(a.shape, lambda b, c: (0,) * a.ndim)
        lb3 = lb.reshape(2, 1, HG_WIDTH)
        o_shape = jax.ShapeDtypeStruct((n_seq * seq_len, HG_WIDTH), F32)
        of, ob, s_out = pl.pallas_call(
            functools.partial(_hgrn_scan_kernel, n_chunks=nc),
            grid=(n_seq, nc),
            in_specs=[x_spec(0, cq), x_spec(0, cff), x_spec(0, ci), x_spec(1, cq), x_spec(1, cff + 1), x_spec(1, ci),
                      whole(lb3), whole(cm), whole(masks), s_spec],
            out_specs=[o_spec(0), o_spec(1), s_spec],
            out_shape=[o_shape, o_shape,
                       jax.ShapeDtypeStruct((2, n_seq, HG_HEADS, HG_KEY_DIM, HG_VAL_DIM), F32)],
            scratch_shapes=[pltpu.VMEM((2, HG_HEADS, HG_KEY_DIM, HG_VAL_DIM), F32)],
            compiler_params=_cparams(2),
            name="hgrn_scan",
        )(proj, proj, proj, proj, proj, proj, lb3, cm, masks, s0)
        ofs.append(of)
        obs.append(ob)
        states.append(s_out)
    of = jnp.concatenate(ofs, axis=0)
    ob = jnp.concatenate(obs, axis=0)

    tm = TOKEN_TILE
    tok = pl.BlockSpec((tm, HG_WIDTH), lambda i: (i, 0))
    out = pl.pallas_call(
        _hgrn_post_kernel,
        grid=(m // tm,),
        in_specs=[tok, tok,
                  pl.BlockSpec((tm, HG_WIDTH), lambda i: (i, cg)),
                  pl.BlockSpec((1, HG_WIDTH), lambda i: (0, 0)),
                  pl.BlockSpec((HG_WIDTH, HG_WIDTH), lambda i: (0, 0))],
        out_specs=pl.BlockSpec((tm, HG_WIDTH), lambda i: (i, 0)),
        out_shape=jax.ShapeDtypeStruct((m, HG_WIDTH), F32),
        compiler_params=_cparams(1),
        name="hgrn_post",
    )(of, ob, proj, gain.reshape(1, HG_WIDTH), _head_sum_matrix(HG_WIDTH, HG_VAL_DIM))
    return out, states


NA_HEADS_PER_STEP = LANES // NA_HEAD_DIM
NA_ROWS_PER_ITER = 16


CTX_HEADS_PER_STEP = 8


def _ctx_attn_kernel(q_ref, k_ref, v_ref, o_ref, kc_ref, vc_ref):
    scale = NA_HEAD_DIM ** -0.5
    hs = range(CTX_HEADS_PER_STEP)
    sl = [slice(h * NA_HEAD_DIM, (h + 1) * NA_HEAD_DIM) for h in hs]
    k = [k_ref[:, s] for s in sl]
    v = [v_ref[:, s] for s in sl]
    for h in hs:
        kc_ref[h] = k[h]
        vc_ref[h] = v[h]
    s = [_dot_nt(q_ref[:, sl[h]], k[h]) * scale for h in hs]
    p = [jnp.exp(s[h] - jnp.max(s[h], axis=-1, keepdims=True)) for h in hs]
    o = [_dot(p[h], v[h]) / jnp.sum(p[h], axis=-1, keepdims=True) for h in hs]
    o_ref[...] = jnp.concatenate(o, axis=-1)


def _ctx_attention(proj, n_seq, seq_len):
    cols = CTX_HEADS_PER_STEP * NA_HEAD_DIM
    nb = NA_WIDTH // cols
    blk = lambda off: pl.BlockSpec((seq_len, cols), lambda b, j: (b, off + j))
    c_spec = pl.BlockSpec((None, CTX_HEADS_PER_STEP, seq_len, NA_HEAD_DIM), lambda b, j: (b, j, 0, 0))
    c_shape = jax.ShapeDtypeStruct((n_seq, NA_HEADS, seq_len, NA_HEAD_DIM), F32)
    return pl.pallas_call(
        _ctx_attn_kernel,
        grid=(n_seq, nb),
        in_specs=[blk(PCOL_Q // cols), blk(PCOL_K // cols), blk(PCOL_V // cols)],
        out_specs=[pl.BlockSpec((seq_len, cols), lambda b, j: (b, j)), c_spec, c_spec],
        out_shape=[jax.ShapeDtypeStruct((n_seq * seq_len, NA_WIDTH), F32), c_shape, c_shape],
        compiler_params=_cparams(2),
        name="ctx_attention",
    )(proj, proj, proj)


def _na_bias_table(rpb, rows):
    assert rows >= NA_KH
    o = np.arange(NA_KH)[:, None]
    i = np.arange(NA_KH)[None, :]
    dr = i - o + (NA_KH - 1)
    cq = np.arange(GRID_W)
    ck = np.arange(GRID_W)
    col_start = np.clip(cq - NA_KW // 2, 0, GRID_W - NA_KW)
    col_mask = (ck[None, :] >= col_start[:, None]) & (ck[None, :] < col_start[:, None] + NA_KW)
    dc = np.clip(ck[None, :] - cq[:, None], -(NA_KW - 1), NA_KW - 1) + (NA_KW - 1)
    onehot = jnp.asarray(dc.reshape(-1)[:, None] == np.arange(2 * NA_KW - 1)[None, :], F32)
    full = jnp.einsum('hrj,pj->hrp', rpb.astype(F32), onehot, precision=lax.Precision.HIGHEST)
    full = jnp.where(col_mask.reshape(-1)[None, None, :], full, NEG_INF).reshape(rpb.shape[0], -1, GRID_W, GRID_W)
    lo = lambda off: int(dr[off, 0])
    tab = jnp.stack([full[:, lo(off):lo(off) + NA_KH] for off in range(NA_KH)])
    return tab.transpose(0, 1, 3, 2, 4).reshape(NA_KH, rpb.shape[0], GRID_W, NA_KH * GRID_W)


def _na_kernel(q_ref, k_ref, v_ref, ck_ref, cv_ref, bias_ref, o_ref, *, rows):
    scale = NA_HEAD_DIM ** -0.5
    n_win = NA_KH * GRID_W

    def row_group(g, carry):
        ch = []
        for i in range(NA_ROWS_PER_ITER):
            r = g * NA_ROWS_PER_ITER + i
            r0 = jnp.clip(r - NA_KH // 2, 0, rows - NA_KH)
            q_at = pl.multiple_of(r * GRID_W, GRID_W)
            k_at = pl.multiple_of(r0 * GRID_W, GRID_W)
            for h in range(NA_HEADS_PER_STEP):
                ch.append((q_at, k_at, r - r0, h, slice(h * NA_HEAD_DIM, (h + 1) * NA_HEAD_DIM)))
        q = [q_ref[pl.ds(q_at, GRID_W), sl] for q_at, _, _, _, sl in ch]
        s_w = [_dot_nt(q[i], k_ref[pl.ds(k_at, n_win), sl]) * scale + bias_ref[off, h]
               for i, (_, k_at, off, h, sl) in enumerate(ch)]
        s_c = [_dot_nt(q[i], ck_ref[h]) * scale for i, (_, _, _, h, _) in enumerate(ch)]
        m = [jnp.maximum(jnp.max(a, axis=-1, keepdims=True), jnp.max(b, axis=-1, keepdims=True))
             for a, b in zip(s_w, s_c)]
        p_w = [jnp.exp(a - mm) for a, mm in zip(s_w, m)]
        p_c = [jnp.exp(b - mm) for b, mm in zip(s_c, m)]
        den = [jnp.sum(a, axis=-1, keepdims=True) + jnp.sum(b, axis=-1, keepdims=True) for a, b in zip(p_w, p_c)]
        o = [(_dot(p_w[i], v_ref[pl.ds(k_at, n_win), sl]) + _dot(p_c[i], cv_ref[h])) / den[i]
             for i, (_, k_at, _, h, sl) in enumerate(ch)]
        for i, (q_at, _, _, _, sl) in enumerate(ch):
            o_ref[pl.ds(q_at, GRID_W), sl] = o[i]
        return carry

    lax.fori_loop(0, rows // NA_ROWS_PER_ITER, row_group, 0)


def _na_attention(proj, row_off, n_seq, seq_len, cache_k, cache_v, layer, bias):
    assert row_off % seq_len == 0
    nb = NA_WIDTH // LANES
    rows = seq_len // GRID_W
    assert rows % NA_ROWS_PER_ITER == 0
    rb = row_off // seq_len
    past = cache_k.shape[3]
    blk = lambda off: pl.BlockSpec((seq_len, LANES), lambda b, j: (rb + b, off + j))
    c_spec = pl.BlockSpec((None, None, NA_HEADS_PER_STEP, past, NA_HEAD_DIM), lambda b, j: (b, layer, j, 0, 0))
    return pl.pallas_call(
        functools.partial(_na_kernel, rows=rows),
        grid=(n_seq, nb),
        in_specs=[blk(PCOL_Q // LANES), blk(PCOL_K // LANES), blk(PCOL_V // LANES), c_spec, c_spec,
                  pl.BlockSpec((NA_KH, NA_HEADS_PER_STEP, GRID_W, NA_KH * GRID_W), lambda b, j: (0, j, 0, 0))],
        out_specs=pl.BlockSpec((seq_len, LANES), lambda b, j: (b, j)),
        out_shape=jax.ShapeDtypeStruct((n_seq * seq_len, NA_WIDTH), F32),
        compiler_params=_cparams(2),
        name="na_attention",
    )(proj, proj, proj, cache_k, cache_v, bias)


ADA_TN = 1024
IN_TM = 1024
IN_TN = 1536
OUT_TM = 512


def _mod_row_fn(tm, n_ctx_rows, latent_len):
    def f(i):
        start = i * tm
        return jnp.where(start < n_ctx_rows, 0, 1 + (start - n_ctx_rows) // latent_len)
    return f


def _ada_kernel(c_ref, w_ref, b_ref, o_ref):
    x = c_ref[...]
    o_ref[...] = _dot(x * _sigmoid(x), w_ref[...]) + b_ref[...]


def _ada(cond, w_ada, b_ada, layer):
    n_rows, d = cond.shape
    n = w_ada.shape[2]
    return pl.pallas_call(
        _ada_kernel,
        grid=(n // ADA_TN,),
        in_specs=[pl.BlockSpec((n_rows, d), lambda j: (0, 0)),
                  pl.BlockSpec((None, d, ADA_TN), lambda j: (layer, 0, j)),
                  pl.BlockSpec((None, 1, ADA_TN), lambda j: (layer, 0, j))],
        out_specs=pl.BlockSpec((n_rows, ADA_TN), lambda j: (0, j)),
        out_shape=jax.ShapeDtypeStruct((n_rows, n), F32),
        compiler_params=_cparams(1),
        name="ada_mod",
    )(cond, w_ada, b_ada.reshape(b_ada.shape[0], 1, n))


def _rms_mod(x, g, sc, sh):
    y = x * lax.rsqrt(jnp.mean(x * x, axis=-1, keepdims=True) + RMS_EPS) * g
    return y * (1.0 + sc) + sh


def _in_proj_kernel(x_ref, g_ref, sh_ref, sc_ref, w_ref, o_ref, h_scr):
    @pl.when(pl.program_id(1) == 0)
    def _():
        h_scr[...] = _rms_mod(x_ref[...], g_ref[...], sc_ref[...], sh_ref[...]).astype(BF16)

    o_ref[...] = jnp.dot(h_scr[...], w_ref[...], preferred_element_type=F32)


def _in_proj(x, norm_g, mod, w, mod_row):
    m, d = x.shape
    n = w.shape[1]
    row = mod_row(IN_TM)
    return pl.pallas_call(
        _in_proj_kernel,
        grid=(m // IN_TM, n // IN_TN),
        in_specs=[pl.BlockSpec((IN_TM, d), lambda i, j: (i, 0)),
                  pl.BlockSpec((1, d), lambda i, j: (0, 0)),
                  pl.BlockSpec((None, 1, d), lambda i, j: (row(i), 0, 0)),
                  pl.BlockSpec((None, 1, d), lambda i, j: (row(i), 0, 1)),
                  pl.BlockSpec((d, IN_TN), lambda i, j: (0, j))],
        out_specs=pl.BlockSpec((IN_TM, IN_TN), lambda i, j: (i, j)),
        out_shape=jax.ShapeDtypeStruct((m, n), F32),
        scratch_shapes=[pltpu.VMEM((IN_TM, d), BF16)],
        compiler_params=_cparams(2),
        name="in_proj",
    )(x, norm_g.reshape(1, d), mod, mod, w)


def _out_proj_kernel(na_ref, rw_ref, hg_ref, x_ref, w_ref, g1_ref, n2_ref, sh2_ref, sc2_ref, rtw_ref, rtb_ref,
                     xo_ref, h2_ref, idx_ref, gate_ref):
    o1, o2 = NA_WIDTH, NA_WIDTH + RW_WIDTH
    mix = _dot(na_ref[...], w_ref[0:o1]) + _dot(rw_ref[...], w_ref[o1:o2]) + _dot(hg_ref[...], w_ref[o2:])
    x = x_ref[...] + g1_ref[...] * mix
    xo_ref[...] = x
    h = _rms_mod(x, n2_ref[...], sc2_ref[...], sh2_ref[...])
    h_hi = h.astype(BF16)
    h2_ref[...] = h_hi
    h_mid = (h - h_hi.astype(F32)).astype(BF16)
    d = lambda a, b: jnp.dot(a, b, preferred_element_type=F32)
    logits = d(h_hi, rtw_ref[0]) + (d(h_hi, rtw_ref[1]) + d(h_mid, rtw_ref[0])) + rtb_ref[...]
    lane = lax.broadcasted_iota(jnp.int32, logits.shape, 1).astype(F32)
    idx = jnp.zeros_like(logits)
    vals = []
    for k in range(TOP_K):
        mx = jnp.max(logits, axis=-1, keepdims=True)
        sel = jnp.min(jnp.where(logits == mx, lane, float(LANES)), axis=-1, keepdims=True)
        idx = jnp.where(lane == k, sel, idx)
        vals.append(mx)
        logits = jnp.where(lane == sel, 3.0 * NEG_INF, logits)
    es = [jnp.exp(v - vals[0]) for v in vals]
    den = es[0] + es[1] + es[2] + es[3]
    gate = jnp.zeros_like(idx)
    for k in range(TOP_K):
        gate = jnp.where(lane == k, es[k] / den, gate)
    idx_ref[...] = idx.astype(jnp.int32)
    gate_ref[...] = gate


def _out_proj(na_o, rw_o, hg_o, x, w_out, mod, norm2_g, router_w, router_b, mod_row):
    m, d = x.shape
    tm = OUT_TM
    row = mod_row(tm)
    n_exp = router_w.shape[1]
    rtw = jnp.pad(router_w.astype(F32), ((0, 0), (0, LANES - n_exp)))
    rtw_hi = rtw.astype(BF16)
    rtw = jnp.stack([rtw_hi, (rtw - rtw_hi.astype(F32)).astype(BF16)])
    rtb = jnp.pad(router_b, (0, LANES - n_exp), constant_values=NEG_INF).reshape(1, LANES)
    tok = lambda w: pl.BlockSpec((tm, w), lambda i: (i, 0))
    mod_blk = lambda part: pl.BlockSpec((None, 1, d), lambda i: (row(i), 0, part))
    return pl.pallas_call(
        _out_proj_kernel,
        grid=(m // tm,),
        in_specs=[tok(NA_WIDTH), tok(RW_WIDTH), tok(HG_WIDTH), tok(d),
                  pl.BlockSpec((d, d), lambda i: (0, 0)),
                  mod_blk(2), pl.BlockSpec((1, d), lambda i: (0, 0)), mod_blk(3), mod_blk(4),
                  pl.BlockSpec((2, d, LANES), lambda i: (0, 0, 0)), pl.BlockSpec((1, LANES), lambda i: (0, 0))],
        out_specs=[tok(d), tok(d), tok(LANES), tok(LANES)],
        out_shape=[jax.ShapeDtypeStruct((m, d), F32), jax.ShapeDtypeStruct((m, d), BF16),
                   jax.ShapeDtypeStruct((m, LANES), jnp.int32), jax.ShapeDtypeStruct((m, LANES), F32)],
        compiler_params=_cparams(1),
        name="out_proj_router",
    )(na_o, rw_o, hg_o, x, w_out, mod, norm2_g.reshape(1, d), mod, mod, rtw, rtb)


def _final_norm_kernel(x_ref, g_ref, o_ref):
    x = x_ref[...]
    o_ref[...] = x * lax.rsqrt(jnp.mean(x * x, axis=-1, keepdims=True) + RMS_EPS) * g_ref[...]


def _final_norm(x, g):
    m, d = x.shape
    tm = TOKEN_TILE
    return pl.pallas_call(
        _final_norm_kernel,
        grid=(m // tm,),
        in_specs=[pl.BlockSpec((tm, d), lambda i: (i, 0)), pl.BlockSpec((1, d), lambda i: (0, 0))],
        out_specs=pl.BlockSpec((tm, d), lambda i: (i, 0)),
        out_shape=jax.ShapeDtypeStruct((m, d), F32),
        compiler_params=_cparams(1),
        name="final_norm",
    )(x, g.reshape(1, d))


MOE_TM = 512
GU_TN = 1024
DOWN_TN = 2048


def _moe_dispatch(top_e, gates, n_exp):
    n_tok, top_k = top_e.shape
    n_assign = n_tok * top_k
    i32 = jnp.int32
    flat_e = top_e.reshape(-1).astype(i32)
    assign = jnp.arange(n_assign, dtype=i32)
    experts = jnp.arange(n_exp, dtype=i32)
    lookup = lambda table, e: jnp.sum(jnp.where(e[:, None] == experts[None, :], table[None, :], 0), axis=1)
    sorted_e, order, gate_sorted = lax.sort((flat_e, assign, gates.reshape(-1)), num_keys=1)
    counts = jnp.sum((flat_e[:, None] == experts[None, :]).astype(i32), axis=0)
    padded = (counts + MOE_TM - 1) // MOE_TM * MOE_TM
    pad_end = jnp.cumsum(padded)
    pad_start = pad_end - padded
    start = jnp.cumsum(counts) - counts
    shift = pad_start - start
    slot_sorted = assign + lookup(shift, sorted_e)
    _, slot_of_assign = lax.sort((order, slot_sorted), num_keys=1)
    n_blocks = -(-n_assign // MOE_TM) + n_exp
    blk = jnp.arange(n_blocks, dtype=i32)
    block_expert = jnp.sum((pad_end[None, :] <= (blk * MOE_TM)[:, None]).astype(i32), axis=1)
    block_expert = jnp.minimum(block_expert, n_exp - 1)
    n_active = (pad_end[-1] // MOE_TM).astype(i32)
    first = blk * MOE_TM - lookup(shift, block_expert)
    last = lookup(start + counts, block_expert)
    lane = jnp.arange(MOE_TM, dtype=i32)[None, :]
    pos = first[:, None] + lane
    valid = (pos < last[:, None]) & (blk[:, None] < n_active)
    pos = jnp.clip(pos, 0, n_assign - 1)
    slot_token = jnp.where(valid, (order // top_k)[pos], (blk[:, None] * MOE_TM + lane) % n_tok).reshape(-1)
    slot_gate = jnp.where(valid, gate_sorted[pos], 0.0).reshape(-1, 1)
    last_used = jnp.sum(jnp.where(blk == jnp.maximum(n_active - 1, 0), block_expert, 0))
    block_expert = jnp.where(blk < n_active, block_expert, last_used)
    return slot_token, slot_gate, slot_of_assign, block_expert, n_active.reshape(1)


def _expert_changed(be_ref, m):
    return (m == 0) | (be_ref[m] != be_ref[jnp.maximum(m - 1, 0)])


def _moe_gu_kernel(be_ref, na_ref, x_ref, wg_ref, wu_ref, bg_ref, bu_ref, o_ref, wg_s, wu_s):
    m = pl.program_id(1)

    @pl.when(_expert_changed(be_ref, m))
    def _():
        wg_s[...] = wg_ref[...].astype(BF16)
        wu_s[...] = wu_ref[...].astype(BF16)

    @pl.when(m < na_ref[0])
    def _():
        x = x_ref[...]
        gate = jnp.dot(x, wg_s[...], preferred_element_type=F32) + bg_ref[...]
        up = jnp.dot(x, wu_s[...], preferred_element_type=F32) + bu_ref[...]
        gate = jnp.minimum(gate, SWIGLU_LIMIT)
        up = jnp.clip(up, -SWIGLU_LIMIT, SWIGLU_LIMIT)
        o_ref[...] = ((up + 1.0) * gate * _sigmoid(SWIGLU_ALPHA * gate)).astype(BF16)

    @pl.when(m >= na_ref[0])
    def _():
        o_ref[...] = jnp.zeros_like(o_ref)


def _moe_down_kernel(be_ref, na_ref, a_ref, w_ref, b_ref, sg_ref, o_ref, w_s):
    m = pl.program_id(1)

    @pl.when(_expert_changed(be_ref, m))
    def _():
        w_s[...] = w_ref[...].astype(BF16)

    @pl.when(m < na_ref[0])
    def _():
        o_ref[...] = (jnp.dot(a_ref[...], w_s[...], preferred_element_type=F32) + b_ref[...]) * sg_ref[...]

    @pl.when(m >= na_ref[0])
    def _():
        o_ref[...] = jnp.zeros_like(o_ref)


def _moe_experts(xb, slot_gate, block_expert, n_active, w_gu, b_gu, w_down, b_down, layer):
    n_slots, d = xb.shape
    n_blocks = n_slots // MOE_TM
    n_exp, _, two_f = w_gu.shape[1:]
    f = two_f // 2
    nj = f // GU_TN
    act = pl.pallas_call(
        _moe_gu_kernel,
        grid_spec=pltpu.PrefetchScalarGridSpec(
            num_scalar_prefetch=2,
            grid=(nj, n_blocks),
            in_specs=[pl.BlockSpec((MOE_TM, d), lambda j, m, be, na: (m, 0)),
                      pl.BlockSpec((None, None, d, GU_TN), lambda j, m, be, na: (layer, be[m], 0, j)),
                      pl.BlockSpec((None, None, d, GU_TN), lambda j, m, be, na: (layer, be[m], 0, nj + j)),
                      pl.BlockSpec((None, None, 1, GU_TN), lambda j, m, be, na: (layer, be[m], 0, j)),
                      pl.BlockSpec((None, None, 1, GU_TN), lambda j, m, be, na: (layer, be[m], 0, nj + j))],
            out_specs=pl.BlockSpec((MOE_TM, GU_TN), lambda j, m, be, na: (m, j)),
            scratch_shapes=[pltpu.VMEM((d, GU_TN), BF16), pltpu.VMEM((d, GU_TN), BF16)]),
        out_shape=jax.ShapeDtypeStruct((n_slots, f), BF16),
        compiler_params=_cparams(2),
        name="moe_gate_up",
    )(block_expert, n_active, xb, w_gu, w_gu, b_gu.reshape(b_gu.shape[0], n_exp, 1, two_f),
      b_gu.reshape(b_gu.shape[0], n_exp, 1, two_f))
    nd = d // DOWN_TN
    return pl.pallas_call(
        _moe_down_kernel,
        grid_spec=pltpu.PrefetchScalarGridSpec(
            num_scalar_prefetch=2,
            grid=(nd, n_blocks),
            in_specs=[pl.BlockSpec((MOE_TM, f), lambda j, m, be, na: (m, 0)),
                      pl.BlockSpec((None, None, f, DOWN_TN), lambda j, m, be, na: (layer, be[m], 0, j)),
                      pl.BlockSpec((None, None, 1, DOWN_TN), lambda j, m, be, na: (layer, be[m], 0, j)),
                      pl.BlockSpec((MOE_TM, 1), lambda j, m, be, na: (m, 0))],
            out_specs=pl.BlockSpec((MOE_TM, DOWN_TN), lambda j, m, be, na: (m, j)),
            scratch_shapes=[pltpu.VMEM((f, DOWN_TN), BF16)]),
        out_shape=jax.ShapeDtypeStruct((n_slots, d), F32),
        compiler_params=_cparams(2),
        name="moe_down",
    )(block_expert, n_active, act, w_down, b_down.reshape(b_down.shape[0], n_exp, 1, d), slot_gate)


GATHER_ROWS = 512
COMBINE_TM = 128
DMA_PRIORITIES = 2


def _gather_rows_kernel(idx_ref, nxt_ref, src_ref, o_ref, buf, sems):
    i = pl.program_id(0)
    n = pl.num_programs(0)
    slot = i % 2

    def copy(ids, s, t):
        return pltpu.make_async_copy(src_ref.at[pl.ds(ids[0, t], 1)], buf.at[s, pl.ds(t, 1)], sems.at[s])

    def issue(ids, s):
        def body(g, carry):
            for p in range(DMA_PRIORITIES):
                copy(ids, s, g * DMA_PRIORITIES + p).start(priority=p)
            return carry
        lax.fori_loop(0, GATHER_ROWS // DMA_PRIORITIES, body, 0, unroll=4)

    @pl.when(i == 0)
    def _():
        issue(idx_ref, 0)

    @pl.when(i + 1 < n)
    def _():
        issue(nxt_ref, 1 - slot)

    pltpu.make_async_copy(src_ref.at[pl.ds(0, GATHER_ROWS)], buf.at[slot], sems.at[slot]).wait()
    o_ref[...] = buf[slot]


def _as_row_tiles(a):
    return a.reshape(a.shape[0], a.shape[1] // LANES, LANES)


def _gather_rows(src, idx):
    n = idx.shape[0]
    nb = n // GATHER_ROWS
    ids = idx.reshape(nb, 1, GATHER_ROWS)
    smem_blk = lambda f: pl.BlockSpec((None, 1, GATHER_ROWS), f, memory_space=pltpu.SMEM)
    blk = (GATHER_ROWS,) + src.shape[1:]
    return pl.pallas_call(
        _gather_rows_kernel,
        grid=(nb,),
        in_specs=[smem_blk(lambda i: (i, 0, 0)), smem_blk(lambda i: (jnp.minimum(i + 1, nb - 1), 0, 0)),
                  pl.BlockSpec(memory_space=pl.ANY)],
        out_specs=pl.BlockSpec(blk, lambda i: (i, 0, 0)),
        out_shape=jax.ShapeDtypeStruct((n,) + src.shape[1:], src.dtype),
        scratch_shapes=[pltpu.VMEM((2,) + blk, src.dtype), pltpu.SemaphoreType.DMA((2,))],
        compiler_params=_cparams(1),
        name="moe_dispatch_gather",
    )(ids, ids, src)


def _combine_kernel(idx_ref, nxt_ref, yb_ref, x_ref, g2_ref, o_ref, buf, acc, sems, *, top_k):
    i = pl.program_id(0)
    n = pl.num_programs(0)
    slot = i % 2

    def copy(ids, s, t, k):
        return pltpu.make_async_copy(yb_ref.at[pl.ds(ids[0, t * top_k + k], 1)], buf.at[s, k, pl.ds(t, 1)], sems.at[s])

    def issue(ids, s):
        def body(t, carry):
            for k in range(top_k):
                copy(ids, s, t, k).start(priority=k % DMA_PRIORITIES)
            return carry
        lax.fori_loop(0, COMBINE_TM, body, 0, unroll=2)

    @pl.when(i == 0)
    def _():
        issue(idx_ref, 0)

    @pl.when(i + 1 < n)
    def _():
        issue(nxt_ref, 1 - slot)

    for k in range(top_k):
        pltpu.make_async_copy(yb_ref.at[pl.ds(0, COMBINE_TM)], buf.at[slot, k], sems.at[slot]).wait()

    total = buf[slot, 0]
    for k in range(1, top_k):
        total = total + buf[slot, k]
    acc[...] = total
    for c in range(acc.shape[1]):
        cs = slice(c * LANES, (c + 1) * LANES)
        o_ref[:, cs] = x_ref[:, cs] + g2_ref[:, cs] * acc[:, c, :]


def _moe_combine(yb, slot_of_assign, x, mod, mod_row, top_k):
    m, d = x.shape
    tm = COMBINE_TM
    nt = m // tm
    row = mod_row(tm)
    ids = slot_of_assign.reshape(nt, 1, tm * top_k)
    smem_blk = lambda f: pl.BlockSpec((None, 1, tm * top_k), f, memory_space=pltpu.SMEM)
    return pl.pallas_call(
        functools.partial(_combine_kernel, top_k=top_k),
        grid=(nt,),
        in_specs=[smem_blk(lambda i: (i, 0, 0)), smem_blk(lambda i: (jnp.minimum(i + 1, nt - 1), 0, 0)),
                  pl.BlockSpec(memory_space=pl.ANY),
                  pl.BlockSpec((tm, d), lambda i: (i, 0)),
                  pl.BlockSpec((None, 1, d), lambda i: (row(i), 0, 5))],
        out_specs=pl.BlockSpec((tm, d), lambda i: (i, 0)),
        out_shape=jax.ShapeDtypeStruct((m, d), F32),
        scratch_shapes=[pltpu.VMEM((2, top_k, tm) + yb.shape[1:], F32), pltpu.VMEM((tm,) + yb.shape[1:], F32),
                        pltpu.SemaphoreType.DMA((2,))],
        compiler_params=_cparams(1),
        name="moe_combine",
    )(ids, ids, yb, x, mod)


def _moe(h2, idx, gate, x, mod, mod_row, w_gu, b_gu, w_down, b_down, layer):
    n_exp = w_gu.shape[1]
    slot_token, slot_gate, slot_of_assign, block_expert, n_active = _moe_dispatch(idx[:, :TOP_K], gate[:, :TOP_K], n_exp)
    xb = _gather_rows(_as_row_tiles(h2), slot_token).reshape(slot_token.shape[0], h2.shape[1])
    yb = _moe_experts(xb, slot_gate, block_expert, n_active, w_gu, b_gu, w_down, b_down, layer)
    return _moe_combine(_as_row_tiles(yb), slot_of_assign, x, mod, mod_row, TOP_K)


def _pad_w_in(w):
    wb = w.astype(BF16)
    cut = 3 * NA_WIDTH + RW_SHIFT_WIDTH
    na = 3 * NA_WIDTH
    return jnp.concatenate([wb[:, na:cut], jnp.zeros((w.shape[0], RW_PAD), BF16), wb[:, :na], wb[:, cut:]], axis=1)


def kernel(x_prompt, x_sample, c, cache_k, cache_v, state_rwkv_fwd, state_rwkv_bwd, state_hgrn_fwd, state_hgrn_bwd, c_ctx, norm1_g, norm2_g, w_ada, b_ada, w_in, w_out, na_rpb, rw_mu, rw_w0, rw_w2, rw_a0, rw_a2, rw_g2, rw_k_k, rw_k_a, rw_r_k, rw_ln_g, rw_ln_b, hg_lb_raw, hg_norm_g, router_w, router_b, exp_w_gu, exp_b_gu, exp_w_down, exp_b_down, final_g):
    n_p, t_p, d = x_prompt.shape
    n_s, t_s, _ = x_sample.shape
    depth = w_in.shape[0]
    mp, ms = n_p * t_p, n_s * t_s
    x = jnp.concatenate([x_prompt.reshape(mp, d), x_sample.reshape(ms, d)], axis=0)
    n_mod_rows = 8
    cond = jnp.concatenate([c_ctx[None], c, jnp.zeros((n_mod_rows - 1 - n_s, d), F32)], axis=0)
    mod_row = lambda tm: _mod_row_fn(tm, mp, t_s)
    p_lb = jax.nn.softmax(hg_lb_raw.astype(F32), axis=0)
    hg_lb = jnp.cumsum(p_lb, axis=0) - p_lb[:1]
    ctx_out = []
    for l in range(depth):
        mod2d = _ada(cond, w_ada, b_ada, l)
        mod = mod2d.reshape(n_mod_rows, 1, 6 * d)
        proj = _in_proj(x, norm1_g[l], mod, _pad_w_in(w_in[l]), mod_row)
        na_p, k_ctx, v_ctx = _ctx_attention(proj, n_p, t_p)
        na_s = _na_attention(proj, mp, n_s, t_s, cache_k, cache_v, l, _na_bias_table(na_rpb[l], t_s // GRID_W))
        na_o = jnp.concatenate([na_p, na_s], axis=0)
        lw = dict(rw_mu=rw_mu[l], rw_w0=rw_w0[l], rw_w2=rw_w2[l], rw_a0=rw_a0[l], rw_a2=rw_a2[l], rw_g2=rw_g2[l],
                  rw_k_k=rw_k_k[l], rw_k_a=rw_k_a[l], rw_r_k=rw_r_k[l], rw_ln_g=rw_ln_g[l], rw_ln_b=rw_ln_b[l])
        rw_groups = [(0, n_p, t_p, jnp.zeros((2, n_p, RW_HEADS, RW_HEAD_DIM, RW_HEAD_DIM), F32)),
                     (mp, n_s, t_s, jnp.stack([state_rwkv_fwd[:, l], state_rwkv_bwd[:, l]]))]
        rw_o, rw_states = _rwkv_mixer(proj, rw_groups, lw)
        hg_groups = [(0, n_p, t_p, jnp.zeros((2, n_p, HG_HEADS, HG_KEY_DIM, HG_VAL_DIM), F32)),
                     (mp, n_s, t_s, jnp.stack([state_hgrn_fwd[:, l], state_hgrn_bwd[:, l]]))]
        hg_o, hg_states = _hgrn_mixer(proj, hg_groups, hg_lb[l], hg_norm_g[l])
        x, h2, idx, gate = _out_proj(na_o, rw_o, hg_o, x, w_out[l].astype(BF16), mod, norm2_g[l], router_w[l],
                                     router_b[l], mod_row)
        x = _moe(h2, idx, gate, x, mod, mod_row, exp_w_gu, exp_b_gu, exp_w_down, exp_b_down, l)
        ctx_out.append((k_ctx, v_ctx, rw_states[0][0], rw_states[0][1], hg_states[0][0], hg_states[0][1]))
    y = _final_norm(x, final_g)
    outs = [y[:mp].reshape(n_p, t_p, d), y[mp:].reshape(n_s, t_s, d)]
    outs += [jnp.stack([t[i] for t in ctx_out], axis=1) for i in range(6)]
    return tuple(outs)
```
